```python
import math
import jax
import jax.numpy as jnp
from jax import lax
import numpy as np

D_MODEL = 2048
BATCH = 4
SEQ = 4096
DEPTH = 4

CTX_LEN = 256
GRID_W = 64
N_MIXERS = 3
NORM_EPS = 1e-6
ROPE_THETA = 10000.0

SWA_HEADS = 32
SWA_KV_HEADS = 4
SWA_HEAD_DIM = 64
WINDOW = 128
WIN_BLOCK = 128

SSD_INNER = 2 * D_MODEL
SSD_HEAD_DIM = 64
SSD_HEADS = SSD_INNER // SSD_HEAD_DIM
SSD_GROUPS = 8
SSD_STATE = 128
SSD_CONV = 5
SSD_CHUNK = 128
SSD_CONV_DIM = SSD_INNER + 2 * SSD_GROUPS * SSD_STATE

GA_HEADS = 16
GA_KV_HEADS = 4
GA_HEAD_DIM = 128
GA_Q_BLOCK = 128

N_EXPERTS = 32
TOP_K = 4
EXPERT_FF = 896
SWIGLU_ALPHA = 1.702
SWIGLU_LIMIT = 7.0
MOE_BLOCK = 256

kernel_name = 'hybrid_interleaved_swa_ssd_axialattn_moe_dit'

F32 = jnp.float32


def rms_norm(x, g):
    xf = x.astype(F32)
    y = xf * lax.rsqrt(jnp.mean(xf * xf, axis=-1, keepdims=True) + NORM_EPS)
    return (y * g.astype(F32)).astype(x.dtype)


def modulate(h, shift, scale):
    return h * (1.0 + scale) + shift


def axial_rope_tables(pos_row, pos_col, head_dim):
    q4 = head_dim // 4
    inv = ROPE_THETA ** (-jnp.arange(q4, dtype=F32) / q4)
    ang = jnp.stack([pos_row.astype(F32)[:, None] * inv, pos_col.astype(F32)[:, None] * inv], axis=1)
    return jnp.cos(ang), jnp.sin(ang)


def apply_rope(x, cos, sin):
    lead = x.shape[:-1]
    q4 = x.shape[-1] // 4
    xr = x.astype(F32).reshape(lead + (2, 2, q4))
    x1, x2 = xr[..., 0, :], xr[..., 1, :]
    bshape = (cos.shape[0],) + (1,) * (x.ndim - 3) + cos.shape[1:]
    cs, sn = cos.reshape(bshape), sin.reshape(bshape)
    out = jnp.stack([x1 * cs - x2 * sn, x2 * cs + x1 * sn], axis=-2)
    return out.reshape(x.shape).astype(x.dtype)


def qkv_project(h, w_in, q_gain, k_gain, n_q, n_kv, hd, rope):
    bsz, n, _ = h.shape
    q, k, v = jnp.split(h @ w_in, [n_q * hd, (n_q + n_kv) * hd], axis=-1)
    q = rms_norm(q.reshape(bsz, n, n_kv, n_q // n_kv, hd), q_gain)
    k = rms_norm(k.reshape(bsz, n, n_kv, hd), k_gain)
    v = v.reshape(bsz, n, n_kv, hd)
    if rope is not None:
        q, k = apply_rope(q, *rope), apply_rope(k, *rope)
    return q * (hd ** -0.5), k, v


def attend(q, keys, vals, mask=None, sink=None):
    logits = [jnp.einsum('bqkrd,blkd->bkrql', q, k).astype(F32) for k in keys]
    if mask is not None:
        logits[0] = jnp.where(mask, logits[0], -jnp.inf)
    if sink is not None:
        logits.append(jnp.broadcast_to(sink.astype(F32)[None, :, :, None, None], logits[0].shape[:-1] + (1,)))
    cuts = [int(s) for s in np.cumsum([l.shape[-1] for l in logits])[:-1]]
    probs = jnp.split(jax.nn.softmax(jnp.concatenate(logits, axis=-1), axis=-1), cuts, axis=-1)
    return sum(jnp.einsum('bkrql,blkd->bqkrd', p.astype(v.dtype), v) for p, v in zip(probs, vals))


def window_attention(h_c, h_l, w_in, q_gain, k_gain, sink, w_out, rope, need_ctx):
    bsz, seq, _ = h_l.shape
    qc, kc, vc = qkv_project(h_c, w_in, q_gain, k_gain, SWA_HEADS, SWA_KV_HEADS, SWA_HEAD_DIM, None)
    ql, kl, vl = qkv_project(h_l, w_in, q_gain, k_gain, SWA_HEADS, SWA_KV_HEADS, SWA_HEAD_DIM, rope)
    sink = sink.reshape(SWA_KV_HEADS, SWA_HEADS // SWA_KV_HEADS)
    nb = seq // WIN_BLOCK
    span = WIN_BLOCK + 2 * WINDOW
    pad = ((0, 0), (WINDOW, WINDOW), (0, 0), (0, 0))
    k_pad, v_pad = jnp.pad(kl, pad), jnp.pad(vl, pad)
    q_blocks = jnp.moveaxis(ql.reshape((bsz, nb, WIN_BLOCK) + ql.shape[2:]), 1, 0)
    offs_q = jnp.arange(WIN_BLOCK)
    offs_k = jnp.arange(span) - WINDOW

    def one_block(args):
        qb, start = args
        kw = lax.dynamic_slice_in_dim(k_pad, start, span, axis=1)
        vw = lax.dynamic_slice_in_dim(v_pad, start, span, axis=1)
        qpos, kpos = start + offs_q, start + offs_k
        band = (jnp.abs(qpos[:, None] - kpos[None, :]) <= WINDOW) & ((kpos >= 0) & (kpos < seq))[None, :]
        return attend(qb, [kw, kc], [vw, vc], mask=band, sink=sink)

    y_l = lax.map(one_block, (q_blocks, jnp.arange(nb) * WIN_BLOCK))
    y_l = jnp.moveaxis(y_l, 0, 1).reshape(bsz, seq, -1) @ w_out
    y_c = None
    if need_ctx:
        y_c = attend(qc, [kc], [vc], sink=sink).reshape(bsz, h_c.shape[1], -1) @ w_out
    return y_c, y_l


def global_attention(h_c, h_l, w_in, q_gain, k_gain, w_out, rope, need_ctx):
    bsz, seq, _ = h_l.shape
    qc, kc, vc = qkv_project(h_c, w_in, q_gain, k_gain, GA_HEADS, GA_KV_HEADS, GA_HEAD_DIM, None)
    ql, kl, vl = qkv_project(h_l, w_in, q_gain, k_gain, GA_HEADS, GA_KV_HEADS, GA_HEAD_DIM, rope)
    nb = seq // GA_Q_BLOCK
    q_blocks = jnp.moveaxis(ql.reshape((bsz, nb, GA_Q_BLOCK) + ql.shape[2:]), 1, 0)
    y_l = lax.map(lambda qb: attend(qb, [kl, kc], [vl, vc]), q_blocks)
    y_l = jnp.moveaxis(y_l, 0, 1).reshape(bsz, seq, -1) @ w_out
    y_c = None
    if need_ctx:
        y_c = attend(qc, [kc], [vc]).reshape(bsz, h_c.shape[1], -1) @ w_out
    return y_c, y_l


def segsum(a):
    t = a.shape[-1]
    strict = jnp.tril(jnp.ones((t, t), bool), -1)
    incl = jnp.tril(jnp.ones((t, t), bool), 0)
    cs = jnp.cumsum(jnp.where(strict, a[..., :, None], 0.0), axis=-2)
    return jnp.where(incl, cs, -jnp.inf)


def ssd_scan(x, dt, a, bm, cm, init_state, with_output):
    bsz, n, n_h, p = x.shape
    g, ns = bm.shape[2], bm.shape[3]
    r = n_h // g
    nc = n // SSD_CHUNK
    xdt = (x * dt[..., None]).reshape(bsz, nc, SSD_CHUNK, g, r, p)
    adt = (dt * a).reshape(bsz, nc, SSD_CHUNK, g, r).transpose(0, 3, 4, 1, 2)
    bm = bm.reshape(bsz, nc, SSD_CHUNK, g, ns)
    cm = cm.reshape(bsz, nc, SSD_CHUNK, g, ns)
    a_cs = jnp.cumsum(adt, axis=-1)
    decay_states = jnp.exp(a_cs[..., -1:] - a_cs)
    states = jnp.einsum('bclgn,bgrcl,bclgrp->bcgrpn', bm, decay_states, xdt)
    states = jnp.concatenate([init_state[:, None], states], axis=1)
    chunk_tot = jnp.pad(a_cs[..., -1], ((0, 0), (0, 0), (0, 0), (1, 0)))
    decay_chunk = jnp.exp(segsum(chunk_tot))
    new_states = jnp.einsum('bgrzc,bcgrpn->bzgrpn', decay_chunk, states)
    final = new_states[:, -1]
    if not with_output:
        return None, final
    cb = jnp.einsum('bclgn,bcsgn->bcgls', cm, bm)
    lmat = jnp.exp(segsum(adt))
    y_diag = jnp.einsum('bcgls,bgrcls,bcsgrp->bclgrp', cb, lmat, xdt)
    y_off = jnp.einsum('bclgn,bcgrpn,bgrcl->bclgrp', cm, new_states[:, :-1], jnp.exp(a_cs))
    return (y_diag + y_off).reshape(bsz, n, n_h, p), final


def centred_dwconv(u, w, b):
    k, ch = w.shape
    out = lax.conv_general_dilated(u, w[:, None, :].astype(u.dtype), window_strides=(1,),
                                   padding=[(k // 2, k // 2)], dimension_numbers=('NWC', 'WIO', 'NWC'),
                                   feature_group_count=ch)
    return out + b


def bidir_ssd(h_c, h_l, w_in, conv_w, conv_b, dt_bias, a_log, d_skip, norm_g, w_out, need_ctx):
    a = -jnp.exp(a_log.astype(F32))

    def prepare(h):
        bsz, n, _ = h.shape
        z, xbc, dt = jnp.split(h @ w_in, [SSD_INNER, SSD_INNER + SSD_CONV_DIM], axis=-1)
        xbc = jax.nn.silu(centred_dwconv(xbc, conv_w, conv_b))
        xs, bm, cm = jnp.split(xbc, [SSD_INNER, SSD_INNER + SSD_GROUPS * SSD_STATE], axis=-1)
        dt = jax.nn.softplus(dt.astype(F32).reshape(bsz, n, 2, SSD_HEADS) + dt_bias.astype(F32))
        return (z, xs.reshape(bsz, n, SSD_HEADS, SSD_HEAD_DIM).astype(F32),
                bm.reshape(bsz, n, SSD_GROUPS, SSD_STATE).astype(F32),
                cm.reshape(bsz, n, SSD_GROUPS, SSD_STATE).astype(F32), dt)

    def finish(y, z):
        bsz, n = y.shape[:2]
        y = y.reshape(bsz, n, SSD_INNER) * jax.nn.silu(z.astype(F32))
        y = rms_norm(y.reshape(bsz, n, SSD_GROUPS, -1), norm_g.reshape(SSD_GROUPS, -1))
        return y.reshape(bsz, n, SSD_INNER).astype(h_l.dtype) @ w_out

    zc, xc, bc, cc, dtc = prepare(h_c)
    zl, xl, bl, cl, dtl = prepare(h_l)
    d_skip = d_skip.astype(F32)[:, None]
    y_l = d_skip * xl
    y_c = d_skip * xc if need_ctx else None
    bsz = h_l.shape[0]
    r = SSD_HEADS // SSD_GROUPS
    for direction in range(2):
        f = (lambda t: jnp.flip(t, axis=1)) if direction else (lambda t: t)
        init = jnp.zeros((bsz, SSD_GROUPS, r, SSD_HEAD_DIM, SSD_STATE), F32)
        yc_d, ctx_state = ssd_scan(f(xc), f(dtc[:, :, direction]), a[direction], f(bc), f(cc), init, need_ctx)
        yl_d, _ = ssd_scan(f(xl), f(dtl[:, :, direction]), a[direction], f(bl), f(cl), ctx_state, True)
        y_l = y_l + f(yl_d)
        if need_ctx:
            y_c = y_c + f(yc_d)
    return (finish(y_c, zc) if need_ctx else None), finish(y_l, zl)


def moe_ffn(h, w_router, b_router, w_gu, b_gu, w_down, b_down):
    n_tok, d = h.shape
    n_exp = w_router.shape[-1]
    n_asg = n_tok * TOP_K
    logits = (h @ w_router + b_router).astype(F32)
    top_val, top_idx = lax.top_k(logits, TOP_K)
    gates = jax.nn.softmax(top_val, axis=-1)
    e_flat = top_idx.reshape(-1)
    order = jnp.argsort(e_flat)
    e_sorted = e_flat[order]
    tok_sorted = order // TOP_K
    gate_sorted = gates.reshape(-1)[order]
    counts = jnp.bincount(e_flat, length=n_exp)
    padded = (counts + MOE_BLOCK - 1) // MOE_BLOCK * MOE_BLOCK
    pad_end = jnp.cumsum(padded)
    pad_start = pad_end - padded
    grp_start = jnp.cumsum(counts) - counts
    dest = pad_start[e_sorted] + jnp.arange(n_asg) - grp_start[e_sorted]
    n_blocks = -(-n_asg // MOE_BLOCK) + n_exp
    rows = jnp.zeros((n_blocks * MOE_BLOCK, d), h.dtype).at[dest].set(h[tok_sorted])
    block_exp = jnp.minimum(jnp.searchsorted(pad_end, jnp.arange(n_blocks) * MOE_BLOCK, side='right'), n_exp - 1)

    def expert_block(args):
        xb, e = args
        gu = xb @ w_gu[e] + b_gu[e]
        glu, lin = jnp.split(gu, 2, axis=-1)
        glu = jnp.minimum(glu, SWIGLU_LIMIT)
        lin = jnp.clip(lin, -SWIGLU_LIMIT, SWIGLU_LIMIT)
        act = glu * jax.nn.sigmoid(SWIGLU_ALPHA * glu) * (lin + 1.0)
        return act @ w_down[e] + b_down[e]

    out_rows = lax.map(expert_block, (rows.reshape(n_blocks, MOE_BLOCK, d), block_exp)).reshape(-1, d)
    y = out_rows[dest] * gate_sorted[:, None].astype(out_rows.dtype)
    return jax.ops.segment_sum(y, tok_sorted, num_segments=n_tok)


def setup_inputs(seed: int = 0) -> dict:
    key = jax.random.key(seed)
    keys = iter(jax.random.split(key, 48))

    def normal(shape, scale):
        return jax.random.normal(next(keys), shape, jnp.float32) * scale

    def gain(shape):
        return 1.0 + normal(shape, 0.02)

    d = D_MODEL
    n_a = len(range(0, DEPTH, N_MIXERS))
    n_b = len(range(1, DEPTH, N_MIXERS))
    n_c = len(range(2, DEPTH, N_MIXERS))
    swa_cols = (SWA_HEADS + 2 * SWA_KV_HEADS) * SWA_HEAD_DIM
    ga_cols = (GA_HEADS + 2 * GA_KV_HEADS) * GA_HEAD_DIM
    ssd_cols = SSD_INNER + SSD_CONV_DIM + 2 * SSD_HEADS
    dt0 = jnp.exp(jax.random.uniform(next(keys), (n_b, 2, SSD_HEADS), jnp.float32,
                                     minval=math.log(1e-3), maxval=math.log(1e-1)))
    a_log = jnp.log(jax.random.uniform(next(keys), (n_b, 2, SSD_HEADS), jnp.float32, minval=1.0, maxval=16.0))
    return {
        'x': normal((BATCH, SEQ, d), 1.0),
        'c': normal((BATCH, d), 1.0),
        'ctx': normal((BATCH, CTX_LEN, d), 1.0),
        'c_ctx': normal((d,), 1.0),
        'ada_w': normal((DEPTH, d, 6 * d), 0.5 * d ** -0.5),
        'ada_b': normal((DEPTH, 6 * d), 0.02),
        'norm_mix': gain((DEPTH, d)),
        'norm_ffn': gain((DEPTH, d)),
        'swa_w_in': normal((n_a, d, swa_cols), d ** -0.5),
        'swa_q_norm': gain((n_a, SWA_HEAD_DIM)),
        'swa_k_norm': gain((n_a, SWA_HEAD_DIM)),
        'swa_sink': normal((n_a, SWA_HEADS), 1.0),
        'swa_w_out': normal((n_a, SWA_HEADS * SWA_HEAD_DIM, d), (SWA_HEADS * SWA_HEAD_DIM) ** -0.5),
        'ssd_w_in': normal((n_b, d, ssd_cols), d ** -0.5),
        'ssd_conv_w': normal((n_b, SSD_CONV, SSD_CONV_DIM), SSD_CONV ** -0.5),
        'ssd_conv_b': normal((n_b, SSD_CONV_DIM), 0.02),
        'ssd_dt_bias': dt0 + jnp.log(-jnp.expm1(-dt0)),
        'ssd_a_log': a_log,
        'ssd_d': gain((n_b, SSD_HEADS)),
        'ssd_norm': gain((n_b, SSD_INNER)),
        'ssd_w_out': normal((n_b, SSD_INNER, d), SSD_INNER ** -0.5),
        'ga_w_in': normal((n_c, d, ga_cols), d ** -0.5),
        'ga_q_norm': gain((n_c, GA_HEAD_DIM)),
        'ga_k_norm': gain((n_c, GA_HEAD_DIM)),
        'ga_w_out': normal((n_c, GA_HEADS * GA_HEAD_DIM, d), (GA_HEADS * GA_HEAD_DIM) ** -0.5),
        'moe_w_router': normal((DEPTH, d, N_EXPERTS), d ** -0.5),
        'moe_b_router': normal((DEPTH, N_EXPERTS), 0.01),
        'moe_w_gate_up': normal((DEPTH, N_EXPERTS, d, 2 * EXPERT_FF), d ** -0.5),
        'moe_b_gate_up': normal((DEPTH, N_EXPERTS, 2 * EXPERT_FF), 0.02),
        'moe_w_down': normal((DEPTH, N_EXPERTS, EXPERT_FF, d), EXPERT_FF ** -0.5),
        'moe_b_down': normal((DEPTH, N_EXPERTS, d), 0.02),
    }


def reference(x, c, ctx, c_ctx, ada_w, ada_b, norm_mix, norm_ffn,
              swa_w_in, swa_q_norm, swa_k_norm, swa_sink, swa_w_out,
              ssd_w_in, ssd_conv_w, ssd_conv_b, ssd_dt_bias, ssd_a_log, ssd_d, ssd_norm, ssd_w_out,
              ga_w_in, ga_q_norm, ga_k_norm, ga_w_out,
              moe_w_router, moe_b_router, moe_w_gate_up, moe_b_gate_up, moe_w_down, moe_b_down):
    bsz, seq, d = x.shape
    n_ctx = ctx.shape[1]
    rows = seq // GRID_W
    pos_row = jnp.repeat(jnp.arange(rows), GRID_W)
    pos_col = jnp.tile(jnp.arange(GRID_W), rows)
    rope_swa = axial_rope_tables(pos_row, pos_col, SWA_HEAD_DIM)
    rope_ga = axial_rope_tables(pos_row, pos_col, GA_HEAD_DIM)
    act_c = jax.nn.silu(c)
    act_cc = jax.nn.silu(c_ctx)
    for i in range(DEPTH):
        kind, j = i % N_MIXERS, i // N_MIXERS
        need_ctx = i < DEPTH - 1
        mod_l = [m[:, None, :] for m in jnp.split(act_c @ ada_w[i] + ada_b[i], 6, axis=-1)]
        mod_c = jnp.split(act_cc @ ada_w[i] + ada_b[i], 6, axis=-1)
        h_l = modulate(rms_norm(x, norm_mix[i]), mod_l[0], mod_l[1])
        h_c = modulate(rms_norm(ctx, norm_mix[i]), mod_c[0], mod_c[1])
        if kind == 0:
            y_c, y_l = window_attention(h_c, h_l, swa_w_in[j], swa_q_norm[j], swa_k_norm[j], swa_sink[j],
                                        swa_w_out[j], rope_swa, need_ctx)
        elif kind == 1:
            y_c, y_l = bidir_ssd(h_c, h_l, ssd_w_in[j], ssd_conv_w[j], ssd_conv_b[j], ssd_dt_bias[j],
                                 ssd_a_log[j], ssd_d[j], ssd_norm[j], ssd_w_out[j], need_ctx)
        else:
            y_c, y_l = global_attention(h_c, h_l, ga_w_in[j], ga_q_norm[j], ga_k_norm[j], ga_w_out[j],
                                        rope_ga, need_ctx)
        x = x + mod_l[2] * y_l
        if need_ctx:
            ctx = ctx + mod_c[2] * y_c
        h_l = modulate(rms_norm(x, norm_ffn[i]), mod_l[3], mod_l[4])
        if need_ctx:
            h_c = modulate(rms_norm(ctx, norm_ffn[i]), mod_c[3], mod_c[4])
            tokens = jnp.concatenate([h_c, h_l], axis=1).reshape(-1, d)
            f = moe_ffn(tokens, moe_w_router[i], moe_b_router[i], moe_w_gate_up[i], moe_b_gate_up[i],
                        moe_w_down[i], moe_b_down[i]).reshape(bsz, n_ctx + seq, d)
            ctx = ctx + mod_c[5] * f[:, :n_ctx]
            x = x + mod_l[5] * f[:, n_ctx:]
        else:
            f = moe_ffn(h_l.reshape(-1, d), moe_w_router[i], moe_b_router[i], moe_w_gate_up[i],
                        moe_b_gate_up[i], moe_w_down[i], moe_b_down[i]).reshape(bsz, seq, d)
            x = x + mod_l[5] * f
    return x
```

```python
import functools
import math

import jax
import jax.numpy as jnp
import numpy as np
from jax import lax
from jax.experimental import pallas as pl
from jax.experimental.pallas import tpu as pltpu

F32 = jnp.float32
BF16 = jnp.bfloat16
I32 = jnp.int32

DEPTH = 4
N_MIXERS = 3
GRID_W = 64
NORM_EPS = 1e-6
ROPE_THETA = 10000.0

SWA_HEADS, SWA_KV_HEADS, SWA_HEAD_DIM, WINDOW = 32, 4, 64, 128
SSD_HEAD_DIM, SSD_GROUPS, SSD_STATE, SSD_CONV, SSD_CHUNK = 64, 8, 128, 5, 128
GA_HEADS, GA_KV_HEADS, GA_HEAD_DIM = 16, 4, 128
N_EXPERTS, TOP_K, EXPERT_FF = 32, 4, 896
SWIGLU_ALPHA, SWIGLU_LIMIT = 1.702, 7.0
MOE_BLOCK = 256

LANES = 128
ROW_TILE = 512
NEG_BIG = -1e30
VMEM_LIMIT = 56 * 1024 * 1024


def _cparams(*sem):
    return pltpu.CompilerParams(dimension_semantics=sem, vmem_limit_bytes=VMEM_LIMIT)


def _adaln_kernel(c_ref, w_ref, b_ref, o_ref):
    a = c_ref[...]
    a = (a * jax.nn.sigmoid(a)).astype(BF16)
    o_ref[0] = jnp.dot(a, w_ref[0].astype(BF16), preferred_element_type=F32) + b_ref[0]


def adaln_all(cvec, ada_w, ada_b):
    n_layers, d, n = ada_w.shape
    tn = 1024
    return pl.pallas_call(
        _adaln_kernel,
        grid=(n_layers, n // tn),
        in_specs=[pl.BlockSpec((8, d), lambda l, j: (0, 0)),
                  pl.BlockSpec((1, d, tn), lambda l, j: (l, 0, j)),
                  pl.BlockSpec((1, 1, tn), lambda l, j: (l, 0, j))],
        out_specs=pl.BlockSpec((1, 8, tn), lambda l, j: (l, 0, j)),
        out_shape=jax.ShapeDtypeStruct((n_layers, 8, n), F32),
        compiler_params=_cparams("parallel", "parallel"),
        name="adaln",
    )(cvec, ada_w, ada_b.reshape(n_layers, 1, n))


def _norm_modulate(x_ref, g_ref, sh_ref, sc_ref):
    x = x_ref[...]
    ms = jnp.mean(x * x, axis=-1, keepdims=True)
    y = x * lax.rsqrt(ms + NORM_EPS) * g_ref[...]
    return y * (1.0 + sc_ref[0]) + sh_ref[0]


def _group_of_tile(i, tiles_per_seq, n_batch):
    return jnp.minimum(i // tiles_per_seq, n_batch)


def _qkv_kernel(x_ref, g_ref, sh_ref, sc_ref, w_ref, gain_ref, gmat_ref, cos_ref, sa_ref, sb_ref,
                o_ref, h_ref, *, n_qk_tiles, head_dim):
    j = pl.program_id(1)

    @pl.when(j == 0)
    def _():
        h_ref[...] = _norm_modulate(x_ref, g_ref, sh_ref, sc_ref).astype(BF16)

    acc = jnp.dot(h_ref[...], w_ref[...], preferred_element_type=F32)

    @pl.when(j < n_qk_tiles)
    def _():
        ss = jnp.dot((acc * acc).astype(BF16), gmat_ref[...], preferred_element_type=F32)
        yn = acc * lax.rsqrt(ss * (1.0 / head_dim) + NORM_EPS) * gain_ref[...]
        q4 = head_dim // 4
        cos, sa, sb = cos_ref[...], sa_ref[...], sb_ref[...]
        for c in range(acc.shape[1] // LANES):
            yc = yn[:, c * LANES:(c + 1) * LANES]
            up = pltpu.roll(yc, LANES - q4, axis=1)
            dn = pltpu.roll(yc, q4, axis=1)
            o_ref[:, c * LANES:(c + 1) * LANES] = (yc * cos + up * sa + dn * sb).astype(o_ref.dtype)

    @pl.when(j >= n_qk_tiles)
    def _():
        o_ref[...] = acc.astype(o_ref.dtype)


def qkv_project(x, g, shift, scale, w, gain_row, gmat, rope_tabs, *, n_qk_cols, head_dim, seq, n_batch):
    t, d = x.shape
    n = w.shape[1]
    tm, tn = ROW_TILE, 256
    tiles_per_seq = seq // tm
    n_lat_tiles = n_batch * tiles_per_seq

    def tab_map(i, j):
        return (jnp.where(i < n_lat_tiles, i % tiles_per_seq, tiles_per_seq), 0)

    def mod_map(i, j):
        return (_group_of_tile(i, tiles_per_seq, n_batch), 0, 0)

    tab_spec = pl.BlockSpec((tm, LANES), tab_map)
    return pl.pallas_call(
        functools.partial(_qkv_kernel, n_qk_tiles=n_qk_cols // tn, head_dim=head_dim),
        grid=(t // tm, n // tn),
        in_specs=[pl.BlockSpec((tm, d), lambda i, j: (i, 0)),
                  pl.BlockSpec((1, d), lambda i, j: (0, 0)),
                  pl.BlockSpec((1, 1, d), mod_map),
                  pl.BlockSpec((1, 1, d), mod_map),
                  pl.BlockSpec((d, tn), lambda i, j: (0, j)),
                  pl.BlockSpec((1, tn), lambda i, j: (0, j)),
                  pl.BlockSpec((tn, tn), lambda i, j: (0, 0)),
                  tab_spec, tab_spec, tab_spec],
        out_specs=pl.BlockSpec((tm, tn), lambda i, j: (i, j)),
        out_shape=jax.ShapeDtypeStruct((t, n), BF16),
        scratch_shapes=[pltpu.VMEM((tm, d), BF16)],
        compiler_params=_cparams("parallel", "arbitrary"),
        name="qkv_project",
    )(x, g, shift, scale, w, gain_row, gmat, *rope_tabs)


def rope_tables(seq, head_dim):
    q4 = head_dim // 4
    pos = np.arange(seq)
    inv = ROPE_THETA ** (-np.arange(q4, dtype=np.float64) / q4)
    lane = np.arange(LANES) % head_dim
    axis, half, idx = lane // (2 * q4), (lane // q4) % 2, lane % q4
    p = np.where(axis[None, :] == 0, (pos // GRID_W)[:, None], (pos % GRID_W)[:, None]).astype(np.float64)
    ang = (p.astype(np.float32) * inv.astype(np.float32)[idx][None, :]).astype(np.float32)
    cos, sin = np.cos(ang), np.sin(ang)
    sa = np.where(half[None, :] == 0, -sin, 0.0)
    sb = np.where(half[None, :] == 1, sin, 0.0)
    pad = np.zeros((ROW_TILE, LANES), np.float32)
    return tuple(jnp.asarray(np.concatenate([a.astype(np.float32), b], axis=0))
                 for a, b in ((cos, pad + 1.0), (sa, pad), (sb, pad)))


def head_group_matrix(width, head_dim):
    r = np.arange(width) // head_dim
    return jnp.asarray((r[:, None] == r[None, :]).astype(np.float32), dtype=BF16)


def _oproj_kernel(a_ref, w_ref, x_ref, gate_ref, o_ref):
    acc = jnp.dot(a_ref[...], w_ref[...], preferred_element_type=F32)
    o_ref[...] = x_ref[...] + gate_ref[0] * acc


def out_project(a, w, x, gate, *, n_rows, seq, n_batch):
    k = a.shape[1]
    d = w.shape[1]
    tm, tn = ROW_TILE, 512
    tiles_per_seq = seq // tm
    return pl.pallas_call(
        _oproj_kernel,
        grid=(n_rows // tm, d // tn),
        in_specs=[pl.BlockSpec((tm, k), lambda i, j: (i, 0)),
                  pl.BlockSpec((k, tn), lambda i, j: (0, j)),
                  pl.BlockSpec((tm, tn), lambda i, j: (i, j)),
                  pl.BlockSpec((1, 1, tn), lambda i, j: (_group_of_tile(i, tiles_per_seq, n_batch), 0, j))],
        out_specs=pl.BlockSpec((tm, tn), lambda i, j: (i, j)),
        out_shape=jax.ShapeDtypeStruct((n_rows, d), F32),
        compiler_params=_cparams("parallel", "parallel"),
        name="out_project",
    )(a, w, x, gate)


def _lane_halves(a, own_half):
    lane = lax.broadcasted_iota(I32, a.shape, 1)
    af = a.astype(F32)
    sw = pltpu.roll(af, LANES // 2, axis=1)
    lo_src, hi_src = (af, sw) if own_half == 0 else (sw, af)
    lo = jnp.where(lane < LANES // 2, lo_src, 0.0).astype(BF16)
    hi = jnp.where(lane >= LANES // 2, hi_src, 0.0).astype(BF16)
    return lo, hi


def _swa_kernel(sink_ref, q_ref, kp_ref, kc_ref, kn_ref, kx_ref, vp_ref, vc_ref, vn_ref, vx_ref, o_ref,
                *, n_blocks, blk, n_ctx):
    i = pl.program_id(1)
    is_lat = i < n_blocks
    kcat = jnp.concatenate([kp_ref[...], kc_ref[...], kn_ref[...], kx_ref[...]], axis=0)
    vcat = jnp.concatenate([vp_ref[...], vc_ref[...], vn_ref[...], vx_ref[...]], axis=0)
    n_keys = 3 * blk + n_ctx
    qi = lax.broadcasted_iota(I32, (blk, n_keys), 0)
    kj = lax.broadcasted_iota(I32, (blk, n_keys), 1)
    in_band = jnp.abs(kj - blk - qi) <= WINDOW
    first_ok = jnp.where(i > 0, 0, blk)
    last_ok = jnp.where(is_lat, jnp.where(i < n_blocks - 1, 3 * blk, 2 * blk), 0)
    valid = (kj >= 3 * blk) | (in_band & (kj >= first_ok) & (kj < last_ok))
    maskadd = jnp.where(valid, 0.0, NEG_BIG).astype(F32)
    rep = SWA_HEADS // SWA_KV_HEADS
    dn = (((1,), (1,)), ((), ()))
    for kv in range(SWA_KV_HEADS):
        ch, half = kv // 2, kv % 2
        klo, khi = _lane_halves(kcat[:, ch * LANES:(ch + 1) * LANES], half)
        vlo, vhi = _lane_halves(vcat[:, ch * LANES:(ch + 1) * LANES], half)
        for c in range(rep // 2):
            col = (kv * (rep // 2) + c) * LANES
            qc = q_ref[:, col:col + LANES]
            out = None
            for kk, vv, head in ((klo, vlo, kv * rep + 2 * c), (khi, vhi, kv * rep + 2 * c + 1)):
                s = lax.dot_general(qc, kk, dn, preferred_element_type=F32) + maskadd
                sink = sink_ref[head]
                m = jnp.maximum(jnp.max(s, axis=1, keepdims=True), sink)
                p = jnp.exp(s - m)
                denom = jnp.sum(p, axis=1, keepdims=True) + jnp.exp(sink - m)
                o = jnp.dot(p.astype(BF16), vv, preferred_element_type=F32) * (1.0 / denom)
                out = o if out is None else out + o
            o_ref[:, col:col + LANES] = out.astype(o_ref.dtype)


def swa_attention(qkv, sink, *, n_batch, seq, n_ctx, with_ctx_queries):
    blk = WINDOW
    nb = seq // blk
    nbc = n_ctx // blk if with_ctx_queries else 0
    n_q = SWA_HEADS * SWA_HEAD_DIM
    kcol, vcol = n_q // 256, n_q // 256 + 1
    lat_blocks = n_batch * nb
    ctx_row0 = n_batch * seq

    def qrow(b, i):
        return jnp.where(i < nb, b * nb + i, lat_blocks + b * (n_ctx // blk) + (i - nb))

    def win(delta, col):
        return pl.BlockSpec((blk, 256), lambda b, i, s: (b * nb + jnp.clip(i + delta, 0, nb - 1), col))

    def ctx(col):
        return pl.BlockSpec((n_ctx, 256), lambda b, i, s: (ctx_row0 // n_ctx + b, col))

    n_rows = n_batch * (seq + (n_ctx if with_ctx_queries else 0))
    grid_spec = pltpu.PrefetchScalarGridSpec(
        num_scalar_prefetch=1,
        grid=(n_batch, nb + nbc),
        in_specs=[pl.BlockSpec((blk, n_q), lambda b, i, s: (qrow(b, i), 0)),
                  win(-1, kcol), win(0, kcol), win(1, kcol), ctx(kcol),
                  win(-1, vcol), win(0, vcol), win(1, vcol), ctx(vcol)],
        out_specs=pl.BlockSpec((blk, n_q), lambda b, i, s: (qrow(b, i), 0)),
    )
    return pl.pallas_call(
        functools.partial(_swa_kernel, n_blocks=nb, blk=blk, n_ctx=n_ctx),
        grid_spec=grid_spec,
        out_shape=jax.ShapeDtypeStruct((n_rows, n_q), BF16),
        compiler_params=_cparams("parallel", "parallel"),
        name="swa_attention",
    )(sink, qkv, qkv, qkv, qkv, qkv, qkv, qkv, qkv, qkv)


def _ga_kernel(q_ref, kl_ref, vl_ref, kx_ref, vx_ref, o_ref, m_ref, l_ref, acc_ref, *, n_lat_blocks, tq, tk):
    i = pl.program_id(2)
    rep = GA_HEADS // GA_KV_HEADS
    hd = GA_HEAD_DIM
    qs = jnp.concatenate([q_ref[:, r * hd:(r + 1) * hd] for r in range(rep)], axis=0)
    dn = (((1,), (1,)), ((), ()))

    s = lax.dot_general(qs, kx_ref[...], dn, preferred_element_type=F32)
    m0 = jnp.max(s, axis=1, keepdims=True)
    p = jnp.exp(s - m0)
    m_ref[...] = m0
    l_ref[...] = jnp.sum(p, axis=1, keepdims=True)
    acc_ref[...] = jnp.dot(p.astype(BF16), vx_ref[...], preferred_element_type=F32)

    def body(c, carry):
        rows = pl.ds(pl.multiple_of(c * tk, tk), tk)
        s = lax.dot_general(qs, kl_ref[rows, :], dn, preferred_element_type=F32)
        m_old = m_ref[...]
        m_new = jnp.maximum(m_old, jnp.max(s, axis=1, keepdims=True))
        alpha = jnp.exp(m_old - m_new)
        p = jnp.exp(s - m_new)
        l_ref[...] = alpha * l_ref[...] + jnp.sum(p, axis=1, keepdims=True)
        acc_ref[...] = alpha * acc_ref[...] + jnp.dot(p.astype(BF16), vl_ref[rows, :], preferred_element_type=F32)
        m_ref[...] = m_new
        return carry

    n_chunks = jnp.where(i < n_lat_blocks, kl_ref.shape[0] // tk, 0)
    lax.fori_loop(0, n_chunks, body, 0)
    out = acc_ref[...] * (1.0 / l_ref[...])
    for r in range(rep):
        o_ref[:, r * hd:(r + 1) * hd] = out[r * tq:(r + 1) * tq].astype(o_ref.dtype)


def global_attention(qkv, *, n_batch, seq, n_ctx, with_ctx_queries):
    tq, tk = 256, 512
    hd = GA_HEAD_DIM
    rep = GA_HEADS // GA_KV_HEADS
    n_q = GA_HEADS * hd
    nq = seq // tq
    nqc = n_ctx // tq if with_ctx_queries else 0
    kcol0, vcol0 = n_q // hd, n_q // hd + GA_KV_HEADS
    ctx_blk0 = n_batch * seq // n_ctx

    def qrow(b, i):
        return jnp.where(i < nq, b * nq + i, n_batch * nq + b * (n_ctx // tq) + (i - nq))

    n_rows = n_batch * (seq + (n_ctx if with_ctx_queries else 0))
    return pl.pallas_call(
        functools.partial(_ga_kernel, n_lat_blocks=nq, tq=tq, tk=tk),
        grid=(n_batch, GA_KV_HEADS, nq + nqc),
        in_specs=[pl.BlockSpec((tq, rep * hd), lambda b, k, i: (qrow(b, i), k)),
                  pl.BlockSpec((seq, hd), lambda b, k, i: (b, kcol0 + k)),
                  pl.BlockSpec((seq, hd), lambda b, k, i: (b, vcol0 + k)),
                  pl.BlockSpec((n_ctx, hd), lambda b, k, i: (ctx_blk0 + b, kcol0 + k)),
                  pl.BlockSpec((n_ctx, hd), lambda b, k, i: (ctx_blk0 + b, vcol0 + k))],
        out_specs=pl.BlockSpec((tq, rep * hd), lambda b, k, i: (qrow(b, i), k)),
        out_shape=jax.ShapeDtypeStruct((n_rows, n_q), BF16),
        scratch_shapes=[pltpu.VMEM((rep * tq, 1), F32), pltpu.VMEM((rep * tq, 1), F32),
                        pltpu.VMEM((rep * tq, hd), F32)],
        compiler_params=_cparams("parallel", "parallel", "arbitrary"),
        name="global_attention",
    )(qkv, qkv, qkv, qkv, qkv)


def _ssd_inproj_kernel(x_ref, g_ref, sh_ref, sc_ref, w_ref, o_ref, dt_ref, h_ref, *, n_main_tiles):
    j = pl.program_id(1)

    @pl.when(j == 0)
    def _():
        h_ref[...] = _norm_modulate(x_ref, g_ref, sh_ref, sc_ref).astype(BF16)

    acc = jnp.dot(h_ref[...], w_ref[...], preferred_element_type=F32)

    @pl.when(j < n_main_tiles)
    def _():
        o_ref[...] = acc.astype(o_ref.dtype)

    @pl.when(j == n_main_tiles)
    def _():
        dt_ref[...] = acc[:, :LANES]


def ssd_in_project(x, g, shift, scale, w_pad, *, n_main, seq, n_batch):
    t, d = x.shape
    tm, tn = ROW_TILE, 256
    tiles_per_seq = seq // tm
    n_main_tiles = n_main // tn

    def mod_map(i, j):
        return (_group_of_tile(i, tiles_per_seq, n_batch), 0, 0)

    return pl.pallas_call(
        functools.partial(_ssd_inproj_kernel, n_main_tiles=n_main_tiles),
        grid=(t // tm, n_main_tiles + 1),
        in_specs=[pl.BlockSpec((tm, d), lambda i, j: (i, 0)),
                  pl.BlockSpec((1, d), lambda i, j: (0, 0)),
                  pl.BlockSpec((1, 1, d), mod_map),
                  pl.BlockSpec((1, 1, d), mod_map),
                  pl.BlockSpec((d, tn), lambda i, j: (0, j))],
        out_specs=[pl.BlockSpec((tm, tn), lambda i, j: (i, jnp.minimum(j, n_main_tiles - 1))),
                   pl.BlockSpec((tm, LANES), lambda i, j: (i, 0))],
        out_shape=[jax.ShapeDtypeStruct((t, n_main), BF16), jax.ShapeDtypeStruct((t, LANES), F32)],
        scratch_shapes=[pltpu.VMEM((tm, d), BF16)],
        compiler_params=_cparams("parallel", "arbitrary"),
        name="ssd_in_project",
    )(x, g, shift, scale, w_pad)


CONV_HALO = 8
CONV_ROWS = 256


def _conv_kernel(u_ref, w_ref, b_ref, o_ref, pad_ref, *, seq_len):
    tc = u_ref.shape[1]
    zeros = jnp.zeros((CONV_HALO, tc), F32)
    pad_ref[0:CONV_HALO, :] = zeros
    pad_ref[CONV_HALO + seq_len:2 * CONV_HALO + seq_len, :] = zeros
    for c in range(seq_len // CONV_ROWS):
        r0 = c * CONV_ROWS
        pad_ref[CONV_HALO + r0:CONV_HALO + r0 + CONV_ROWS, :] = u_ref[r0:r0 + CONV_ROWS, :].astype(F32)
    w = w_ref[...]
    b = b_ref[...]
    for c in range(seq_len // CONV_ROWS):
        r0 = c * CONV_ROWS
        acc = b
        for k in range(SSD_CONV):
            off = CONV_HALO + r0 + k - SSD_CONV // 2
            acc = acc + w[k:k + 1, :] * pad_ref[off:off + CONV_ROWS, :]
        o_ref[r0:r0 + CONV_ROWS, :] = (acc * jax.nn.sigmoid(acc)).astype(o_ref.dtype)


def ssd_conv(main, conv_w, conv_b, *, seq_len, n_seqs, row_block0, col0):
    n_ch = conv_w.shape[1]
    tc = 256
    return pl.pallas_call(
        functools.partial(_conv_kernel, seq_len=seq_len),
        grid=(n_seqs, n_ch // tc),
        in_specs=[pl.BlockSpec((seq_len, tc), lambda b, j: (row_block0 + b, col0 // tc + j)),
                  pl.BlockSpec((SSD_CONV, tc), lambda b, j: (0, j)),
                  pl.BlockSpec((1, tc), lambda b, j: (0, j))],
        out_specs=pl.BlockSpec((seq_len, tc), lambda b, j: (b, j)),
        out_shape=jax.ShapeDtypeStruct((n_seqs * seq_len, n_ch), BF16),
        scratch_shapes=[pltpu.VMEM((seq_len + 2 * CONV_HALO, tc), F32)],
        compiler_params=_cparams("parallel", "parallel"),
        name="ssd_conv",
    )(main, conv_w, conv_b.reshape(1, n_ch))


DIR_LANES = 16


def _prep_kernel(raw_ref, bias_ref, alog_ref, acs_o, eacs_o, dtT_o, acsT_o, wT_o, etot_o):
    v = raw_ref[0] + bias_ref[0]
    dt = jnp.maximum(v, 0.0) + jnp.log1p(jnp.exp(-jnp.abs(v)))
    a = dt * (-jnp.exp(alog_ref[0]))
    ch = SSD_CHUNK
    row = lax.broadcasted_iota(I32, (ch, LANES), 0)
    lane = lax.broadcasted_iota(I32, (ch, LANES), 1)
    fwd, rev = a, a
    s = 1
    while s < ch:
        fwd = fwd + jnp.where(row >= s, pltpu.roll(fwd, s, axis=0), 0.0)
        rev = rev + jnp.where(row < ch - s, pltpu.roll(rev, ch - s, axis=0), 0.0)
        s *= 2
    is_fwd = (lane % DIR_LANES) < DIR_LANES // 2
    acs = jnp.where(is_fwd, fwd, rev)
    tot = jnp.where(is_fwd[0:1], fwd[ch - 1:ch, :], rev[0:1, :])
    acs_o[0] = acs
    eacs_o[0] = jnp.exp(acs)
    dtT_o[0, 0] = dt.T[:DIR_LANES]
    acsT_o[0, 0] = acs.T[:DIR_LANES]
    wT_o[0, 0] = (jnp.exp(tot - acs) * dt).T[:DIR_LANES]
    etot_o[0, 0] = jnp.broadcast_to(jnp.exp(tot), (8, LANES))


def ssd_prep(dt_g, bias_g, alog_g):
    n_g, t, _ = dt_g.shape
    ch = SSD_CHUNK
    nck = t // ch
    vec = pl.BlockSpec((1, 1, LANES), lambda g, c: (g, 0, 0))
    rows = pl.BlockSpec((1, ch, LANES), lambda g, c: (g, c, 0))
    tr = pl.BlockSpec((1, 1, DIR_LANES, LANES), lambda g, c: (g, c, 0, 0))
    return pl.pallas_call(
        _prep_kernel,
        grid=(n_g, nck),
        in_specs=[rows, vec, vec],
        out_specs=[rows, rows, tr, tr, tr, pl.BlockSpec((1, 1, 8, LANES), lambda g, c: (g, c, 0, 0))],
        out_shape=[jax.ShapeDtypeStruct((n_g, t, LANES), F32), jax.ShapeDtypeStruct((n_g, t, LANES), F32),
                   jax.ShapeDtypeStruct((n_g, nck, DIR_LANES, LANES), F32),
                   jax.ShapeDtypeStruct((n_g, nck, DIR_LANES, LANES), F32),
                   jax.ShapeDtypeStruct((n_g, nck, DIR_LANES, LANES), F32),
                   jax.ShapeDtypeStruct((n_g, nck, 8, LANES), F32)],
        compiler_params=_cparams("parallel", "parallel"),
        name="ssd_prep",
    )(dt_g, bias_g, alog_g)


HEADS_PER_GROUP = 8
PAIRS_PER_GROUP = HEADS_PER_GROUP // 2
GROUP_COLS = HEADS_PER_GROUP * SSD_HEAD_DIM


def _scan_kernel(*refs, n_chunks, emit_state, alias_out):
    (x_ref, b_ref, c_ref, z_ref, acs_ref, eacs_ref, dtT_ref, acsT_ref, wT_ref, etot_ref,
     dskip_ref, ng_ref, init_ref) = refs[:13]
    rest = refs[13 + (1 if alias_out else 0):]
    y_ref = rest[0]
    fin_ref = rest[1] if emit_state else None
    ysc, st = rest[-2], rest[-1]
    ch = SSD_CHUNK
    st[...] = init_ref[0, 0]
    lo = lax.broadcasted_iota(I32, (1, LANES), 1) < LANES // 2
    li = lax.broadcasted_iota(I32, (ch, ch), 0)
    si = lax.broadcasted_iota(I32, (ch, ch), 1)
    dn = (((1,), (1,)), ((), ()))
    for d in range(2):
        mask = (si <= li) if d == 0 else (si >= li)

        def chunk(cc, carry, d=d, mask=mask):
            c = cc if d == 0 else n_chunks - 1 - cc
            rows = pl.ds(pl.multiple_of(c * ch, ch), ch)
            bc, cm = b_ref[rows, :], c_ref[rows, :]
            cb = lax.dot_general(cm, bc, dn, preferred_element_type=F32)
            bt = bc.astype(F32).T
            cf = cm.astype(F32)
            acs_blk = acs_ref[0, rows, :]
            eacs_blk = eacs_ref[0, rows, :]
            et = etot_ref[0, c]
            for pr in range(PAIRS_PER_GROUP):
                cols = slice(pr * LANES, (pr + 1) * LANES)
                x2 = x_ref[rows, cols]
                zero = jnp.zeros_like(x2)
                s2 = st[d * PAIRS_PER_GROUP + pr]
                r_lo = d * HEADS_PER_GROUP + 2 * pr
                y2 = jnp.zeros((ch, LANES), F32)
                snew = s2 * jnp.where(lo, et[0:1, r_lo:r_lo + 1], et[0:1, r_lo + 1:r_lo + 2])
                for hh in range(2):
                    r = r_lo + hh
                    keep = lo if hh == 0 else jnp.logical_not(lo)
                    xm = jnp.where(keep, x2, zero)
                    sm = jnp.where(keep, s2, 0.0).astype(BF16)
                    lm = jnp.exp(jnp.where(mask, acs_blk[:, r:r + 1] - acsT_ref[0, c, r:r + 1, :], NEG_BIG))
                    m_h = (cb * lm * dtT_ref[0, c, r:r + 1, :]).astype(BF16)
                    c_h = (cf * eacs_blk[:, r:r + 1]).astype(BF16)
                    y2 = y2 + jnp.dot(m_h, xm, preferred_element_type=F32)
                    y2 = y2 + jnp.dot(c_h, sm, preferred_element_type=F32)
                    b_h = (bt * wT_ref[0, c, r:r + 1, :]).astype(BF16)
                    snew = snew + jnp.dot(b_h, xm, preferred_element_type=F32)
                st[d * PAIRS_PER_GROUP + pr] = snew
                if d == 0:
                    ysc[rows, cols] = y2
                else:
                    ysc[rows, cols] = ysc[rows, cols] + y2
            if d == 1:
                y = ysc[rows, :] + dskip_ref[0] * x_ref[rows, :].astype(F32)
                zf = z_ref[rows, :].astype(F32)
                y = y * (zf * jax.nn.sigmoid(zf))
                y = y * lax.rsqrt(jnp.mean(y * y, axis=-1, keepdims=True) + NORM_EPS) * ng_ref[0]
                y_ref[rows, :] = y.astype(y_ref.dtype)
            return carry

        lax.fori_loop(0, n_chunks, chunk, 0)
    if emit_state:
        fin_ref[0, 0] = st[...]


def ssd_scan(xbc, main, prep, dskip_g, normg_g, init_state, y_prev, *, seq_len, n_batch, row_block0,
             n_rows_total, emit_state):
    acs, eacs, dt_t, acs_t, w_t, etot = prep
    n_g = SSD_GROUPS
    nck = seq_len // SSD_CHUNK
    inner = n_g * GROUP_COLS
    bcol0 = inner // LANES
    ccol0 = bcol0 + n_g * SSD_STATE // LANES
    rb = row_block0

    def tr_spec(h):
        return pl.BlockSpec((1, nck, h, LANES), lambda b, g: (g, rb + b, 0, 0))

    vec = pl.BlockSpec((1, 1, GROUP_COLS), lambda b, g: (g, 0, 0))
    st_spec = pl.BlockSpec((1, 1, 2 * PAIRS_PER_GROUP, SSD_STATE, LANES), lambda b, g: (b, g, 0, 0, 0))
    in_specs = [pl.BlockSpec((seq_len, GROUP_COLS), lambda b, g: (b, g)),
                pl.BlockSpec((seq_len, LANES), lambda b, g: (b, bcol0 + g)),
                pl.BlockSpec((seq_len, LANES), lambda b, g: (b, ccol0 + g)),
                pl.BlockSpec((seq_len, GROUP_COLS), lambda b, g: (rb + b, g)),
                pl.BlockSpec((1, seq_len, LANES), lambda b, g: (g, rb + b, 0)),
                pl.BlockSpec((1, seq_len, LANES), lambda b, g: (g, rb + b, 0)),
                tr_spec(DIR_LANES), tr_spec(DIR_LANES), tr_spec(DIR_LANES), tr_spec(8),
                vec, vec, st_spec]
    args = [xbc, xbc, xbc, main, acs, eacs, dt_t, acs_t, w_t, etot, dskip_g, normg_g, init_state]
    aliases = {}
    if y_prev is not None:
        in_specs.append(pl.BlockSpec(memory_space=pl.ANY))
        args.append(y_prev)
        aliases = {len(args) - 1: 0}
    out_specs = [pl.BlockSpec((seq_len, GROUP_COLS), lambda b, g: (rb + b, g))]
    out_shape = [jax.ShapeDtypeStruct((n_rows_total, inner), BF16)]
    if emit_state:
        out_specs.append(st_spec)
        out_shape.append(jax.ShapeDtypeStruct(init_state.shape, F32))
    return pl.pallas_call(
        functools.partial(_scan_kernel, n_chunks=nck, emit_state=emit_state, alias_out=y_prev is not None),
        grid=(n_batch, n_g),
        in_specs=in_specs,
        out_specs=out_specs,
        out_shape=out_shape,
        scratch_shapes=[pltpu.VMEM((seq_len, GROUP_COLS), F32),
                        pltpu.VMEM((2 * PAIRS_PER_GROUP, SSD_STATE, LANES), F32)],
        input_output_aliases=aliases,
        compiler_params=_cparams("parallel", "parallel"),
        name="ssd_scan",
    )(*args)


def _router_kernel(x_ref, g_ref, sh_ref, sc_ref, whi_ref, wlo_ref, b_ref, tri_ref,
                   h_o, meta_o, gate_o, cnt_o, carry):
    i = pl.program_id(0)

    @pl.when(i == 0)
    def _():
        carry[...] = jnp.zeros_like(carry)

    h = _norm_modulate(x_ref, g_ref, sh_ref, sc_ref)
    h_o[...] = h
    h_hi = h.astype(BF16)
    h_lo = (h - h_hi.astype(F32)).astype(BF16)
    whi = whi_ref[...]
    logits = (jnp.dot(h_hi, whi, preferred_element_type=F32) + jnp.dot(h_lo, whi, preferred_element_type=F32)
              + jnp.dot(h_hi, wlo_ref[...], preferred_element_type=F32) + b_ref[...])
    tm = logits.shape[0]
    lane = lax.broadcasted_iota(I32, (tm, LANES), 1).astype(F32)
    work = logits
    vals, idxs = [], []
    for _ in range(TOP_K):
        m = jnp.max(work, axis=1, keepdims=True)
        idx = jnp.min(jnp.where(work == m, lane, float(LANES)), axis=1, keepdims=True)
        vals.append(m)
        idxs.append(idx)
        work = jnp.where(lane == idx, -jnp.inf, work)
    es = [jnp.exp(v - vals[0]) for v in vals]
    inv = 1.0 / (es[0] + es[1] + es[2] + es[3])
    multihot = sum(jnp.where(lane == idx, 1.0, 0.0) for idx in idxs)
    before = jnp.dot(tri_ref[...], multihot.astype(BF16), preferred_element_type=F32) + carry[...]
    meta = jnp.zeros((tm, LANES), F32)
    gates = jnp.zeros((tm, LANES), F32)
    for k in range(TOP_K):
        rank = jnp.sum(jnp.where(lane == idxs[k], before, 0.0), axis=1, keepdims=True)
        meta = meta + jnp.where(lane == float(k), idxs[k], 0.0) + jnp.where(lane == float(TOP_K + k), rank, 0.0)
        gates = gates + jnp.where(lane == float(k), es[k] * inv, 0.0)
    meta_o[...] = meta.astype(I32)
    gate_o[...] = gates
    carry[...] = carry[...] + jnp.sum(multihot, axis=0, keepdims=True)
    cnt_o[...] = carry[...]


def moe_router(x, g, shift, scale, w_hi, w_lo, b_pad, *, n_rows, seq, n_batch):
    d = x.shape[1]
    tm = ROW_TILE
    tiles_per_seq = seq // tm
    tri = jnp.asarray(np.tril(np.ones((tm, tm), np.float32), -1), dtype=BF16)

    def mod_map(i):
        return (_group_of_tile(i, tiles_per_seq, n_batch), 0, 0)

    full = lambda shape: pl.BlockSpec(shape, lambda i: (0,) * len(shape))
    return pl.pallas_call(
        _router_kernel,
        grid=(n_rows // tm,),
        in_specs=[pl.BlockSpec((tm, d), lambda i: (i, 0)), full((1, d)),
                  pl.BlockSpec((1, 1, d), mod_map), pl.BlockSpec((1, 1, d), mod_map),
                  full((d, LANES)), full((d, LANES)), full((1, LANES)), full((tm, tm))],
        out_specs=[pl.BlockSpec((tm, d), lambda i: (i, 0)), pl.BlockSpec((tm, LANES), lambda i: (i, 0)),
                   pl.BlockSpec((tm, LANES), lambda i: (i, 0)), full((1, LANES))],
        out_shape=[jax.ShapeDtypeStruct((n_rows, d), F32), jax.ShapeDtypeStruct((n_rows, LANES), I32),
                   jax.ShapeDtypeStruct((n_rows, LANES), F32), jax.ShapeDtypeStruct((1, LANES), F32)],
        scratch_shapes=[pltpu.VMEM((1, LANES), F32)],
        compiler_params=_cparams("arbitrary"),
        name="moe_router",
    )(x, g, shift, scale, w_hi, w_lo, b_pad, tri)


def _row_gather_start(idx_ref, n, src_hbm, dst, sem):
    def body(r, c):
        pltpu.make_async_copy(src_hbm.at[pl.ds(idx_ref[0, 0, r], 1), :], dst.at[pl.ds(r, 1), :], sem).start()
        return c

    lax.fori_loop(0, n, body, 0)


def _row_gather_wait(n, src_hbm, dst, sem):
    def body(r, c):
        pltpu.make_async_copy(src_hbm.at[pl.ds(0, 1), :], dst.at[pl.ds(r, 1), :], sem).wait()
        return c

    lax.fori_loop(0, n, body, 0)


def _ffn_kernel(bexp_ref, nact_ref, src_ref, srcn_ref, h_hbm, wgu_ref, bgu_ref, wd_ref, bd_ref, o_ref, xbuf, sem):
    s = pl.program_id(0)
    slot = s % 2
    nact = nact_ref[0]

    @pl.when(jnp.logical_and(s == 0, nact > 0))
    def _():
        _row_gather_start(src_ref, MOE_BLOCK, h_hbm, xbuf.at[0], sem.at[0])

    @pl.when(s + 1 < nact)
    def _():
        _row_gather_start(srcn_ref, MOE_BLOCK, h_hbm, xbuf.at[1 - slot], sem.at[1 - slot])

    @pl.when(s < nact)
    def _():
        _row_gather_wait(MOE_BLOCK, h_hbm, xbuf.at[slot], sem.at[slot])
        x = xbuf[slot].astype(BF16)
        gu = jnp.dot(x, wgu_ref[0], preferred_element_type=F32) + bgu_ref[0]
        glu = jnp.minimum(gu[:, :EXPERT_FF], SWIGLU_LIMIT)
        lin = jnp.clip(gu[:, EXPERT_FF:], -SWIGLU_LIMIT, SWIGLU_LIMIT)
        act = glu * jax.nn.sigmoid(SWIGLU_ALPHA * glu) * (lin + 1.0)
        o_ref[...] = jnp.dot(act.astype(BF16), wd_ref[0], preferred_element_type=F32) + bd_ref[0]

    @pl.when(s >= nact)
    def _():
        o_ref[...] = jnp.zeros_like(o_ref)


def moe_ffn_blocks(block_exp, n_active, row_src, h, w_gu, b_gu, w_down, b_down):
    n_blocks = row_src.shape[0]
    d = h.shape[1]
    ff2 = w_gu.shape[2]
    grid_spec = pltpu.PrefetchScalarGridSpec(
        num_scalar_prefetch=2,
        grid=(n_blocks,),
        in_specs=[pl.BlockSpec((1, 1, MOE_BLOCK), lambda s, be, na: (s, 0, 0), memory_space=pltpu.SMEM),
                  pl.BlockSpec((1, 1, MOE_BLOCK), lambda s, be, na: (jnp.minimum(s + 1, n_blocks - 1), 0, 0),
                               memory_space=pltpu.SMEM),
                  pl.BlockSpec(memory_space=pl.ANY),
                  pl.BlockSpec((1, d, ff2), lambda s, be, na: (be[s], 0, 0)),
                  pl.BlockSpec((1, 1, ff2), lambda s, be, na: (be[s], 0, 0)),
                  pl.BlockSpec((1, ff2 // 2, d), lambda s, be, na: (be[s], 0, 0)),
                  pl.BlockSpec((1, 1, d), lambda s, be, na: (be[s], 0, 0))],
        out_specs=pl.BlockSpec((MOE_BLOCK, d), lambda s, be, na: (s, 0)),
        scratch_shapes=[pltpu.VMEM((2, MOE_BLOCK, d), F32), pltpu.SemaphoreType.DMA((2,))],
    )
    return pl.pallas_call(
        _ffn_kernel,
        grid_spec=grid_spec,
        out_shape=jax.ShapeDtypeStruct((n_blocks * MOE_BLOCK, d), F32),
        compiler_params=_cparams("arbitrary"),
        name="moe_ffn",
    )(block_exp, n_active, row_src, row_src, h, w_gu, b_gu, w_down, b_down)


COMBINE_TILE = 128


def _combine_kernel(dst_ref, dstn_ref, rows_hbm, x_ref, gate_ref, mod_ref, o_ref, buf, sem):
    s = pl.program_id(0)
    n = pl.num_programs(0)
    slot = s % 2
    tm = COMBINE_TILE

    @pl.when(s == 0)
    def _():
        _row_gather_start(dst_ref, TOP_K * tm, rows_hbm, buf.at[0], sem.at[0])

    @pl.when(s + 1 < n)
    def _():
        _row_gather_start(dstn_ref, TOP_K * tm, rows_hbm, buf.at[1 - slot], sem.at[1 - slot])

    _row_gather_wait(TOP_K * tm, rows_hbm, buf.at[slot], sem.at[slot])
    gates = gate_ref[...]
    f = gates[:, 0:1] * buf[slot, 0:tm, :]
    for k in range(1, TOP_K):
        f = f + gates[:, k:k + 1] * buf[slot, k * tm:(k + 1) * tm, :]
    o_ref[...] = x_ref[...] + mod_ref[0] * f


def moe_combine(dest_tiles, out_rows, x, gates, gate_mod, *, n_rows, seq, n_batch):
    d = x.shape[1]
    tm = COMBINE_TILE
    n_tiles = n_rows // tm
    tiles_per_seq = seq // tm
    return pl.pallas_call(
        _combine_kernel,
        grid=(n_tiles,),
        in_specs=[pl.BlockSpec((1, 1, TOP_K * tm), lambda s: (s, 0, 0), memory_space=pltpu.SMEM),
                  pl.BlockSpec((1, 1, TOP_K * tm), lambda s: (jnp.minimum(s + 1, n_tiles - 1), 0, 0),
                               memory_space=pltpu.SMEM),
                  pl.BlockSpec(memory_space=pl.ANY),
                  pl.BlockSpec((tm, d), lambda s: (s, 0)),
                  pl.BlockSpec((tm, LANES), lambda s: (s, 0)),
                  pl.BlockSpec((1, 1, d), lambda s: (_group_of_tile(s, tiles_per_seq, n_batch), 0, 0))],
        out_specs=pl.BlockSpec((tm, d), lambda s: (s, 0)),
        out_shape=jax.ShapeDtypeStruct((n_rows, d), F32),
        scratch_shapes=[pltpu.VMEM((2, TOP_K * tm, d), F32), pltpu.SemaphoreType.DMA((2,))],
        compiler_params=_cparams("arbitrary"),
        name="moe_combine",
    )(dest_tiles, dest_tiles, out_rows, x, gates, gate_mod)


def moe_layer(x, g, shift, scale, gate_mod, w_router, b_router, w_gu, b_gu, w_down, b_down,
              *, n_rows, seq, n_batch):
    n_exp = w_router.shape[1]
    w_pad = jnp.pad(w_router, ((0, 0), (0, LANES - n_exp)))
    w_hi = w_pad.astype(BF16)
    w_lo = (w_pad - w_hi.astype(F32)).astype(BF16)
    b_pad = jnp.concatenate([b_router, jnp.full((LANES - n_exp,), NEG_BIG, F32)])[None, :]
    h, meta, gates, cnt = moe_router(x, g, shift, scale, w_hi, w_lo, b_pad, n_rows=n_rows, seq=seq, n_batch=n_batch)
    top_idx, rank = meta[:, :TOP_K], meta[:, TOP_K:2 * TOP_K]
    counts = cnt[0, :n_exp].astype(I32)
    padded = (counts + MOE_BLOCK - 1) // MOE_BLOCK * MOE_BLOCK
    pad_end = jnp.cumsum(padded)
    pad_start = pad_end - padded
    dest = pad_start[top_idx] + rank
    n_blocks = n_rows * TOP_K // MOE_BLOCK + n_exp
    block_exp = jnp.minimum(jnp.searchsorted(pad_end, jnp.arange(n_blocks, dtype=I32) * MOE_BLOCK, side="right"),
                            n_exp - 1).astype(I32)
    n_active = (pad_end[-1:] // MOE_BLOCK).astype(I32)
    tok = jnp.broadcast_to(jnp.arange(n_rows, dtype=I32)[:, None], dest.shape)
    row_src = jnp.zeros((n_blocks * MOE_BLOCK,), I32).at[dest.reshape(-1)].set(tok.reshape(-1))
    out_rows = moe_ffn_blocks(block_exp, n_active, row_src.reshape(n_blocks, 1, MOE_BLOCK), h,
                              w_gu.astype(BF16), b_gu[:, None, :], w_down.astype(BF16), b_down[:, None, :])
    tm = COMBINE_TILE
    dest_tiles = dest.reshape(n_rows // tm, tm, TOP_K).transpose(0, 2, 1).reshape(n_rows // tm, 1, TOP_K * tm)
    return moe_combine(dest_tiles, out_rows, x, gates, gate_mod, n_rows=n_rows, seq=seq, n_batch=n_batch)


def attention_layer(x, g, shift, scale, gate_mod, w_in, q_gain, k_gain, w_out, sink, *, n_heads, n_kv, head_dim,
                    seq, n_ctx, n_batch, need_ctx):
    n_qk = (n_heads + n_kv) * head_dim
    gain_row = jnp.concatenate([jnp.tile(q_gain, n_heads) * head_dim ** -0.5, jnp.tile(k_gain, n_kv),
                                jnp.ones((n_kv * head_dim,), F32)])[None, :]
    qkv = qkv_project(x, g, shift, scale, w_in.astype(BF16), gain_row, head_group_matrix(256, head_dim),
                      rope_tables(seq, head_dim), n_qk_cols=n_qk, head_dim=head_dim, seq=seq, n_batch=n_batch)
    if sink is not None:
        att = swa_attention(qkv, sink, n_batch=n_batch, seq=seq, n_ctx=n_ctx, with_ctx_queries=need_ctx)
    else:
        att = global_attention(qkv, n_batch=n_batch, seq=seq, n_ctx=n_ctx, with_ctx_queries=need_ctx)
    n_rows = n_batch * (seq + (n_ctx if need_ctx else 0))
    return out_project(att, w_out.astype(BF16), x, gate_mod, n_rows=n_rows, seq=seq, n_batch=n_batch)


def ssd_layer(x, g, shift, scale, gate_mod, w_in, conv_w, conv_b, dt_bias, a_log, d_skip, norm_g, w_out,
              *, seq, n_ctx, n_batch):
    t, d = x.shape
    n_g = SSD_GROUPS
    n_heads = d_skip.shape[0]
    inner = n_heads * SSD_HEAD_DIM
    n_main = inner + conv_w.shape[1]
    w_pad = jnp.pad(w_in, ((0, 0), (0, n_main + 256 - w_in.shape[1]))).astype(BF16)
    main, dt_raw = ssd_in_project(x, g, shift, scale, w_pad, n_main=n_main, seq=seq, n_batch=n_batch)

    def to_groups(a):
        lead = a.shape[:-1]
        a = a.reshape(lead + (2, n_g, HEADS_PER_GROUP))
        a = jnp.moveaxis(a, -2, 0).reshape((n_g,) + lead + (DIR_LANES,))
        return jnp.pad(a, [(0, 0)] * (a.ndim - 1) + [(0, LANES - DIR_LANES)])

    prep = ssd_prep(to_groups(dt_raw[:, :2 * n_heads]), to_groups(dt_bias.reshape(1, -1)),
                    to_groups(a_log.reshape(1, -1)))
    dskip_g = jnp.repeat(d_skip, SSD_HEAD_DIM).reshape(n_g, 1, GROUP_COLS)
    normg_g = norm_g.reshape(n_g, 1, GROUP_COLS)
    n_lat = n_batch * seq
    xbc_c = ssd_conv(main, conv_w, conv_b, seq_len=n_ctx, n_seqs=n_batch, row_block0=n_lat // n_ctx, col0=inner)
    xbc_l = ssd_conv(main, conv_w, conv_b, seq_len=seq, n_seqs=n_batch, row_block0=0, col0=inner)
    zero_state = jnp.zeros((n_batch, n_g, 2 * PAIRS_PER_GROUP, SSD_STATE, LANES), F32)
    y = jnp.zeros((t, inner), BF16)
    y, ctx_state = ssd_scan(xbc_c, main, prep, dskip_g, normg_g, zero_state, y, seq_len=n_ctx, n_batch=n_batch,
                            row_block0=n_lat // n_ctx, n_rows_total=t, emit_state=True)
    (y,) = ssd_scan(xbc_l, main, prep, dskip_g, normg_g, ctx_state, y, seq_len=seq, n_batch=n_batch,
                    row_block0=0, n_rows_total=t, emit_state=False)
    return out_project(y, w_out.astype(BF16), x, gate_mod, n_rows=t, seq=seq, n_batch=n_batch)


def kernel(x, c, ctx, c_ctx, ada_w, ada_b, norm_mix, norm_ffn, swa_w_in, swa_q_norm, swa_k_norm, swa_sink,
           swa_w_out, ssd_w_in, ssd_conv_w, ssd_conv_b, ssd_dt_bias, ssd_a_log, ssd_d, ssd_norm, ssd_w_out,
           ga_w_in, ga_q_norm, ga_k_norm, ga_w_out, moe_w_router, moe_b_router, moe_w_gate_up, moe_b_gate_up,
           moe_w_down, moe_b_down):
    n_batch, seq, d = x.shape
    n_ctx = ctx.shape[1]
    n_lat = n_batch * seq
    xs = jnp.concatenate([x.reshape(n_lat, d), ctx.reshape(n_batch * n_ctx, d)], axis=0)
    cvec = jnp.concatenate([c, c_ctx[None, :], jnp.zeros((8 - n_batch - 1, d), F32)], axis=0)
    mods = adaln_all(cvec, ada_w, ada_b)[:, :n_batch + 1].reshape(DEPTH, n_batch + 1, 6, 1, d)
    dims = dict(seq=seq, n_ctx=n_ctx, n_batch=n_batch)
    for i in range(DEPTH):
        kind, j = i % N_MIXERS, i // N_MIXERS
        need_ctx = i < DEPTH - 1
        m = [mods[i, :, k] for k in range(6)]
        g_mix = norm_mix[i][None, :]
        if kind == 0:
            xs = attention_layer(xs, g_mix, m[0], m[1], m[2], swa_w_in[j], swa_q_norm[j], swa_k_norm[j],
                                 swa_w_out[j], swa_sink[j], n_heads=SWA_HEADS, n_kv=SWA_KV_HEADS,
                                 head_dim=SWA_HEAD_DIM, need_ctx=need_ctx, **dims)
        elif kind == 1:
            xs = ssd_layer(xs, g_mix, m[0], m[1], m[2], ssd_w_in[j], ssd_conv_w[j], ssd_conv_b[j], ssd_dt_bias[j],
                           ssd_a_log[j], ssd_d[j], ssd_norm[j], ssd_w_out[j], **dims)
        else:
            xs = attention_layer(xs, g_mix, m[0], m[1], m[2], ga_w_in[j], ga_q_norm[j], ga_k_norm[j], ga_w_out[j],
                                 None, n_heads=GA_HEADS, n_kv=GA_KV_HEADS, head_dim=GA_HEAD_DIM,
                                 need_ctx=need_ctx, **dims)
        n_rows = n_batch * (seq + (n_ctx if need_ctx else 0))
        xs = moe_layer(xs, norm_ffn[i][None, :], m[3], m[4], m[5], moe_w_router[i], moe_b_router[i],
                       moe_w_gate_up[i], moe_b_gate_up[i], moe_w_down[i], moe_b_down[i],
                       n_rows=n_rows, seq=seq, n_batch=n_batch)
    return xs[:n_lat].reshape(n_batch, seq, d)
```

```python
import functools
import math

import jax
import jax.numpy as jnp
import numpy as np
from jax import lax
from jax.experimental import pallas as pl
from jax.experimental.pallas import tpu as pltpu

F32 = jnp.float32
BF16 = jnp.bfloat16
I32 = jnp.int32

DEPTH = 4
N_MIXERS = 3
GRID_W = 64
NORM_EPS = 1e-6
ROPE_THETA = 10000.0

SWA_HEADS, SWA_KV_HEADS, SWA_HEAD_DIM, WINDOW = 32, 4, 64, 128
SSD_HEAD_DIM, SSD_GROUPS, SSD_STATE, SSD_CONV, SSD_CHUNK = 64, 8, 128, 5, 128
GA_HEADS, GA_KV_HEADS, GA_HEAD_DIM = 16, 4, 128
N_EXPERTS, TOP_K, EXPERT_FF = 32, 4, 896
SWIGLU_ALPHA, SWIGLU_LIMIT = 1.702, 7.0
MOE_BLOCK = 256

LANES = 128
ROW_TILE = 512
NEG_BIG = -1e30
VMEM_LIMIT = 56 * 1024 * 1024


def _cparams(*sem):
    return pltpu.CompilerParams(dimension_semantics=sem, vmem_limit_bytes=VMEM_LIMIT)


def _adaln_kernel(c_ref, w_ref, b_ref, o_ref):
    a = c_ref[...]
    a = (a * jax.nn.sigmoid(a)).astype(BF16)
    o_ref[0] = jnp.dot(a, w_ref[0].astype(BF16), preferred_element_type=F32) + b_ref[0]


def adaln_all(cvec, ada_w, ada_b):
    n_layers, d, n = ada_w.shape
    tn = 1024
    return pl.pallas_call(
        _adaln_kernel,
        grid=(n_layers, n // tn),
        in_specs=[pl.BlockSpec((8, d), lambda l, j: (0, 0)),
                  pl.BlockSpec((1, d, tn), lambda l, j: (l, 0, j)),
                  pl.BlockSpec((1, 1, tn), lambda l, j: (l, 0, j))],
        out_specs=pl.BlockSpec((1, 8, tn), lambda l, j: (l, 0, j)),
        out_shape=jax.ShapeDtypeStruct((n_layers, 8, n), F32),
        compiler_params=_cparams("parallel", "parallel"),
        name="adaln",
    )(cvec, ada_w, ada_b.reshape(n_layers, 1, n))


def _norm_modulate(x_ref, g_ref, sh_ref, sc_ref):
    x = x_ref[...]
    ms = jnp.mean(x * x, axis=-1, keepdims=True)
    y = x * lax.rsqrt(ms + NORM_EPS) * g_ref[...]
    return y * (1.0 + sc_ref[0]) + sh_ref[0]


def _group_of_tile(i, tiles_per_seq, n_batch):
    return jnp.minimum(i // tiles_per_seq, n_batch)


def _qkv_kernel(x_ref, g_ref, sh_ref, sc_ref, w_ref, gain_ref, gmat_ref, cos_ref, sa_ref, sb_ref,
                o_ref, h_ref, *, n_qk_tiles, head_dim):
    j = pl.program_id(1)

    @pl.when(j == 0)
    def _():
        h_ref[...] = _norm_modulate(x_ref, g_ref, sh_ref, sc_ref).astype(BF16)

    acc = jnp.dot(h_ref[...], w_ref[...], preferred_element_type=F32)

    @pl.when(j < n_qk_tiles)
    def _():
        ss = jnp.dot((acc * acc).astype(BF16), gmat_ref[...], preferred_element_type=F32)
        yn = acc * lax.rsqrt(ss * (1.0 / head_dim) + NORM_EPS) * gain_ref[...]
        q4 = head_dim // 4
        cos, sa, sb = cos_ref[...], sa_ref[...], sb_ref[...]
        for c in range(acc.shape[1] // LANES):
            yc = yn[:, c * LANES:(c + 1) * LANES]
            up = pltpu.roll(yc, LANES - q4, axis=1)
            dn = pltpu.roll(yc, q4, axis=1)
            o_ref[:, c * LANES:(c + 1) * LANES] = (yc * cos + up * sa + dn * sb).astype(o_ref.dtype)

    @pl.when(j >= n_qk_tiles)
    def _():
        o_ref[...] = acc.astype(o_ref.dtype)


def qkv_project(x, g, shift, scale, w, gain_row, gmat, rope_tabs, *, n_qk_cols, head_dim, seq, n_batch):
    t, d = x.shape
    n = w.shape[1]
    tm, tn = ROW_TILE, 256
    tiles_per_seq = seq // tm
    n_lat_tiles = n_batch * tiles_per_seq

    def tab_map(i, j):
        return (jnp.where(i < n_lat_tiles, i % tiles_per_seq, tiles_per_seq), 0)

    def mod_map(i, j):
        return (_group_of_tile(i, tiles_per_seq, n_batch), 0, 0)

    tab_spec = pl.BlockSpec((tm, LANES), tab_map)
    return pl.pallas_call(
        functools.partial(_qkv_kernel, n_qk_tiles=n_qk_cols // tn, head_dim=head_dim),
        grid=(t // tm, n // tn),
        in_specs=[pl.BlockSpec((tm, d), lambda i, j: (i, 0)),
                  pl.BlockSpec((1, d), lambda i, j: (0, 0)),
                  pl.BlockSpec((1, 1, d), mod_map),
                  pl.BlockSpec((1, 1, d), mod_map),
                  pl.BlockSpec((d, tn), lambda i, j: (0, j)),
                  pl.BlockSpec((1, tn), lambda i, j: (0, j)),
                  pl.BlockSpec((tn, tn), lambda i, j: (0, 0)),
                  tab_spec, tab_spec, tab_spec],
        out_specs=pl.BlockSpec((tm, tn), lambda i, j: (i, j)),
        out_shape=jax.ShapeDtypeStruct((t, n), BF16),
        scratch_shapes=[pltpu.VMEM((tm, d), BF16)],
        compiler_params=_cparams("parallel", "arbitrary"),
        name="qkv_project",
    )(x, g, shift, scale, w, gain_row, gmat, *rope_tabs)


def rope_tables(seq, head_dim):
    q4 = head_dim // 4
    pos = np.arange(seq)
    inv = ROPE_THETA ** (-np.arange(q4, dtype=np.float64) / q4)
    lane = np.arange(LANES) % head_dim
    axis, half, idx = lane // (2 * q4), (lane // q4) % 2, lane % q4
    p = np.where(axis[None, :] == 0, (pos // GRID_W)[:, None], (pos % GRID_W)[:, None]).astype(np.float64)
    ang = (p.astype(np.float32) * inv.astype(np.float32)[idx][None, :]).astype(np.float32)
    cos, sin = np.cos(ang), np.sin(ang)
    sa = np.where(half[None, :] == 0, -sin, 0.0)
    sb = np.where(half[None, :] == 1, sin, 0.0)
    pad = np.zeros((ROW_TILE, LANES), np.float32)
    return tuple(jnp.asarray(np.concatenate([a.astype(np.float32), b], axis=0))
                 for a, b in ((cos, pad + 1.0), (sa, pad), (sb, pad)))


def head_group_matrix(width, head_dim):
    r = np.arange(width) // head_dim
    return jnp.asarray((r[:, None] == r[None, :]).astype(np.float32), dtype=BF16)


def _oproj_kernel(a_ref, w_ref, x_ref, gate_ref, o_ref):
    acc = jnp.dot(a_ref[...], w_ref[...], preferred_element_type=F32)
    o_ref[...] = x_ref[...] + gate_ref[0] * acc


def out_project(a, w, x, gate, *, n_rows, seq, n_batch):
    k = a.shape[1]
    d = w.shape[1]
    tm, tn = ROW_TILE, 512
    tiles_per_seq = seq // tm
    return pl.pallas_call(
        _oproj_kernel,
        grid=(n_rows // tm, d // tn),
        in_specs=[pl.BlockSpec((tm, k), lambda i, j: (i, 0)),
                  pl.BlockSpec((k, tn), lambda i, j: (0, j)),
                  pl.BlockSpec((tm, tn), lambda i, j: (i, j)),
                  pl.BlockSpec((1, 1, tn), lambda i, j: (_group_of_tile(i, tiles_per_seq, n_batch), 0, j))],
        out_specs=pl.BlockSpec((tm, tn), lambda i, j: (i, j)),
        out_shape=jax.ShapeDtypeStruct((n_rows, d), F32),
        compiler_params=_cparams("parallel", "parallel"),
        name="out_project",
    )(a, w, x, gate)


def _lane_halves(a, own_half):
    lane = lax.broadcasted_iota(I32, a.shape, 1)
    af = a.astype(F32)
    sw = pltpu.roll(af, LANES // 2, axis=1)
    lo_src, hi_src = (af, sw) if own_half == 0 else (sw, af)
    lo = jnp.where(lane < LANES // 2, lo_src, 0.0).astype(BF16)
    hi = jnp.where(lane >= LANES // 2, hi_src, 0.0).astype(BF16)
    return lo, hi


def _swa_kernel(sink_ref, q_ref, kp_ref, kc_ref, kn_ref, kx_ref, vp_ref, vc_ref, vn_ref, vx_ref, o_ref,
                *, n_blocks, blk, n_ctx):
    i = pl.program_id(1)
    is_lat = i < n_blocks
    kcat = jnp.concatenate([kp_ref[...], kc_ref[...], kn_ref[...], kx_ref[...]], axis=0)
    vcat = jnp.concatenate([vp_ref[...], vc_ref[...], vn_ref[...], vx_ref[...]], axis=0)
    n_keys = 3 * blk + n_ctx
    qi = lax.broadcasted_iota(I32, (blk, n_keys), 0)
    kj = lax.broadcasted_iota(I32, (blk, n_keys), 1)
    in_band = jnp.abs(kj - blk - qi) <= WINDOW
    first_ok = jnp.where(i > 0, 0, blk)
    last_ok = jnp.where(is_lat, jnp.where(i < n_blocks - 1, 3 * blk, 2 * blk), 0)
    valid = (kj >= 3 * blk) | (in_band & (kj >= first_ok) & (kj < last_ok))
    maskadd = jnp.where(valid, 0.0, NEG_BIG).astype(F32)
    rep = SWA_HEADS // SWA_KV_HEADS
    dn = (((1,), (1,)), ((), ()))
    for kv in range(SWA_KV_HEADS):
        ch, half = kv // 2, kv % 2
        klo, khi = _lane_halves(kcat[:, ch * LANES:(ch + 1) * LANES], half)
        vlo, vhi = _lane_halves(vcat[:, ch * LANES:(ch + 1) * LANES], half)
        for c in range(rep // 2):
            col = (kv * (rep // 2) + c) * LANES
            qc = q_ref[:, col:col + LANES]
            out = None
            for kk, vv, head in ((klo, vlo, kv * rep + 2 * c), (khi, vhi, kv * rep + 2 * c + 1)):
                s = lax.dot_general(qc, kk, dn, preferred_element_type=F32) + maskadd
                sink = sink_ref[head]
                m = jnp.maximum(jnp.max(s, axis=1, keepdims=True), sink)
                p = jnp.exp(s - m)
                denom = jnp.sum(p, axis=1, keepdims=True) + jnp.exp(sink - m)
                o = jnp.dot(p.astype(BF16), vv, preferred_element_type=F32) * (1.0 / denom)
                out = o if out is None else out + o
            o_ref[:, col:col + LANES] = out.astype(o_ref.dtype)


def swa_attention(qkv, sink, *, n_batch, seq, n_ctx, with_ctx_queries):
    blk = WINDOW
    nb = seq // blk
    nbc = n_ctx // blk if with_ctx_queries else 0
    n_q = SWA_HEADS * SWA_HEAD_DIM
    kcol, vcol = n_q // 256, n_q // 256 + 1
    lat_blocks = n_batch * nb
    ctx_row0 = n_batch * seq

    def qrow(b, i):
        return jnp.where(i < nb, b * nb + i, lat_blocks + b * (n_ctx // blk) + (i - nb))

    def win(delta, col):
        return pl.BlockSpec((blk, 256), lambda b, i, s: (b * nb + jnp.clip(i + delta, 0, nb - 1), col))

    def ctx(col):
        return pl.BlockSpec((n_ctx, 256), lambda b, i, s: (ctx_row0 // n_ctx + b, col))

    n_rows = n_batch * (seq + (n_ctx if with_ctx_queries else 0))
    grid_spec = pltpu.PrefetchScalarGridSpec(
        num_scalar_prefetch=1,
        grid=(n_batch, nb + nbc),
        in_specs=[pl.BlockSpec((blk, n_q), lambda b, i, s: (qrow(b, i), 0)),
                  win(-1, kcol), win(0, kcol), win(1, kcol), ctx(kcol),
                  win(-1, vcol), win(0, vcol), win(1, vcol), ctx(vcol)],
        out_specs=pl.BlockSpec((blk, n_q), lambda b, i, s: (qrow(b, i), 0)),
    )
    return pl.pallas_call(
        functools.partial(_swa_kernel, n_blocks=nb, blk=blk, n_ctx=n_ctx),
        grid_spec=grid_spec,
        out_shape=jax.ShapeDtypeStruct((n_rows, n_q), BF16),
        compiler_params=_cparams("parallel", "parallel"),
        name="swa_attention",
    )(sink, qkv, qkv, qkv, qkv, qkv, qkv, qkv, qkv, qkv)


def _ga_kernel(q_ref, kl_ref, vl_ref, kx_ref, vx_ref, o_ref, s_sc, vt_sc, vtx_sc, acc_sc, *, n_lat_blocks, tq, tk):
    i = pl.program_id(2)
    rep = GA_HEADS // GA_KV_HEADS
    hd = GA_HEAD_DIM
    n_ctx = kx_ref.shape[0]
    seq = kl_ref.shape[0]
    dn = (((1,), (1,)), ((), ()))

    @pl.when(i == 0)
    def _():
        vtx_sc[...] = vx_ref[...].astype(F32).T.astype(BF16)
        for c in range(seq // tk):
            vt_sc[c] = vl_ref[c * tk:(c + 1) * tk, :].astype(F32).T.astype(BF16)

    qs = jnp.concatenate([q_ref[:, r * hd:(r + 1) * hd] for r in range(rep)], axis=0)
    n_chunks = jnp.where(i < n_lat_blocks, seq // tk, 0)

    s = lax.dot_general(kx_ref[...], qs, dn, preferred_element_type=F32)
    s_sc[0:n_ctx, :] = s

    def scores(c, m):
        r0 = pl.multiple_of(c * tk, tk)
        s = lax.dot_general(kl_ref[pl.ds(r0, tk), :], qs, dn, preferred_element_type=F32)
        s_sc[pl.ds(n_ctx + r0, tk), :] = s
        return jnp.maximum(m, jnp.max(s, axis=0, keepdims=True))

    m = lax.fori_loop(0, n_chunks, scores, jnp.max(s, axis=0, keepdims=True))

    p = jnp.exp(s_sc[0:n_ctx, :] - m)
    acc_sc[...] = jnp.dot(vtx_sc[...], p.astype(BF16), preferred_element_type=F32)

    def values(c, l):
        r0 = pl.multiple_of(n_ctx + c * tk, LANES)
        p = jnp.exp(s_sc[pl.ds(r0, tk), :] - m)
        acc_sc[...] += jnp.dot(vt_sc[c], p.astype(BF16), preferred_element_type=F32)
        return l + jnp.sum(p, axis=0, keepdims=True)

    l = lax.fori_loop(0, n_chunks, values, jnp.sum(p, axis=0, keepdims=True))
    out = (acc_sc[...] * (1.0 / l)).T
    for r in range(rep):
        o_ref[:, r * hd:(r + 1) * hd] = out[r * tq:(r + 1) * tq].astype(o_ref.dtype)


def global_attention(qkv, *, n_batch, seq, n_ctx, with_ctx_queries):
    tq, tk = 256, 512
    hd = GA_HEAD_DIM
    rep = GA_HEADS // GA_KV_HEADS
    n_q = GA_HEADS * hd
    nq = seq // tq
    nqc = n_ctx // tq if with_ctx_queries else 0
    kcol0, vcol0 = n_q // hd, n_q // hd + GA_KV_HEADS
    ctx_blk0 = n_batch * seq // n_ctx

    def qrow(b, i):
        return jnp.where(i < nq, b * nq + i, n_batch * nq + b * (n_ctx // tq) + (i - nq))

    n_rows = n_batch * (seq + (n_ctx if with_ctx_queries else 0))
    return pl.pallas_call(
        functools.partial(_ga_kernel, n_lat_blocks=nq, tq=tq, tk=tk),
        grid=(n_batch, GA_KV_HEADS, nq + nqc),
        in_specs=[pl.BlockSpec((tq, rep * hd), lambda b, k, i: (qrow(b, i), k)),
                  pl.BlockSpec((seq, hd), lambda b, k, i: (b, kcol0 + k)),
                  pl.BlockSpec((seq, hd), lambda b, k, i: (b, vcol0 + k)),
                  pl.BlockSpec((n_ctx, hd), lambda b, k, i: (ctx_blk0 + b, kcol0 + k)),
                  pl.BlockSpec((n_ctx, hd), lambda b, k, i: (ctx_blk0 + b, vcol0 + k))],
        out_specs=pl.BlockSpec((tq, rep * hd), lambda b, k, i: (qrow(b, i), k)),
        out_shape=jax.ShapeDtypeStruct((n_rows, n_q), BF16),
        scratch_shapes=[pltpu.VMEM((n_ctx + seq, rep * tq), F32), pltpu.VMEM((seq // tk, hd, tk), BF16),
                        pltpu.VMEM((hd, n_ctx), BF16), pltpu.VMEM((hd, rep * tq), F32)],
        compiler_params=_cparams("parallel", "parallel", "arbitrary"),
        name="global_attention",
    )(qkv, qkv, qkv, qkv, qkv)


def _ssd_inproj_kernel(x_ref, g_ref, sh_ref, sc_ref, w_ref, o_ref, dt_ref, h_ref, *, n_main_tiles):
    j = pl.program_id(1)

    @pl.when(j == 0)
    def _():
        h_ref[...] = _norm_modulate(x_ref, g_ref, sh_ref, sc_ref).astype(BF16)

    acc = jnp.dot(h_ref[...], w_ref[...], preferred_element_type=F32)

    @pl.when(j < n_main_tiles)
    def _():
        o_ref[...] = acc.astype(o_ref.dtype)

    @pl.when(j == n_main_tiles)
    def _():
        dt_ref[...] = acc[:, :LANES]


def ssd_in_project(x, g, shift, scale, w_pad, *, n_main, seq, n_batch):
    t, d = x.shape
    tm, tn = ROW_TILE, 256
    tiles_per_seq = seq // tm
    n_main_tiles = n_main // tn

    def mod_map(i, j):
        return (_group_of_tile(i, tiles_per_seq, n_batch), 0, 0)

    return pl.pallas_call(
        functools.partial(_ssd_inproj_kernel, n_main_tiles=n_main_tiles),
        grid=(t // tm, n_main_tiles + 1),
        in_specs=[pl.BlockSpec((tm, d), lambda i, j: (i, 0)),
                  pl.BlockSpec((1, d), lambda i, j: (0, 0)),
                  pl.BlockSpec((1, 1, d), mod_map),
                  pl.BlockSpec((1, 1, d), mod_map),
                  pl.BlockSpec((d, tn), lambda i, j: (0, j))],
        out_specs=[pl.BlockSpec((tm, tn), lambda i, j: (i, jnp.minimum(j, n_main_tiles - 1))),
                   pl.BlockSpec((tm, LANES), lambda i, j: (i, 0))],
        out_shape=[jax.ShapeDtypeStruct((t, n_main), BF16), jax.ShapeDtypeStruct((t, LANES), F32)],
        scratch_shapes=[pltpu.VMEM((tm, d), BF16)],
        compiler_params=_cparams("parallel", "arbitrary"),
        name="ssd_in_project",
    )(x, g, shift, scale, w_pad)


CONV_HALO = 8
CONV_ROWS = 256


def _conv_kernel(u_ref, w_ref, b_ref, o_ref, pad_ref, *, seq_len):
    tc = u_ref.shape[1]
    zeros = jnp.zeros((CONV_HALO, tc), F32)
    pad_ref[0:CONV_HALO, :] = zeros
    pad_ref[CONV_HALO + seq_len:2 * CONV_HALO + seq_len, :] = zeros
    for c in range(seq_len // CONV_ROWS):
        r0 = c * CONV_ROWS
        pad_ref[CONV_HALO + r0:CONV_HALO + r0 + CONV_ROWS, :] = u_ref[r0:r0 + CONV_ROWS, :].astype(F32)
    w = w_ref[...]
    b = b_ref[...]
    for c in range(seq_len // CONV_ROWS):
        r0 = c * CONV_ROWS
        acc = b
        for k in range(SSD_CONV):
            off = CONV_HALO + r0 + k - SSD_CONV // 2
            acc = acc + w[k:k + 1, :] * pad_ref[off:off + CONV_ROWS, :]
        o_ref[r0:r0 + CONV_ROWS, :] = (acc * jax.nn.sigmoid(acc)).astype(o_ref.dtype)


def ssd_conv(main, conv_w, conv_b, *, seq_len, n_seqs, row_block0, col0):
    n_ch = conv_w.shape[1]
    tc = 256
    return pl.pallas_call(
        functools.partial(_conv_kernel, seq_len=seq_len),
        grid=(n_seqs, n_ch // tc),
        in_specs=[pl.BlockSpec((seq_len, tc), lambda b, j: (row_block0 + b, col0 // tc + j)),
                  pl.BlockSpec((SSD_CONV, tc), lambda b, j: (0, j)),
                  pl.BlockSpec((1, tc), lambda b, j: (0, j))],
        out_specs=pl.BlockSpec((seq_len, tc), lambda b, j: (b, j)),
        out_shape=jax.ShapeDtypeStruct((n_seqs * seq_len, n_ch), BF16),
        scratch_shapes=[pltpu.VMEM((seq_len + 2 * CONV_HALO, tc), F32)],
        compiler_params=_cparams("parallel", "parallel"),
        name="ssd_conv",
    )(main, conv_w, conv_b.reshape(1, n_ch))


DIR_LANES = 16


def _prep_kernel(raw_ref, bias_ref, alog_ref, acs_o, eacs_o, dtT_o, acsT_o, wT_o, etot_o):
    ch = SSD_CHUNK
    row = lax.broadcasted_iota(I32, (ch, LANES), 0)
    lane = lax.broadcasted_iota(I32, (ch, LANES), 1)
    is_fwd = (lane % DIR_LANES) < DIR_LANES // 2
    neg_a = -jnp.exp(alog_ref[0])
    for c in range(raw_ref.shape[1] // ch):
        rows = slice(c * ch, (c + 1) * ch)
        v = raw_ref[0, rows, :] + bias_ref[0]
        dt = jnp.maximum(v, 0.0) + jnp.log1p(jnp.exp(-jnp.abs(v)))
        a = dt * neg_a
        fwd, rev = a, a
        s = 1
        while s < ch:
            fwd = fwd + jnp.where(row >= s, pltpu.roll(fwd, s, axis=0), 0.0)
            rev = rev + jnp.where(row < ch - s, pltpu.roll(rev, ch - s, axis=0), 0.0)
            s *= 2
        acs = jnp.where(is_fwd, fwd, rev)
        tot = jnp.where(is_fwd[0:1], fwd[ch - 1:ch, :], rev[0:1, :])
        acs_o[0, rows, :] = acs
        eacs_o[0, rows, :] = jnp.exp(acs)
        dtT_o[0, c] = dt.T[:DIR_LANES]
        acsT_o[0, c] = acs.T[:DIR_LANES]
        wT_o[0, c] = (jnp.exp(tot - acs) * dt).T[:DIR_LANES]
        etot_o[0, c] = jnp.broadcast_to(jnp.exp(tot), (8, LANES))


def ssd_prep(dt_g, bias_g, alog_g):
    n_g, t, _ = dt_g.shape
    ch = SSD_CHUNK
    nck_all = t // ch
    per_step = max(k for k in range(1, 9) if nck_all % k == 0)
    nck = nck_all // per_step
    vec = pl.BlockSpec((1, 1, LANES), lambda g, c: (g, 0, 0))
    rows = pl.BlockSpec((1, per_step * ch, LANES), lambda g, c: (g, c, 0))
    tr = pl.BlockSpec((1, per_step, DIR_LANES, LANES), lambda g, c: (g, c, 0, 0))
    return pl.pallas_call(
        _prep_kernel,
        grid=(n_g, nck),
        in_specs=[rows, vec, vec],
        out_specs=[rows, rows, tr, tr, tr, pl.BlockSpec((1, per_step, 8, LANES), lambda g, c: (g, c, 0, 0))],
        out_shape=[jax.ShapeDtypeStruct((n_g, t, LANES), F32), jax.ShapeDtypeStruct((n_g, t, LANES), F32),
                   jax.ShapeDtypeStruct((n_g, nck_all, DIR_LANES, LANES), F32),
                   jax.ShapeDtypeStruct((n_g, nck_all, DIR_LANES, LANES), F32),
                   jax.ShapeDtypeStruct((n_g, nck_all, DIR_LANES, LANES), F32),
                   jax.ShapeDtypeStruct((n_g, nck_all, 8, LANES), F32)],
        compiler_params=_cparams("parallel", "parallel"),
        name="ssd_prep",
    )(dt_g, bias_g, alog_g)


HEADS_PER_GROUP = 8
PAIRS_PER_GROUP = HEADS_PER_GROUP // 2
GROUP_COLS = HEADS_PER_GROUP * SSD_HEAD_DIM


def _scan_kernel(*refs, n_chunks, emit_state, alias_out):
    (x_ref, b_ref, c_ref, z_ref, acs_ref, eacs_ref, dtT_ref, acsT_ref, wT_ref, etot_ref,
     dskip_ref, ng_ref, init_ref) = refs[:13]
    rest = refs[13 + (1 if alias_out else 0):]
    y_ref = rest[0]
    fin_ref = rest[1] if emit_state else None
    ysc, st = rest[-2], rest[-1]
    ch = SSD_CHUNK
    st[...] = init_ref[0, 0]
    lo = lax.broadcasted_iota(I32, (1, LANES), 1) < LANES // 2
    li = lax.broadcasted_iota(I32, (ch, ch), 0)
    si = lax.broadcasted_iota(I32, (ch, ch), 1)
    dn = (((1,), (1,)), ((), ()))
    for d in range(2):
        mask = (si <= li) if d == 0 else (si >= li)

        def chunk(cc, carry, d=d, mask=mask):
            c = cc if d == 0 else n_chunks - 1 - cc
            rows = pl.ds(pl.multiple_of(c * ch, ch), ch)
            bc, cm = b_ref[rows, :], c_ref[rows, :]
            cb = lax.dot_general(cm, bc, dn, preferred_element_type=F32)
            bt = bc.astype(F32).T
            cf = cm.astype(F32)
            acs_blk = acs_ref[0, rows, :]
            eacs_blk = eacs_ref[0, rows, :]
            et = etot_ref[0, c]
            for pr in range(PAIRS_PER_GROUP):
                cols = slice(pr * LANES, (pr + 1) * LANES)
                x2 = x_ref[rows, cols]
                zero = jnp.zeros_like(x2)
                s2 = st[d * PAIRS_PER_GROUP + pr]
                r_lo = d * HEADS_PER_GROUP + 2 * pr
                y2 = jnp.zeros((ch, LANES), F32)
                snew = s2 * jnp.where(lo, et[0:1, r_lo:r_lo + 1], et[0:1, r_lo + 1:r_lo + 2])
                for hh in range(2):
                    r = r_lo + hh
                    keep = lo if hh == 0 else jnp.logical_not(lo)
                    xm = jnp.where(keep, x2, zero)
                    sm = jnp.where(keep, s2, 0.0).astype(BF16)
                    lm = jnp.exp(jnp.where(mask, acs_blk[:, r:r + 1] - acsT_ref[0, c, r:r + 1, :], NEG_BIG))
                    m_h = (cb * lm * dtT_ref[0, c, r:r + 1, :]).astype(BF16)
                    c_h = (cf * eacs_blk[:, r:r + 1]).astype(BF16)
                    y2 = y2 + jnp.dot(m_h, xm, preferred_element_type=F32)
                    y2 = y2 + jnp.dot(c_h, sm, preferred_element_type=F32)
                    b_h = (bt * wT_ref[0, c, r:r + 1, :]).astype(BF16)
                    snew = snew + jnp.dot(b_h, xm, preferred_element_type=F32)
                st[d * PAIRS_PER_GROUP + pr] = snew
                if d == 0:
                    ysc[rows, cols] = y2
                else:
                    ysc[rows, cols] = ysc[rows, cols] + y2
            if d == 1:
                y = ysc[rows, :] + dskip_ref[0] * x_ref[rows, :].astype(F32)
                zf = z_ref[rows, :].astype(F32)
                y = y * (zf * jax.nn.sigmoid(zf))
                y = y * lax.rsqrt(jnp.mean(y * y, axis=-1, keepdims=True) + NORM_EPS) * ng_ref[0]
                y_ref[rows, :] = y.astype(y_ref.dtype)
            return carry

        lax.fori_loop(0, n_chunks, chunk, 0)
    if emit_state:
        fin_ref[0, 0] = st[...]


def ssd_scan(xbc, main, prep, dskip_g, normg_g, init_state, y_prev, *, seq_len, n_batch, row_block0,
             n_rows_total, emit_state):
    acs, eacs, dt_t, acs_t, w_t, etot = prep
    n_g = SSD_GROUPS
    nck = seq_len // SSD_CHUNK
    inner = n_g * GROUP_COLS
    bcol0 = inner // LANES
    ccol0 = bcol0 + n_g * SSD_STATE // LANES
    rb = row_block0

    def tr_spec(h):
        return pl.BlockSpec((1, nck, h, LANES), lambda b, g: (g, rb + b, 0, 0))

    vec = pl.BlockSpec((1, 1, GROUP_COLS), lambda b, g: (g, 0, 0))
    st_spec = pl.BlockSpec((1, 1, 2 * PAIRS_PER_GROUP, SSD_STATE, LANES), lambda b, g: (b, g, 0, 0, 0))
    in_specs = [pl.BlockSpec((seq_len, GROUP_COLS), lambda b, g: (b, g)),
                pl.BlockSpec((seq_len, LANES), lambda b, g: (b, bcol0 + g)),
                pl.BlockSpec((seq_len, LANES), lambda b, g: (b, ccol0 + g)),
                pl.BlockSpec((seq_len, GROUP_COLS), lambda b, g: (rb + b, g)),
                pl.BlockSpec((1, seq_len, LANES), lambda b, g: (g, rb + b, 0)),
                pl.BlockSpec((1, seq_len, LANES), lambda b, g: (g, rb + b, 0)),
                tr_spec(DIR_LANES), tr_spec(DIR_LANES), tr_spec(DIR_LANES), tr_spec(8),
                vec, vec, st_spec]
    args = [xbc, xbc, xbc, main, acs, eacs, dt_t, acs_t, w_t, etot, dskip_g, normg_g, init_state]
    aliases = {}
    if y_prev is not None:
        in_specs.append(pl.BlockSpec(memory_space=pl.ANY))
        args.append(y_prev)
        aliases = {len(args) - 1: 0}
    out_specs = [pl.BlockSpec((seq_len, GROUP_COLS), lambda b, g: (rb + b, g))]
    out_shape = [jax.ShapeDtypeStruct((n_rows_total, inner), BF16)]
    if emit_state:
        out_specs.append(st_spec)
        out_shape.append(jax.ShapeDtypeStruct(init_state.shape, F32))
    return pl.pallas_call(
        functools.partial(_scan_kernel, n_chunks=nck, emit_state=emit_state, alias_out=y_prev is not None),
        grid=(n_batch, n_g),
        in_specs=in_specs,
        out_specs=out_specs,
        out_shape=out_shape,
        scratch_shapes=[pltpu.VMEM((seq_len, GROUP_COLS), F32),
                        pltpu.VMEM((2 * PAIRS_PER_GROUP, SSD_STATE, LANES), F32)],
        input_output_aliases=aliases,
        compiler_params=_cparams("parallel", "parallel"),
        name="ssd_scan",
    )(*args)


def _router_kernel(x_ref, g_ref, sh_ref, sc_ref, whi_ref, wlo_ref, b_ref, tri_ref,
                   h_o, meta_o, gate_o, cnt_o, carry):
    i = pl.program_id(0)

    @pl.when(i == 0)
    def _():
        carry[...] = jnp.zeros_like(carry)

    h = _norm_modulate(x_ref, g_ref, sh_ref, sc_ref)
    _rows_to_slabs(h_o, h)
    h_hi = h.astype(BF16)
    h_lo = (h - h_hi.astype(F32)).astype(BF16)
    whi = whi_ref[...]
    logits = (jnp.dot(h_hi, whi, preferred_element_type=F32) + jnp.dot(h_lo, whi, preferred_element_type=F32)
              + jnp.dot(h_hi, wlo_ref[...], preferred_element_type=F32) + b_ref[...])
    tm = logits.shape[0]
    lane = lax.broadcasted_iota(I32, (tm, LANES), 1).astype(F32)
    work = logits
    vals, idxs = [], []
    for _ in range(TOP_K):
        m = jnp.max(work, axis=1, keepdims=True)
        idx = jnp.min(jnp.where(work == m, lane, float(LANES)), axis=1, keepdims=True)
        vals.append(m)
        idxs.append(idx)
        work = jnp.where(lane == idx, -jnp.inf, work)
    es = [jnp.exp(v - vals[0]) for v in vals]
    inv = 1.0 / (es[0] + es[1] + es[2] + es[3])
    multihot = sum(jnp.where(lane == idx, 1.0, 0.0) for idx in idxs)
    before = jnp.dot(tri_ref[...], multihot.astype(BF16), preferred_element_type=F32) + carry[...]
    meta = jnp.zeros((tm, LANES), F32)
    gates = jnp.zeros((tm, LANES), F32)
    for k in range(TOP_K):
        rank = jnp.sum(jnp.where(lane == idxs[k], before, 0.0), axis=1, keepdims=True)
        meta = meta + jnp.where(lane == float(k), idxs[k], 0.0) + jnp.where(lane == float(TOP_K + k), rank, 0.0)
        gates = gates + jnp.where(lane == float(k), es[k] * inv, 0.0)
    meta_o[...] = meta.astype(I32)
    gate_o[...] = gates
    carry[...] = carry[...] + jnp.sum(multihot, axis=0, keepdims=True)
    cnt_o[...] = carry[...]


def moe_router(x, g, shift, scale, w_hi, w_lo, b_pad, *, n_rows, seq, n_batch):
    d = x.shape[1]
    tm = ROW_TILE
    tiles_per_seq = seq // tm
    tri = jnp.asarray(np.tril(np.ones((tm, tm), np.float32), -1), dtype=BF16)

    def mod_map(i):
        return (_group_of_tile(i, tiles_per_seq, n_batch), 0, 0)

    full = lambda shape: pl.BlockSpec(shape, lambda i: (0,) * len(shape))
    return pl.pallas_call(
        _router_kernel,
        grid=(n_rows // tm,),
        in_specs=[pl.BlockSpec((tm, d), lambda i: (i, 0)), full((1, d)),
                  pl.BlockSpec((1, 1, d), mod_map), pl.BlockSpec((1, 1, d), mod_map),
                  full((d, LANES)), full((d, LANES)), full((1, LANES)), full((tm, tm))],
        out_specs=[pl.BlockSpec((tm, ROW_SLABS, LANES), lambda i: (i, 0, 0)),
                   pl.BlockSpec((tm, LANES), lambda i: (i, 0)),
                   pl.BlockSpec((tm, LANES), lambda i: (i, 0)), full((1, LANES))],
        out_shape=[jax.ShapeDtypeStruct((n_rows, ROW_SLABS, LANES), F32),
                   jax.ShapeDtypeStruct((n_rows, LANES), I32),
                   jax.ShapeDtypeStruct((n_rows, LANES), F32), jax.ShapeDtypeStruct((1, LANES), F32)],
        scratch_shapes=[pltpu.VMEM((1, LANES), F32)],
        compiler_params=_cparams("arbitrary"),
        name="moe_router",
    )(x, g, shift, scale, w_hi, w_lo, b_pad, tri)


ROW_SLABS = 16
DMA_UNROLL = 8
DISPATCH_TILE = 256


def _rows_to_slabs(ref, val):
    for c in range(ROW_SLABS):
        ref[:, c, :] = val[:, c * LANES:(c + 1) * LANES]


def _slabs_to_rows(ref, rows):
    return jnp.concatenate([ref[rows, c, :] for c in range(ROW_SLABS)], axis=1)


def _wait_rows(n, like_hbm, sem):
    pltpu.make_async_copy(like_hbm.at[pl.ds(0, n)], like_hbm.at[pl.ds(0, n)], sem).wait()


def _dispatch_kernel(tail_start_ref, tail_len_ref, nact_ref, dst_ref, h_ref, xs_hbm, zbuf, sem, zsem,
                     *, n_exp, n_blocks):
    s = pl.program_id(0)
    tm = h_ref.shape[0]

    @pl.when(s == 0)
    def _():
        zbuf[...] = jnp.zeros_like(zbuf)
        half = MOE_BLOCK // 2

        def unused(b, wait):
            for part in range(2):
                cp = pltpu.make_async_copy(zbuf, xs_hbm.at[pl.ds(b * MOE_BLOCK + part * half, half)], zsem)
                cp.wait() if wait else cp.start()

        lax.fori_loop(nact_ref[0], n_blocks, lambda b, c: (unused(b, False), c)[1], 0)
        lax.fori_loop(nact_ref[0], n_blocks, lambda b, c: (unused(b, True), c)[1], 0)
        sizes = [1 << b for b in range(MOE_BLOCK.bit_length() - 1)]

        def tails(e, wait):
            start, length = tail_start_ref[e], tail_len_ref[e]
            for size in sizes:
                @pl.when((length & size) != 0)
                def _(size=size):
                    pos = start + (length & ~(2 * size - 1))
                    cp = pltpu.make_async_copy(zbuf.at[pl.ds(0, size)], xs_hbm.at[pl.ds(pos, size)], zsem)
                    cp.wait() if wait else cp.start()

        lax.fori_loop(0, n_exp, lambda e, c: (tails(e, False), c)[1], 0)
        lax.fori_loop(0, n_exp, lambda e, c: (tails(e, True), c)[1], 0)

    def issue(it, c):
        for u in range(DMA_UNROLL):
            r = it * DMA_UNROLL + u
            pltpu.make_async_copy(h_ref.at[pl.ds(r // TOP_K, 1)], xs_hbm.at[pl.ds(dst_ref[0, 0, r], 1)], sem).start()
        return c

    lax.fori_loop(0, TOP_K * tm // DMA_UNROLL, issue, 0)
    _wait_rows(TOP_K * tm, xs_hbm, sem)


def moe_dispatch(tail_start, tail_len, n_active, dest, h3, *, n_blocks):
    n_rows = h3.shape[0]
    tm = DISPATCH_TILE
    n_exp = tail_start.shape[0]
    grid_spec = pltpu.PrefetchScalarGridSpec(
        num_scalar_prefetch=3,
        grid=(n_rows // tm,),
        in_specs=[pl.BlockSpec((1, 1, TOP_K * tm), lambda s, a, b, c: (s, 0, 0), memory_space=pltpu.SMEM),
                  pl.BlockSpec((tm, ROW_SLABS, LANES), lambda s, a, b, c: (s, 0, 0))],
        out_specs=pl.BlockSpec(memory_space=pl.ANY),
        scratch_shapes=[pltpu.VMEM((MOE_BLOCK // 2, ROW_SLABS, LANES), F32),
                        pltpu.SemaphoreType.DMA(()), pltpu.SemaphoreType.DMA(())],
    )
    return pl.pallas_call(
        functools.partial(_dispatch_kernel, n_exp=n_exp, n_blocks=n_blocks),
        grid_spec=grid_spec,
        out_shape=jax.ShapeDtypeStruct((n_blocks * MOE_BLOCK, ROW_SLABS, LANES), F32),
        compiler_params=_cparams("arbitrary"),
        name="moe_dispatch",
    )(tail_start, tail_len, n_active, dest.reshape(n_rows // tm, 1, TOP_K * tm), h3)


def _ffn_kernel(bexp_ref, nact_ref, x_ref, wgu_ref, bgu_ref, wd_ref, bd_ref, o_ref):
    s = pl.program_id(0)

    @pl.when(s < nact_ref[0])
    def _():
        x = _slabs_to_rows(x_ref, slice(None)).astype(BF16)
        gu = jnp.dot(x, wgu_ref[0, 0], preferred_element_type=F32) + bgu_ref[0, 0]
        glu = jnp.minimum(gu[:, :EXPERT_FF], SWIGLU_LIMIT)
        lin = jnp.clip(gu[:, EXPERT_FF:], -SWIGLU_LIMIT, SWIGLU_LIMIT)
        act = glu * jax.nn.sigmoid(SWIGLU_ALPHA * glu) * (lin + 1.0)
        y = jnp.dot(act.astype(BF16), wd_ref[0, 0], preferred_element_type=F32) + bd_ref[0, 0]
        _rows_to_slabs(o_ref, y)

    @pl.when(s >= nact_ref[0])
    def _():
        o_ref[...] = jnp.zeros_like(o_ref)


def moe_ffn_blocks(block_exp, n_active, xs, w_gu, b_gu, w_down, b_down, *, layer):
    n_blocks = xs.shape[0] // MOE_BLOCK
    d, ff2 = w_gu.shape[2], w_gu.shape[3]
    wmap = lambda s, be, na: (layer, be[s], 0, 0)
    grid_spec = pltpu.PrefetchScalarGridSpec(
        num_scalar_prefetch=2,
        grid=(n_blocks,),
        in_specs=[pl.BlockSpec((MOE_BLOCK, ROW_SLABS, LANES), lambda s, be, na: (s, 0, 0)),
                  pl.BlockSpec((1, 1, d, ff2), wmap), pl.BlockSpec((1, 1, 1, ff2), wmap),
                  pl.BlockSpec((1, 1, ff2 // 2, d), wmap), pl.BlockSpec((1, 1, 1, d), wmap)],
        out_specs=pl.BlockSpec((MOE_BLOCK, ROW_SLABS, LANES), lambda s, be, na: (s, 0, 0)),
    )
    return pl.pallas_call(
        _ffn_kernel,
        grid_spec=grid_spec,
        out_shape=jax.ShapeDtypeStruct(xs.shape, F32),
        compiler_params=_cparams("arbitrary"),
        name="moe_ffn",
    )(block_exp, n_active, xs, w_gu, b_gu, w_down, b_down)


COMBINE_TILE = 128


def _combine_kernel(dst_ref, dstn_ref, rows_hbm, x_ref, gate_ref, mod_ref, o_ref, buf, sem):
    s = pl.program_id(0)
    n = pl.num_programs(0)
    slot = s % 2
    tm = COMBINE_TILE

    def gather(idx_ref, sl):
        def issue(it, c):
            for u in range(DMA_UNROLL):
                r = it * DMA_UNROLL + u
                pltpu.make_async_copy(rows_hbm.at[pl.ds(idx_ref[0, 0, r], 1)], buf.at[sl, pl.ds(r, 1)],
                                      sem.at[sl]).start()
            return c

        lax.fori_loop(0, TOP_K * tm // DMA_UNROLL, issue, 0)

    @pl.when(s == 0)
    def _():
        gather(dst_ref, 0)

    @pl.when(s + 1 < n)
    def _():
        gather(dstn_ref, 1 - slot)

    _wait_rows(TOP_K * tm, rows_hbm, sem.at[slot])
    gates = gate_ref[...]
    f = gates[:, 0:1] * _slabs_to_rows(buf.at[slot], slice(0, tm))
    for k in range(1, TOP_K):
        f = f + gates[:, k:k + 1] * _slabs_to_rows(buf.at[slot], slice(k * tm, (k + 1) * tm))
    o_ref[...] = x_ref[...] + mod_ref[0] * f


def moe_combine(dest_tiles, out_rows, x, gates, gate_mod, *, n_rows, seq, n_batch):
    d = x.shape[1]
    tm = COMBINE_TILE
    n_tiles = n_rows // tm
    tiles_per_seq = seq // tm
    return pl.pallas_call(
        _combine_kernel,
        grid=(n_tiles,),
        in_specs=[pl.BlockSpec((1, 1, TOP_K * tm), lambda s: (s, 0, 0), memory_space=pltpu.SMEM),
                  pl.BlockSpec((1, 1, TOP_K * tm), lambda s: (jnp.minimum(s + 1, n_tiles - 1), 0, 0),
                               memory_space=pltpu.SMEM),
                  pl.BlockSpec(memory_space=pl.ANY),
                  pl.BlockSpec((tm, d), lambda s: (s, 0)),
                  pl.BlockSpec((tm, LANES), lambda s: (s, 0)),
                  pl.BlockSpec((1, 1, d), lambda s: (_group_of_tile(s, tiles_per_seq, n_batch), 0, 0))],
        out_specs=pl.BlockSpec((tm, d), lambda s: (s, 0)),
        out_shape=jax.ShapeDtypeStruct((n_rows, d), F32),
        scratch_shapes=[pltpu.VMEM((2, TOP_K * tm, ROW_SLABS, LANES), F32), pltpu.SemaphoreType.DMA((2,))],
        compiler_params=_cparams("arbitrary"),
        name="moe_combine",
    )(dest_tiles, dest_tiles, out_rows, x, gates, gate_mod)


def moe_layer(x, g, shift, scale, gate_mod, w_router, b_router, w_gu, b_gu, w_down, b_down,
              *, layer, n_rows, seq, n_batch):
    n_exp = w_router.shape[1]
    w_pad = jnp.pad(w_router, ((0, 0), (0, LANES - n_exp)))
    w_hi = w_pad.astype(BF16)
    w_lo = (w_pad - w_hi.astype(F32)).astype(BF16)
    b_pad = jnp.concatenate([b_router, jnp.full((LANES - n_exp,), NEG_BIG, F32)])[None, :]
    h3, meta, gates, cnt = moe_router(x, g, shift, scale, w_hi, w_lo, b_pad, n_rows=n_rows, seq=seq, n_batch=n_batch)
    top_idx, rank = meta[:, :TOP_K], meta[:, TOP_K:2 * TOP_K]
    counts = cnt[0, :n_exp].astype(I32)
    padded = (counts + MOE_BLOCK - 1) // MOE_BLOCK * MOE_BLOCK
    pad_end = jnp.cumsum(padded)
    pad_start = pad_end - padded
    onehot = top_idx[:, :, None] == jnp.arange(n_exp, dtype=I32)[None, None, :]
    dest = jnp.sum(jnp.where(onehot, pad_start[None, None, :], 0), axis=-1) + rank
    n_blocks = n_rows * TOP_K // MOE_BLOCK + n_exp
    block_row0 = jnp.arange(n_blocks, dtype=I32) * MOE_BLOCK
    block_exp = jnp.minimum(jnp.sum((pad_end[None, :] <= block_row0[:, None]).astype(I32), axis=1), n_exp - 1)
    n_active = (pad_end[-1:] // MOE_BLOCK).astype(I32)
    xs = moe_dispatch(pad_start + counts, padded - counts, n_active, dest, h3, n_blocks=n_blocks)
    out_rows = moe_ffn_blocks(block_exp, n_active, xs, w_gu, b_gu, w_down, b_down, layer=layer)
    tm = COMBINE_TILE
    dest_tiles = dest.reshape(n_rows // tm, tm, TOP_K).transpose(0, 2, 1).reshape(n_rows // tm, 1, TOP_K * tm)
    return moe_combine(dest_tiles, out_rows, x, gates, gate_mod, n_rows=n_rows, seq=seq, n_batch=n_batch)


def attention_layer(x, g, shift, scale, gate_mod, w_in, q_gain, k_gain, w_out, sink, *, n_heads, n_kv, head_dim,
                    seq, n_ctx, n_batch, need_ctx):
    n_qk = (n_heads + n_kv) * head_dim
    gain_row = jnp.concatenate([jnp.tile(q_gain, n_heads) * head_dim ** -0.5, jnp.tile(k_gain, n_kv),
                                jnp.ones((n_kv * head_dim,), F32)])[None, :]
    qkv = qkv_project(x, g, shift, scale, w_in.astype(BF16), gain_row, head_group_matrix(256, head_dim),
                      rope_tables(seq, head_dim), n_qk_cols=n_qk, head_dim=head_dim, seq=seq, n_batch=n_batch)
    if sink is not None:
        att = swa_attention(qkv, sink, n_batch=n_batch, seq=seq, n_ctx=n_ctx, with_ctx_queries=need_ctx)
    else:
        att = global_attention(qkv, n_batch=n_batch, seq=seq, n_ctx=n_ctx, with_ctx_queries=need_ctx)
    n_rows = n_batch * (seq + (n_ctx if need_ctx else 0))
    return out_project(att, w_out.astype(BF16), x, gate_mod, n_rows=n_rows, seq=seq, n_batch=n_batch)


def ssd_layer(x, g, shift, scale, gate_mod, w_in, conv_w, conv_b, dt_bias, a_log, d_skip, norm_g, w_out,
              *, seq, n_ctx, n_batch):
    t, d = x.shape
    n_g = SSD_GROUPS
    n_heads = d_skip.shape[0]
    inner = n_heads * SSD_HEAD_DIM
    n_main = inner + conv_w.shape[1]
    w_pad = jnp.pad(w_in, ((0, 0), (0, n_main + 256 - w_in.shape[1]))).astype(BF16)
    main, dt_raw = ssd_in_project(x, g, shift, scale, w_pad, n_main=n_main, seq=seq, n_batch=n_batch)

    def to_groups(a):
        lead = a.shape[:-1]
        a = a.reshape(lead + (2, n_g, HEADS_PER_GROUP))
        a = jnp.moveaxis(a, -2, 0).reshape((n_g,) + lead + (DIR_LANES,))
        return jnp.pad(a, [(0, 0)] * (a.ndim - 1) + [(0, LANES - DIR_LANES)])

    prep = ssd_prep(to_groups(dt_raw[:, :2 * n_heads]), to_groups(dt_bias.reshape(1, -1)),
                    to_groups(a_log.reshape(1, -1)))
    dskip_g = jnp.repeat(d_skip, SSD_HEAD_DIM).reshape(n_g, 1, GROUP_COLS)
    normg_g = norm_g.reshape(n_g, 1, GROUP_COLS)
    n_lat = n_batch * seq
    xbc_c = ssd_conv(main, conv_w, conv_b, seq_len=n_ctx, n_seqs=n_batch, row_block0=n_lat // n_ctx, col0=inner)
    xbc_l = ssd_conv(main, conv_w, conv_b, seq_len=seq, n_seqs=n_batch, row_block0=0, col0=inner)
    zero_state = jnp.zeros((n_batch, n_g, 2 * PAIRS_PER_GROUP, SSD_STATE, LANES), F32)
    y = jnp.zeros((t, inner), BF16)
    y, ctx_state = ssd_scan(xbc_c, main, prep, dskip_g, normg_g, zero_state, y, seq_len=n_ctx, n_batch=n_batch,
                            row_block0=n_lat // n_ctx, n_rows_total=t, emit_state=True)
    (y,) = ssd_scan(xbc_l, main, prep, dskip_g, normg_g, ctx_state, y, seq_len=seq, n_batch=n_batch,
                    row_block0=0, n_rows_total=t, emit_state=False)
    return out_project(y, w_out.astype(BF16), x, gate_mod, n_rows=t, seq=seq, n_batch=n_batch)


def kernel(x, c, ctx, c_ctx, ada_w, ada_b, norm_mix, norm_ffn, swa_w_in, swa_q_norm, swa_k_norm, swa_sink,
           swa_w_out, ssd_w_in, ssd_conv_w, ssd_conv_b, ssd_dt_bias, ssd_a_log, ssd_d, ssd_norm, ssd_w_out,
           ga_w_in, ga_q_norm, ga_k_norm, ga_w_out, moe_w_router, moe_b_router, moe_w_gate_up, moe_b_gate_up,
           moe_w_down, moe_b_down):
    n_batch, seq, d = x.shape
    n_ctx = ctx.shape[1]
    n_lat = n_batch * seq
    xs = jnp.concatenate([x.reshape(n_lat, d), ctx.reshape(n_batch * n_ctx, d)], axis=0)
    cvec = jnp.concatenate([c, c_ctx[None, :], jnp.zeros((8 - n_batch - 1, d), F32)], axis=0)
    mods = adaln_all(cvec, ada_w, ada_b)[:, :n_batch + 1].reshape(DEPTH, n_batch + 1, 6, 1, d)
    dims = dict(seq=seq, n_ctx=n_ctx, n_batch=n_batch)
    w_gu_bf, w_down_bf = moe_w_gate_up.astype(BF16), moe_w_down.astype(BF16)
    for i in range(DEPTH):
        kind, j = i % N_MIXERS, i // N_MIXERS
        need_ctx = i < DEPTH - 1
        m = [mods[i, :, k] for k in range(6)]
        g_mix = norm_mix[i][None, :]
        if kind == 0:
            xs = attention_layer(xs, g_mix, m[0], m[1], m[2], swa_w_in[j], swa_q_norm[j], swa_k_norm[j],
                                 swa_w_out[j], swa_sink[j], n_heads=SWA_HEADS, n_kv=SWA_KV_HEADS,
                                 head_dim=SWA_HEAD_DIM, need_ctx=need_ctx, **dims)
        elif kind == 1:
            xs = ssd_layer(xs, g_mix, m[0], m[1], m[2], ssd_w_in[j], ssd_conv_w[j], ssd_conv_b[j], ssd_dt_bias[j],
                           ssd_a_log[j], ssd_d[j], ssd_norm[j], ssd_w_out[j], **dims)
        else:
            xs = attention_layer(xs, g_mix, m[0], m[1], m[2], ga_w_in[j], ga_q_norm[j], ga_k_norm[j], ga_w_out[j],
                                 None, n_heads=GA_HEADS, n_kv=GA_KV_HEADS, head_dim=GA_HEAD_DIM,
                                 need_ctx=need_ctx, **dims)
        n_rows = n_batch * (seq + (n_ctx if need_ctx else 0))
        xs = moe_layer(xs, norm_ffn[i][None, :], m[3], m[4], m[5], moe_w_router[i], moe_b_router[i],
                       w_gu_bf, moe_b_gate_up[:, :, None, :], w_down_bf, moe_b_down[:, :, None, :],
                       layer=i, n_rows=n_rows, seq=seq, n_batch=n_batch)
    return xs[:n_lat].reshape(n_batch, seq, d)
```

```python
import functools
import math

import jax
import jax.numpy as jnp
import numpy as np
from jax import lax
from jax.experimental import pallas as pl
from jax.experimental.pallas import tpu as pltpu

F32 = jnp.float32
BF16 = jnp.bfloat16
I32 = jnp.int32

DEPTH = 4
N_MIXERS = 3
GRID_W = 64
NORM_EPS = 1e-6
ROPE_THETA = 10000.0

SWA_HEADS, SWA_KV_HEADS, SWA_HEAD_DIM, WINDOW = 32, 4, 64, 128
SSD_HEAD_DIM, SSD_GROUPS, SSD_STATE, SSD_CONV, SSD_CHUNK = 64, 8, 128, 5, 128
GA_HEADS, GA_KV_HEADS, GA_HEAD_DIM = 16, 4, 128
N_EXPERTS, TOP_K, EXPERT_FF = 32, 4, 896
SWIGLU_ALPHA, SWIGLU_LIMIT = 1.702, 7.0
MOE_BLOCK = 256

LANES = 128
ROW_TILE = 512
NEG_BIG = -1e30
VMEM_LIMIT = 56 * 1024 * 1024


def _cparams(*sem):
    return pltpu.CompilerParams(dimension_semantics=sem, vmem_limit_bytes=VMEM_LIMIT)


def _adaln_kernel(c_ref, w_ref, b_ref, o_ref):
    a = c_ref[...]
    a = (a * jax.nn.sigmoid(a)).astype(BF16)
    o_ref[0] = jnp.dot(a, w_ref[0].astype(BF16), preferred_element_type=F32) + b_ref[0]


def adaln_all(cvec, ada_w, ada_b):
    n_layers, d, n = ada_w.shape
    tn = 1024
    return pl.pallas_call(
        _adaln_kernel,
        grid=(n_layers, n // tn),
        in_specs=[pl.BlockSpec((8, d), lambda l, j: (0, 0)),
                  pl.BlockSpec((1, d, tn), lambda l, j: (l, 0, j)),
                  pl.BlockSpec((1, 1, tn), lambda l, j: (l, 0, j))],
        out_specs=pl.BlockSpec((1, 8, tn), lambda l, j: (l, 0, j)),
        out_shape=jax.ShapeDtypeStruct((n_layers, 8, n), F32),
        compiler_params=_cparams("parallel", "parallel"),
        name="adaln",
    )(cvec, ada_w, ada_b.reshape(n_layers, 1, n))


def _norm_modulate(x_ref, g_ref, sh_ref, sc_ref):
    x = x_ref[...]
    ms = jnp.mean(x * x, axis=-1, keepdims=True)
    y = x * lax.rsqrt(ms + NORM_EPS) * g_ref[...]
    return y * (1.0 + sc_ref[0]) + sh_ref[0]


def _group_of_tile(i, tiles_per_seq, n_batch):
    return jnp.minimum(i // tiles_per_seq, n_batch)


def _qkv_kernel(x_ref, g_ref, sh_ref, sc_ref, w_ref, gain_ref, gmat_ref, cos_ref, sa_ref, sb_ref,
                o_ref, h_ref, *, n_qk_tiles, head_dim):
    j = pl.program_id(1)

    @pl.when(j == 0)
    def _():
        h_ref[...] = _norm_modulate(x_ref, g_ref, sh_ref, sc_ref).astype(BF16)

    sub = gmat_ref.shape[0]
    q4 = head_dim // 4
    for t in range(w_ref.shape[1] // sub):
        cols = slice(t * sub, (t + 1) * sub)
        acc = jnp.dot(h_ref[...], w_ref[:, cols], preferred_element_type=F32)
        sub_tile = j * (w_ref.shape[1] // sub) + t

        @pl.when(sub_tile < n_qk_tiles)
        def _(acc=acc, cols=cols):
            ss = jnp.dot((acc * acc).astype(BF16), gmat_ref[...], preferred_element_type=F32)
            yn = acc * lax.rsqrt(ss * (1.0 / head_dim) + NORM_EPS) * gain_ref[:, cols]
            cos, sa, sb = cos_ref[...], sa_ref[...], sb_ref[...]
            for c in range(sub // LANES):
                yc = yn[:, c * LANES:(c + 1) * LANES]
                up = pltpu.roll(yc, LANES - q4, axis=1)
                dn = pltpu.roll(yc, q4, axis=1)
                lanes = slice(cols.start + c * LANES, cols.start + (c + 1) * LANES)
                o_ref[:, lanes] = (yc * cos + up * sa + dn * sb).astype(o_ref.dtype)

        @pl.when(sub_tile >= n_qk_tiles)
        def _(acc=acc, cols=cols):
            o_ref[:, cols] = acc.astype(o_ref.dtype)


def _proj_row_tile(seq, n_ctx_rows):
    return next(tm for tm in (1024, 512) if seq % tm == 0 and n_ctx_rows % tm == 0)


def qkv_project(x, g, shift, scale, w, gain_row, gmat, rope_tabs, *, n_qk_cols, head_dim, seq, n_batch):
    t, d = x.shape
    n = w.shape[1]
    tm, tn = _proj_row_tile(seq, t - n_batch * seq), 512
    sub = gmat.shape[0]
    tiles_per_seq = seq // tm
    n_lat_tiles = n_batch * tiles_per_seq

    def tab_map(i, j):
        return (jnp.where(i < n_lat_tiles, i % tiles_per_seq, tiles_per_seq), 0)

    def mod_map(i, j):
        return (_group_of_tile(i, tiles_per_seq, n_batch), 0, 0)

    tab_spec = pl.BlockSpec((tm, LANES), tab_map)
    return pl.pallas_call(
        functools.partial(_qkv_kernel, n_qk_tiles=n_qk_cols // sub, head_dim=head_dim),
        grid=(t // tm, n // tn),
        in_specs=[pl.BlockSpec((tm, d), lambda i, j: (i, 0)),
                  pl.BlockSpec((1, d), lambda i, j: (0, 0)),
                  pl.BlockSpec((1, 1, d), mod_map),
                  pl.BlockSpec((1, 1, d), mod_map),
                  pl.BlockSpec((d, tn), lambda i, j: (0, j)),
                  pl.BlockSpec((1, tn), lambda i, j: (0, j)),
                  pl.BlockSpec((sub, sub), lambda i, j: (0, 0)),
                  tab_spec, tab_spec, tab_spec],
        out_specs=pl.BlockSpec((tm, tn), lambda i, j: (i, j)),
        out_shape=jax.ShapeDtypeStruct((t, n), BF16),
        scratch_shapes=[pltpu.VMEM((tm, d), BF16)],
        compiler_params=_cparams("parallel", "arbitrary"),
        name="qkv_project",
    )(x, g, shift, scale, w, gain_row, gmat, *rope_tabs)


def rope_tables(seq, head_dim, tile):
    q4 = head_dim // 4
    pos = np.arange(seq)
    inv = ROPE_THETA ** (-np.arange(q4, dtype=np.float64) / q4)
    lane = np.arange(LANES) % head_dim
    axis, half, idx = lane // (2 * q4), (lane // q4) % 2, lane % q4
    p = np.where(axis[None, :] == 0, (pos // GRID_W)[:, None], (pos % GRID_W)[:, None]).astype(np.float64)
    ang = (p.astype(np.float32) * inv.astype(np.float32)[idx][None, :]).astype(np.float32)
    cos, sin = np.cos(ang), np.sin(ang)
    sa = np.where(half[None, :] == 0, -sin, 0.0)
    sb = np.where(half[None, :] == 1, sin, 0.0)
    pad = np.zeros((tile, LANES), np.float32)
    return tuple(jnp.asarray(np.concatenate([a.astype(np.float32), b], axis=0))
                 for a, b in ((cos, pad + 1.0), (sa, pad), (sb, pad)))


def head_group_matrix(width, head_dim):
    r = np.arange(width) // head_dim
    return jnp.asarray((r[:, None] == r[None, :]).astype(np.float32), dtype=BF16)


def _oproj_kernel(a_ref, w_ref, x_ref, gate_ref, o_ref):
    acc = jnp.dot(a_ref[...], w_ref[...], preferred_element_type=F32)
    o_ref[...] = x_ref[...] + gate_ref[0] * acc


def out_project(a, w, x, gate, *, n_rows, seq, n_batch):
    k = a.shape[1]
    d = w.shape[1]
    tm, tn = ROW_TILE, 512
    tiles_per_seq = seq // tm
    return pl.pallas_call(
        _oproj_kernel,
        grid=(n_rows // tm, d // tn),
        in_specs=[pl.BlockSpec((tm, k), lambda i, j: (i, 0)),
                  pl.BlockSpec((k, tn), lambda i, j: (0, j)),
                  pl.BlockSpec((tm, tn), lambda i, j: (i, j)),
                  pl.BlockSpec((1, 1, tn), lambda i, j: (_group_of_tile(i, tiles_per_seq, n_batch), 0, j))],
        out_specs=pl.BlockSpec((tm, tn), lambda i, j: (i, j)),
        out_shape=jax.ShapeDtypeStruct((n_rows, d), F32),
        compiler_params=_cparams("parallel", "parallel"),
        name="out_project",
    )(a, w, x, gate)


def _lane_halves(a, own_half):
    lane = lax.broadcasted_iota(I32, a.shape, 1)
    af = a.astype(F32)
    sw = pltpu.roll(af, LANES // 2, axis=1)
    lo_src, hi_src = (af, sw) if own_half == 0 else (sw, af)
    lo = jnp.where(lane < LANES // 2, lo_src, 0.0).astype(BF16)
    hi = jnp.where(lane >= LANES // 2, hi_src, 0.0).astype(BF16)
    return lo, hi


def _swa_kernel(sink_ref, q_ref, kp_ref, kc_ref, kn_ref, kx_ref, vp_ref, vc_ref, vn_ref, vx_ref, o_ref,
                *, n_blocks, blk, n_ctx):
    i = pl.program_id(1)
    is_lat = i < n_blocks
    kcat = jnp.concatenate([kp_ref[...], kc_ref[...], kn_ref[...], kx_ref[...]], axis=0)
    vcat = jnp.concatenate([vp_ref[...], vc_ref[...], vn_ref[...], vx_ref[...]], axis=0)
    n_keys = 3 * blk + n_ctx
    kj = lax.broadcasted_iota(I32, (n_keys, blk), 0)
    qi = lax.broadcasted_iota(I32, (n_keys, blk), 1)
    in_band = jnp.abs(kj - blk - qi) <= WINDOW
    first_ok = jnp.where(i > 0, 0, blk)
    last_ok = jnp.where(is_lat, jnp.where(i < n_blocks - 1, 3 * blk, 2 * blk), 0)
    valid = (kj >= 3 * blk) | (in_band & (kj >= first_ok) & (kj < last_ok))
    maskadd = jnp.where(valid, 0.0, NEG_BIG).astype(F32)
    rep = SWA_HEADS // SWA_KV_HEADS
    dn = (((1,), (1,)), ((), ()))
    drow = lax.broadcasted_iota(I32, (LANES, n_keys), 0)
    for kv in range(SWA_KV_HEADS):
        ch, half = kv // 2, kv % 2
        klo, khi = _lane_halves(kcat[:, ch * LANES:(ch + 1) * LANES], half)
        vt = vcat[:, ch * LANES:(ch + 1) * LANES].astype(F32).T
        vsw = pltpu.roll(vt, LANES // 2, axis=0)
        lo_src, hi_src = (vt, vsw) if half == 0 else (vsw, vt)
        vlo = jnp.where(drow < LANES // 2, lo_src, 0.0).astype(BF16)
        vhi = jnp.where(drow >= LANES // 2, hi_src, 0.0).astype(BF16)
        for c in range(rep // 2):
            col = (kv * (rep // 2) + c) * LANES
            qc = q_ref[:, col:col + LANES]
            out_t = None
            for kk, vv, head in ((klo, vlo, kv * rep + 2 * c), (khi, vhi, kv * rep + 2 * c + 1)):
                s = lax.dot_general(kk, qc, dn, preferred_element_type=F32) + maskadd
                sink = sink_ref[head]
                m = jnp.maximum(jnp.max(s, axis=0, keepdims=True), sink)
                p = jnp.exp(s - m)
                denom = jnp.sum(p, axis=0, keepdims=True) + jnp.exp(sink - m)
                o = jnp.dot(vv, p.astype(BF16), preferred_element_type=F32) * (1.0 / denom)
                out_t = o if out_t is None else out_t + o
            o_ref[:, col:col + LANES] = out_t.T.astype(o_ref.dtype)


def swa_attention(qkv, sink, *, n_batch, seq, n_ctx, with_ctx_queries):
    blk = WINDOW
    nb = seq // blk
    nbc = n_ctx // blk if with_ctx_queries else 0
    n_q = SWA_HEADS * SWA_HEAD_DIM
    kcol, vcol = n_q // 256, n_q // 256 + 1
    lat_blocks = n_batch * nb
    ctx_row0 = n_batch * seq

    def qrow(b, i):
        return jnp.where(i < nb, b * nb + i, lat_blocks + b * (n_ctx // blk) + (i - nb))

    def win(delta, col):
        return pl.BlockSpec((blk, 256), lambda b, i, s: (b * nb + jnp.clip(i + delta, 0, nb - 1), col))

    def ctx(col):
        return pl.BlockSpec((n_ctx, 256), lambda b, i, s: (ctx_row0 // n_ctx + b, col))

    n_rows = n_batch * (seq + (n_ctx if with_ctx_queries else 0))
    grid_spec = pltpu.PrefetchScalarGridSpec(
        num_scalar_prefetch=1,
        grid=(n_batch, nb + nbc),
        in_specs=[pl.BlockSpec((blk, n_q), lambda b, i, s: (qrow(b, i), 0)),
                  win(-1, kcol), win(0, kcol), win(1, kcol), ctx(kcol),
                  win(-1, vcol), win(0, vcol), win(1, vcol), ctx(vcol)],
        out_specs=pl.BlockSpec((blk, n_q), lambda b, i, s: (qrow(b, i), 0)),
    )
    return pl.pallas_call(
        functools.partial(_swa_kernel, n_blocks=nb, blk=blk, n_ctx=n_ctx),
        grid_spec=grid_spec,
        out_shape=jax.ShapeDtypeStruct((n_rows, n_q), BF16),
        compiler_params=_cparams("parallel", "parallel"),
        name="swa_attention",
    )(sink, qkv, qkv, qkv, qkv, qkv, qkv, qkv, qkv, qkv)


def _ga_kernel(q_ref, kl_ref, vl_ref, kx_ref, vx_ref, o_ref, s_sc, vt_sc, vtx_sc, acc_sc, *, n_lat_blocks, tq, tk):
    i = pl.program_id(2)
    rep = GA_HEADS // GA_KV_HEADS
    hd = GA_HEAD_DIM
    n_ctx = kx_ref.shape[0]
    seq = kl_ref.shape[0]
    dn = (((1,), (1,)), ((), ()))

    @pl.when(i == 0)
    def _():
        vtx_sc[...] = vx_ref[...].astype(F32).T.astype(BF16)
        for c in range(seq // tk):
            vt_sc[c] = vl_ref[c * tk:(c + 1) * tk, :].astype(F32).T.astype(BF16)

    qs = jnp.concatenate([q_ref[:, r * hd:(r + 1) * hd] for r in range(rep)], axis=0)
    n_chunks = jnp.where(i < n_lat_blocks, seq // tk, 0)

    s = lax.dot_general(kx_ref[...], qs, dn, preferred_element_type=F32)
    s_sc[0:n_ctx, :] = s

    def scores(c, m):
        r0 = pl.multiple_of(c * tk, tk)
        s = lax.dot_general(kl_ref[pl.ds(r0, tk), :], qs, dn, preferred_element_type=F32)
        s_sc[pl.ds(n_ctx + r0, tk), :] = s
        return jnp.maximum(m, jnp.max(s, axis=0, keepdims=True))

    m = lax.fori_loop(0, n_chunks, scores, jnp.max(s, axis=0, keepdims=True))

    p = jnp.exp(s_sc[0:n_ctx, :] - m)
    acc_sc[...] = jnp.dot(vtx_sc[...], p.astype(BF16), preferred_element_type=F32)

    def values(c, l):
        r0 = pl.multiple_of(n_ctx + c * tk, LANES)
        p = jnp.exp(s_sc[pl.ds(r0, tk), :] - m)
        acc_sc[...] += jnp.dot(vt_sc[c], p.astype(BF16), preferred_element_type=F32)
        return l + jnp.sum(p, axis=0, keepdims=True)

    l = lax.fori_loop(0, n_chunks, values, jnp.sum(p, axis=0, keepdims=True))
    out = (acc_sc[...] * (1.0 / l)).T
    for r in range(rep):
        o_ref[:, r * hd:(r + 1) * hd] = out[r * tq:(r + 1) * tq].astype(o_ref.dtype)


def global_attention(qkv, *, n_batch, seq, n_ctx, with_ctx_queries):
    tq, tk = 256, 512
    hd = GA_HEAD_DIM
    rep = GA_HEADS // GA_KV_HEADS
    n_q = GA_HEADS * hd
    nq = seq // tq
    nqc = n_ctx // tq if with_ctx_queries else 0
    kcol0, vcol0 = n_q // hd, n_q // hd + GA_KV_HEADS
    ctx_blk0 = n_batch * seq // n_ctx

    def qrow(b, i):
        return jnp.where(i < nq, b * nq + i, n_batch * nq + b * (n_ctx // tq) + (i - nq))

    n_rows = n_batch * (seq + (n_ctx if with_ctx_queries else 0))
    return pl.pallas_call(
        functools.partial(_ga_kernel, n_lat_blocks=nq, tq=tq, tk=tk),
        grid=(n_batch, GA_KV_HEADS, nq + nqc),
        in_specs=[pl.BlockSpec((tq, rep * hd), lambda b, k, i: (qrow(b, i), k)),
                  pl.BlockSpec((seq, hd), lambda b, k, i: (b, kcol0 + k)),
                  pl.BlockSpec((seq, hd), lambda b, k, i: (b, vcol0 + k)),
                  pl.BlockSpec((n_ctx, hd), lambda b, k, i: (ctx_blk0 + b, kcol0 + k)),
                  pl.BlockSpec((n_ctx, hd), lambda b, k, i: (ctx_blk0 + b, vcol0 + k))],
        out_specs=pl.BlockSpec((tq, rep * hd), lambda b, k, i: (qrow(b, i), k)),
        out_shape=jax.ShapeDtypeStruct((n_rows, n_q), BF16),
        scratch_shapes=[pltpu.VMEM((n_ctx + seq, rep * tq), F32), pltpu.VMEM((seq // tk, hd, tk), BF16),
                        pltpu.VMEM((hd, n_ctx), BF16), pltpu.VMEM((hd, rep * tq), F32)],
        compiler_params=_cparams("parallel", "parallel", "arbitrary"),
        name="global_attention",
    )(qkv, qkv, qkv, qkv, qkv)


def _ssd_inproj_kernel(x_ref, g_ref, sh_ref, sc_ref, w_ref, o_ref, dt_ref, h_ref, *, n_main_tiles):
    j = pl.program_id(1)

    @pl.when(j == 0)
    def _():
        h_ref[...] = _norm_modulate(x_ref, g_ref, sh_ref, sc_ref).astype(BF16)

    acc = jnp.dot(h_ref[...], w_ref[...], preferred_element_type=F32)

    @pl.when(j < n_main_tiles)
    def _():
        o_ref[...] = acc.astype(o_ref.dtype)

    @pl.when(j == n_main_tiles)
    def _():
        dt_ref[...] = acc[:, :LANES]


SSD_PROJ_COLS = 512


def ssd_in_project(x, g, shift, scale, w_pad, *, n_main, seq, n_batch):
    t, d = x.shape
    tm, tn = _proj_row_tile(seq, t - n_batch * seq), SSD_PROJ_COLS
    tiles_per_seq = seq // tm
    n_main_tiles = n_main // tn

    def mod_map(i, j):
        return (_group_of_tile(i, tiles_per_seq, n_batch), 0, 0)

    return pl.pallas_call(
        functools.partial(_ssd_inproj_kernel, n_main_tiles=n_main_tiles),
        grid=(t // tm, n_main_tiles + 1),
        in_specs=[pl.BlockSpec((tm, d), lambda i, j: (i, 0)),
                  pl.BlockSpec((1, d), lambda i, j: (0, 0)),
                  pl.BlockSpec((1, 1, d), mod_map),
                  pl.BlockSpec((1, 1, d), mod_map),
                  pl.BlockSpec((d, tn), lambda i, j: (0, j))],
        out_specs=[pl.BlockSpec((tm, tn), lambda i, j: (i, jnp.minimum(j, n_main_tiles - 1))),
                   pl.BlockSpec((tm, LANES), lambda i, j: (i, 0))],
        out_shape=[jax.ShapeDtypeStruct((t, n_main), BF16), jax.ShapeDtypeStruct((t, LANES), F32)],
        scratch_shapes=[pltpu.VMEM((tm, d), BF16)],
        compiler_params=_cparams("parallel", "arbitrary"),
        name="ssd_in_project",
    )(x, g, shift, scale, w_pad)


CONV_HALO = 8
CONV_ROWS = 256


def _conv_kernel(u_ref, w_ref, b_ref, o_ref, pad_ref, *, seq_len):
    tc = u_ref.shape[1]
    zeros = jnp.zeros((CONV_HALO, tc), F32)
    pad_ref[0:CONV_HALO, :] = zeros
    pad_ref[CONV_HALO + seq_len:2 * CONV_HALO + seq_len, :] = zeros
    for c in range(seq_len // CONV_ROWS):
        r0 = c * CONV_ROWS
        pad_ref[CONV_HALO + r0:CONV_HALO + r0 + CONV_ROWS, :] = u_ref[r0:r0 + CONV_ROWS, :].astype(F32)
    w = w_ref[...]
    b = b_ref[...]
    for c in range(seq_len // CONV_ROWS):
        r0 = c * CONV_ROWS
        acc = b
        for k in range(SSD_CONV):
            off = CONV_HALO + r0 + k - SSD_CONV // 2
            acc = acc + w[k:k + 1, :] * pad_ref[off:off + CONV_ROWS, :]
        o_ref[r0:r0 + CONV_ROWS, :] = (acc * jax.nn.sigmoid(acc)).astype(o_ref.dtype)


def ssd_conv(main, conv_w, conv_b, *, seq_len, n_seqs, row_block0, col0):
    n_ch = conv_w.shape[1]
    tc = 256
    return pl.pallas_call(
        functools.partial(_conv_kernel, seq_len=seq_len),
        grid=(n_seqs, n_ch // tc),
        in_specs=[pl.BlockSpec((seq_len, tc), lambda b, j: (row_block0 + b, col0 // tc + j)),
                  pl.BlockSpec((SSD_CONV, tc), lambda b, j: (0, j)),
                  pl.BlockSpec((1, tc), lambda b, j: (0, j))],
        out_specs=pl.BlockSpec((seq_len, tc), lambda b, j: (b, j)),
        out_shape=jax.ShapeDtypeStruct((n_seqs * seq_len, n_ch), BF16),
        scratch_shapes=[pltpu.VMEM((seq_len + 2 * CONV_HALO, tc), F32)],
        compiler_params=_cparams("parallel", "parallel"),
        name="ssd_conv",
    )(main, conv_w, conv_b.reshape(1, n_ch))


DIR_LANES = 16


def _prep_kernel(raw_ref, bias_ref, alog_ref, acs_o, eacs_o, dtT_o, acsT_o, wT_o, etot_o):
    ch = SSD_CHUNK
    row = lax.broadcasted_iota(I32, (ch, LANES), 0)
    lane = lax.broadcasted_iota(I32, (ch, LANES), 1)
    is_fwd = (lane % DIR_LANES) < DIR_LANES // 2
    neg_a = -jnp.exp(alog_ref[0])
    for c in range(raw_ref.shape[1] // ch):
        rows = slice(c * ch, (c + 1) * ch)
        v = raw_ref[0, rows, :] + bias_ref[0]
        dt = jnp.maximum(v, 0.0) + jnp.log1p(jnp.exp(-jnp.abs(v)))
        a = dt * neg_a
        fwd, rev = a, a
        s = 1
        while s < ch:
            fwd = fwd + jnp.where(row >= s, pltpu.roll(fwd, s, axis=0), 0.0)
            rev = rev + jnp.where(row < ch - s, pltpu.roll(rev, ch - s, axis=0), 0.0)
            s *= 2
        acs = jnp.where(is_fwd, fwd, rev)
        tot = jnp.where(is_fwd[0:1], fwd[ch - 1:ch, :], rev[0:1, :])
        acs_o[0, rows, :] = acs
        eacs_o[0, rows, :] = jnp.exp(acs)
        dtT_o[0, c] = dt.T[:DIR_LANES]
        acsT_o[0, c] = acs.T[:DIR_LANES]
        wT_o[0, c] = (jnp.exp(tot - acs) * dt).T[:DIR_LANES]
        etot_o[0, c] = jnp.broadcast_to(jnp.exp(tot), (8, LANES))


def ssd_prep(dt_g, bias_g, alog_g):
    n_g, t, _ = dt_g.shape
    ch = SSD_CHUNK
    nck_all = t // ch
    per_step = max(k for k in range(1, 9) if nck_all % k == 0)
    nck = nck_all // per_step
    vec = pl.BlockSpec((1, 1, LANES), lambda g, c: (g, 0, 0))
    rows = pl.BlockSpec((1, per_step * ch, LANES), lambda g, c: (g, c, 0))
    tr = pl.BlockSpec((1, per_step, DIR_LANES, LANES), lambda g, c: (g, c, 0, 0))
    return pl.pallas_call(
        _prep_kernel,
        grid=(n_g, nck),
        in_specs=[rows, vec, vec],
        out_specs=[rows, rows, tr, tr, tr, pl.BlockSpec((1, per_step, 8, LANES), lambda g, c: (g, c, 0, 0))],
        out_shape=[jax.ShapeDtypeStruct((n_g, t, LANES), F32), jax.ShapeDtypeStruct((n_g, t, LANES), F32),
                   jax.ShapeDtypeStruct((n_g, nck_all, DIR_LANES, LANES), F32),
                   jax.ShapeDtypeStruct((n_g, nck_all, DIR_LANES, LANES), F32),
                   jax.ShapeDtypeStruct((n_g, nck_all, DIR_LANES, LANES), F32),
                   jax.ShapeDtypeStruct((n_g, nck_all, 8, LANES), F32)],
        compiler_params=_cparams("parallel", "parallel"),
        name="ssd_prep",
    )(dt_g, bias_g, alog_g)


HEADS_PER_GROUP = 8
PAIRS_PER_GROUP = HEADS_PER_GROUP // 2
GROUP_COLS = HEADS_PER_GROUP * SSD_HEAD_DIM


def _scan_kernel(*refs, n_chunks, emit_state, alias_out):
    (x_ref, b_ref, c_ref, z_ref, acs_ref, eacs_ref, dtT_ref, acsT_ref, wT_ref, etot_ref,
     dskip_ref, ng_ref, init_ref) = refs[:13]
    rest = refs[13 + (1 if alias_out else 0):]
    y_ref = rest[0]
    fin_ref = rest[1] if emit_state else None
    ysc, st = rest[-2], rest[-1]
    ch = SSD_CHUNK
    st[...] = init_ref[0, 0]
    lo = lax.broadcasted_iota(I32, (1, LANES), 1) < LANES // 2
    li = lax.broadcasted_iota(I32, (ch, ch), 0)
    si = lax.broadcasted_iota(I32, (ch, ch), 1)
    dn = (((1,), (1,)), ((), ()))
    for d in range(2):
        mask = (si <= li) if d == 0 else (si >= li)

        def chunk(cc, carry, d=d, mask=mask):
            c = cc if d == 0 else n_chunks - 1 - cc
            rows = pl.ds(pl.multiple_of(c * ch, ch), ch)
            bc, cm = b_ref[rows, :], c_ref[rows, :]
            cb = lax.dot_general(cm, bc, dn, preferred_element_type=F32)
            bt = bc.astype(F32).T
            cf = cm.astype(F32)
            acs_blk = acs_ref[0, rows, :]
            eacs_blk = eacs_ref[0, rows, :]
            et = etot_ref[0, c]
            for pr in range(PAIRS_PER_GROUP):
                cols = slice(pr * LANES, (pr + 1) * LANES)
                x2 = x_ref[rows, cols]
                zero = jnp.zeros_like(x2)
                s2 = st[d * PAIRS_PER_GROUP + pr]
                r_lo = d * HEADS_PER_GROUP + 2 * pr
                y2 = jnp.zeros((ch, LANES), F32)
                snew = s2 * jnp.where(lo, et[0:1, r_lo:r_lo + 1], et[0:1, r_lo + 1:r_lo + 2])
                for hh in range(2):
                    r = r_lo + hh
                    keep = lo if hh == 0 else jnp.logical_not(lo)
                    xm = jnp.where(keep, x2, zero)
                    sm = jnp.where(keep, s2, 0.0).astype(BF16)
                    lm = jnp.exp(jnp.where(mask, acs_blk[:, r:r + 1] - acsT_ref[0, c, r:r + 1, :], NEG_BIG))
                    m_h = (cb * lm * dtT_ref[0, c, r:r + 1, :]).astype(BF16)
                    c_h = (cf * eacs_blk[:, r:r + 1]).astype(BF16)
                    y2 = y2 + jnp.dot(m_h, xm, preferred_element_type=F32)
                    y2 = y2 + jnp.dot(c_h, sm, preferred_element_type=F32)
                    b_h = (bt * wT_ref[0, c, r:r + 1, :]).astype(BF16)
                    snew = snew + jnp.dot(b_h, xm, preferred_element_type=F32)
                st[d * PAIRS_PER_GROUP + pr] = snew
                if d == 0:
                    ysc[rows, cols] = y2
                else:
                    ysc[rows, cols] = ysc[rows, cols] + y2
            if d == 1:
                y = ysc[rows, :] + dskip_ref[0] * x_ref[rows, :].astype(F32)
                zf = z_ref[rows, :].astype(F32)
                y = y * (zf * jax.nn.sigmoid(zf))
                y = y * lax.rsqrt(jnp.mean(y * y, axis=-1, keepdims=True) + NORM_EPS) * ng_ref[0]
                y_ref[rows, :] = y.astype(y_ref.dtype)
            return carry

        lax.fori_loop(0, n_chunks, chunk, 0)
    if emit_state:
        fin_ref[0, 0] = st[...]


def ssd_scan(xbc, main, prep, dskip_g, normg_g, init_state, y_prev, *, seq_len, n_batch, row_block0,
             n_rows_total, emit_state):
    acs, eacs, dt_t, acs_t, w_t, etot = prep
    n_g = SSD_GROUPS
    nck = seq_len // SSD_CHUNK
    inner = n_g * GROUP_COLS
    bcol0 = inner // LANES
    ccol0 = bcol0 + n_g * SSD_STATE // LANES
    rb = row_block0

    def tr_spec(h):
        return pl.BlockSpec((1, nck, h, LANES), lambda b, g: (g, rb + b, 0, 0))

    vec = pl.BlockSpec((1, 1, GROUP_COLS), lambda b, g: (g, 0, 0))
    st_spec = pl.BlockSpec((1, 1, 2 * PAIRS_PER_GROUP, SSD_STATE, LANES), lambda b, g: (b, g, 0, 0, 0))
    in_specs = [pl.BlockSpec((seq_len, GROUP_COLS), lambda b, g: (b, g)),
                pl.BlockSpec((seq_len, LANES), lambda b, g: (b, bcol0 + g)),
                pl.BlockSpec((seq_len, LANES), lambda b, g: (b, ccol0 + g)),
                pl.BlockSpec((seq_len, GROUP_COLS), lambda b, g: (rb + b, g)),
                pl.BlockSpec((1, seq_len, LANES), lambda b, g: (g, rb + b, 0)),
                pl.BlockSpec((1, seq_len, LANES), lambda b, g: (g, rb + b, 0)),
                tr_spec(DIR_LANES), tr_spec(DIR_LANES), tr_spec(DIR_LANES), tr_spec(8),
                vec, vec, st_spec]
    args = [xbc, xbc, xbc, main, acs, eacs, dt_t, acs_t, w_t, etot, dskip_g, normg_g, init_state]
    aliases = {}
    if y_prev is not None:
        in_specs.append(pl.BlockSpec(memory_space=pl.ANY))
        args.append(y_prev)
        aliases = {len(args) - 1: 0}
    out_specs = [pl.BlockSpec((seq_len, GROUP_COLS), lambda b, g: (rb + b, g))]
    out_shape = [jax.ShapeDtypeStruct((n_rows_total, inner), BF16)]
    if emit_state:
        out_specs.append(st_spec)
        out_shape.append(jax.ShapeDtypeStruct(init_state.shape, F32))
    return pl.pallas_call(
        functools.partial(_scan_kernel, n_chunks=nck, emit_state=emit_state, alias_out=y_prev is not None),
        grid=(n_batch, n_g),
        in_specs=in_specs,
        out_specs=out_specs,
        out_shape=out_shape,
        scratch_shapes=[pltpu.VMEM((seq_len, GROUP_COLS), F32),
                        pltpu.VMEM((2 * PAIRS_PER_GROUP, SSD_STATE, LANES), F32)],
        input_output_aliases=aliases,
        compiler_params=_cparams("parallel", "parallel"),
        name="ssd_scan",
    )(*args)


def _router_kernel(x_ref, g_ref, sh_ref, sc_ref, whi_ref, wlo_ref, b_ref, tri_ref,
                   h_o, meta_o, gate_o, cnt_o, carry):
    i = pl.program_id(0)

    @pl.when(i == 0)
    def _():
        carry[...] = jnp.zeros_like(carry)

    h = _norm_modulate(x_ref, g_ref, sh_ref, sc_ref)
    _rows_to_slabs(h_o, h)
    h_hi = h.astype(BF16)
    h_lo = (h - h_hi.astype(F32)).astype(BF16)
    whi = whi_ref[...]
    logits = (jnp.dot(h_hi, whi, preferred_element_type=F32) + jnp.dot(h_lo, whi, preferred_element_type=F32)
              + jnp.dot(h_hi, wlo_ref[...], preferred_element_type=F32) + b_ref[...])
    tm = logits.shape[0]
    lane = lax.broadcasted_iota(I32, (tm, LANES), 1).astype(F32)
    work = logits
    vals, idxs = [], []
    for _ in range(TOP_K):
        m = jnp.max(work, axis=1, keepdims=True)
        idx = jnp.min(jnp.where(work == m, lane, float(LANES)), axis=1, keepdims=True)
        vals.append(m)
        idxs.append(idx)
        work = jnp.where(lane == idx, -jnp.inf, work)
    es = [jnp.exp(v - vals[0]) for v in vals]
    inv = 1.0 / (es[0] + es[1] + es[2] + es[3])
    multihot = sum(jnp.where(lane == idx, 1.0, 0.0) for idx in idxs)
    before = jnp.dot(tri_ref[...], multihot.astype(BF16), preferred_element_type=F32) + carry[...]
    meta = jnp.zeros((tm, LANES), F32)
    gates = jnp.zeros((tm, LANES), F32)
    for k in range(TOP_K):
        rank = jnp.sum(jnp.where(lane == idxs[k], before, 0.0), axis=1, keepdims=True)
        meta = meta + jnp.where(lane == float(k), idxs[k], 0.0) + jnp.where(lane == float(TOP_K + k), rank, 0.0)
        gates = gates + jnp.where(lane == float(k), es[k] * inv, 0.0)
    meta_o[...] = meta.astype(I32)
    gate_o[...] = gates
    carry[...] = carry[...] + jnp.sum(multihot, axis=0, keepdims=True)
    cnt_o[...] = carry[...]


def moe_router(x, g, shift, scale, w_hi, w_lo, b_pad, *, n_rows, seq, n_batch):
    d = x.shape[1]
    tm = ROW_TILE
    tiles_per_seq = seq // tm
    tri = jnp.asarray(np.tril(np.ones((tm, tm), np.float32), -1), dtype=BF16)

    def mod_map(i):
        return (_group_of_tile(i, tiles_per_seq, n_batch), 0, 0)

    full = lambda shape: pl.BlockSpec(shape, lambda i: (0,) * len(shape))
    return pl.pallas_call(
        _router_kernel,
        grid=(n_rows // tm,),
        in_specs=[pl.BlockSpec((tm, d), lambda i: (i, 0)), full((1, d)),
                  pl.BlockSpec((1, 1, d), mod_map), pl.BlockSpec((1, 1, d), mod_map),
                  full((d, LANES)), full((d, LANES)), full((1, LANES)), full((tm, tm))],
        out_specs=[pl.BlockSpec((tm, ROW_SLABS, LANES), lambda i: (i, 0, 0)),
                   pl.BlockSpec((tm, LANES), lambda i: (i, 0)),
                   pl.BlockSpec((tm, LANES), lambda i: (i, 0)), full((1, LANES))],
        out_shape=[jax.ShapeDtypeStruct((n_rows, ROW_SLABS, LANES), F32),
                   jax.ShapeDtypeStruct((n_rows, LANES), I32),
                   jax.ShapeDtypeStruct((n_rows, LANES), F32), jax.ShapeDtypeStruct((1, LANES), F32)],
        scratch_shapes=[pltpu.VMEM((1, LANES), F32)],
        compiler_params=_cparams("arbitrary"),
        name="moe_router",
    )(x, g, shift, scale, w_hi, w_lo, b_pad, tri)


ROW_SLABS = 16
DMA_UNROLL = 8
DISPATCH_TILE = 256
FF_CHUNK = 256
OUT_CHUNK = 512


def _rows_to_slabs(ref, val):
    for c in range(ROW_SLABS):
        ref[:, c, :] = val[:, c * LANES:(c + 1) * LANES]


def _slabs_to_rows(ref, rows):
    return jnp.concatenate([ref[rows, c, :] for c in range(ROW_SLABS)], axis=1)


def _wait_rows(n, like_hbm, sem):
    pltpu.make_async_copy(like_hbm.at[pl.ds(0, n)], like_hbm.at[pl.ds(0, n)], sem).wait()


def _dispatch_kernel(tail_start_ref, tail_len_ref, nact_ref, dst_ref, h_ref, xs_hbm, zbuf, sem, zsem,
                     *, n_exp, n_blocks):
    s = pl.program_id(0)
    tm = h_ref.shape[0]

    @pl.when(s == 0)
    def _():
        zbuf[...] = jnp.zeros_like(zbuf)
        half = MOE_BLOCK // 2

        def unused(b, wait):
            for part in range(2):
                cp = pltpu.make_async_copy(zbuf, xs_hbm.at[pl.ds(b * MOE_BLOCK + part * half, half)], zsem)
                cp.wait() if wait else cp.start()

        lax.fori_loop(nact_ref[0], n_blocks, lambda b, c: (unused(b, False), c)[1], 0)
        lax.fori_loop(nact_ref[0], n_blocks, lambda b, c: (unused(b, True), c)[1], 0)
        sizes = [1 << b for b in range(MOE_BLOCK.bit_length() - 1)]

        def tails(e, wait):
            start, length = tail_start_ref[e], tail_len_ref[e]
            for size in sizes:
                @pl.when((length & size) != 0)
                def _(size=size):
                    pos = start + (length & ~(2 * size - 1))
                    cp = pltpu.make_async_copy(zbuf.at[pl.ds(0, size)], xs_hbm.at[pl.ds(pos, size)], zsem)
                    cp.wait() if wait else cp.start()

        lax.fori_loop(0, n_exp, lambda e, c: (tails(e, False), c)[1], 0)
        lax.fori_loop(0, n_exp, lambda e, c: (tails(e, True), c)[1], 0)

    def issue(it, c):
        for u in range(DMA_UNROLL):
            tok = it * (DMA_UNROLL // TOP_K) + u // TOP_K
            pltpu.make_async_copy(h_ref.at[pl.ds(tok, 1)], xs_hbm.at[pl.ds(dst_ref[0, 0, it * DMA_UNROLL + u], 1)],
                                  sem).start(priority=u % 2)
        return c

    lax.fori_loop(0, TOP_K * tm // DMA_UNROLL, issue, 0)
    _wait_rows(TOP_K * tm, xs_hbm, sem)


def moe_dispatch(tail_start, tail_len, n_active, dest, h3, *, n_blocks):
    n_rows = h3.shape[0]
    tm = DISPATCH_TILE
    n_exp = tail_start.shape[0]
    grid_spec = pltpu.PrefetchScalarGridSpec(
        num_scalar_prefetch=3,
        grid=(n_rows // tm,),
        in_specs=[pl.BlockSpec((1, 1, TOP_K * tm), lambda s, a, b, c: (s, 0, 0), memory_space=pltpu.SMEM),
                  pl.BlockSpec((tm, ROW_SLABS, LANES), lambda s, a, b, c: (s, 0, 0))],
        out_specs=pl.BlockSpec(memory_space=pl.ANY),
        scratch_shapes=[pltpu.VMEM((MOE_BLOCK // 2, ROW_SLABS, LANES), F32),
                        pltpu.SemaphoreType.DMA(()), pltpu.SemaphoreType.DMA(())],
    )
    return pl.pallas_call(
        functools.partial(_dispatch_kernel, n_exp=n_exp, n_blocks=n_blocks),
        grid_spec=grid_spec,
        out_shape=jax.ShapeDtypeStruct((n_blocks * MOE_BLOCK, ROW_SLABS, LANES), F32),
        compiler_params=_cparams("arbitrary"),
        name="moe_dispatch",
    )(tail_start, tail_len, n_active, dest.reshape(n_rows // tm, 1, TOP_K * tm), h3)


def _ffn_kernel(bexp_ref, nact_ref, xs_hbm, wgu_ref, bgu_ref, wd_ref, bd_ref, o_ref, xbuf, sem):
    s = pl.program_id(0)
    slot = s % 2
    nact = nact_ref[0]

    def block_copies(blk, sl):
        rows = pl.ds(blk * MOE_BLOCK, MOE_BLOCK)
        return [pltpu.make_async_copy(xs_hbm.at[rows, c, :], xbuf.at[sl, :, c * LANES:(c + 1) * LANES], sem.at[sl])
                for c in range(ROW_SLABS)]

    @pl.when(jnp.logical_and(s == 0, nact > 0))
    def _():
        for cp in block_copies(0, 0):
            cp.start()

    @pl.when(s + 1 < nact)
    def _():
        for cp in block_copies(s + 1, 1 - slot):
            cp.start()

    @pl.when(s < nact)
    def _():
        for cp in block_copies(s, slot):
            cp.wait()
        x = xbuf[slot].astype(BF16)
        ff = EXPERT_FF

        def swiglu(glu, lin):
            glu = jnp.minimum(glu, SWIGLU_LIMIT)
            lin = jnp.clip(lin, -SWIGLU_LIMIT, SWIGLU_LIMIT)
            return (glu * jax.nn.sigmoid(SWIGLU_ALPHA * glu) * (lin + 1.0)).astype(BF16)

        acts = []
        full = ff // FF_CHUNK * FF_CHUNK
        for lo in range(0, full, FF_CHUNK):
            hi = lo + FF_CHUNK
            glu = jnp.dot(x, wgu_ref[0, 0, :, lo:hi], preferred_element_type=F32) + bgu_ref[0, 0, :, lo:hi]
            lin = (jnp.dot(x, wgu_ref[0, 0, :, ff + lo:ff + hi], preferred_element_type=F32)
                   + bgu_ref[0, 0, :, ff + lo:ff + hi])
            acts.append(swiglu(glu, lin))
        if full < ff:
            rem = ff - full
            w_rem = jnp.concatenate([wgu_ref[0, 0, :, full:ff], wgu_ref[0, 0, :, ff + full:2 * ff]], axis=1)
            b_rem = jnp.concatenate([bgu_ref[0, 0, :, full:ff], bgu_ref[0, 0, :, ff + full:2 * ff]], axis=1)
            gl = jnp.dot(x, w_rem, preferred_element_type=F32) + b_rem
            acts.append(swiglu(gl[:, :rem], gl[:, rem:]))
        act = jnp.concatenate(acts, axis=1)
        for n0 in range(0, o_ref.shape[1] * LANES, OUT_CHUNK):
            y = jnp.dot(act, wd_ref[0, 0, :, n0:n0 + OUT_CHUNK], preferred_element_type=F32)
            y = y + bd_ref[0, 0, :, n0:n0 + OUT_CHUNK]
            for c in range(OUT_CHUNK // LANES):
                o_ref[:, n0 // LANES + c, :] = y[:, c * LANES:(c + 1) * LANES]

    @pl.when(s >= nact)
    def _():
        o_ref[...] = jnp.zeros_like(o_ref)


def moe_ffn_blocks(block_exp, n_active, xs, w_gu, b_gu, w_down, b_down, *, layer):
    n_blocks = xs.shape[0] // MOE_BLOCK
    d, ff2 = w_gu.shape[2], w_gu.shape[3]
    wmap = lambda s, be, na: (layer, be[s], 0, 0)
    grid_spec = pltpu.PrefetchScalarGridSpec(
        num_scalar_prefetch=2,
        grid=(n_blocks,),
        in_specs=[pl.BlockSpec(memory_space=pl.ANY),
                  pl.BlockSpec((1, 1, d, ff2), wmap), pl.BlockSpec((1, 1, 1, ff2), wmap),
                  pl.BlockSpec((1, 1, ff2 // 2, d), wmap), pl.BlockSpec((1, 1, 1, d), wmap)],
        out_specs=pl.BlockSpec((MOE_BLOCK, ROW_SLABS, LANES), lambda s, be, na: (s, 0, 0)),
        scratch_shapes=[pltpu.VMEM((2, MOE_BLOCK, d), F32), pltpu.SemaphoreType.DMA((2,))],
    )
    return pl.pallas_call(
        _ffn_kernel,
        grid_spec=grid_spec,
        out_shape=jax.ShapeDtypeStruct(xs.shape, F32),
        compiler_params=_cparams("arbitrary"),
        name="moe_ffn",
    )(block_exp, n_active, xs, w_gu, b_gu, w_down, b_down)


COMBINE_TILE = 128


def _combine_kernel(dst_ref, dstn_ref, rows_hbm, x_ref, gate_ref, mod_ref, o_ref, buf, sem):
    s = pl.program_id(0)
    n = pl.num_programs(0)
    slot = s % 2
    tm = COMBINE_TILE

    def gather(idx_ref, sl):
        def issue(it, c):
            for u in range(DMA_UNROLL):
                r = it * DMA_UNROLL + u
                pltpu.make_async_copy(rows_hbm.at[pl.ds(idx_ref[0, 0, r], 1)], buf.at[sl, pl.ds(r, 1)],
                                      sem.at[sl]).start(priority=u % 2)
            return c

        lax.fori_loop(0, TOP_K * tm // DMA_UNROLL, issue, 0)

    @pl.when(s == 0)
    def _():
        gather(dst_ref, 0)

    @pl.when(s + 1 < n)
    def _():
        gather(dstn_ref, 1 - slot)

    _wait_rows(TOP_K * tm, rows_hbm, sem.at[slot])
    gates = gate_ref[...]
    g = [jnp.broadcast_to(gates[:, k:k + 1], (tm, LANES)) for k in range(TOP_K)]
    for c in range(ROW_SLABS):
        cols = slice(c * LANES, (c + 1) * LANES)
        f = g[0] * buf[slot, 0:tm, c, :]
        for k in range(1, TOP_K):
            f = f + g[k] * buf[slot, k * tm:(k + 1) * tm, c, :]
        o_ref[:, cols] = x_ref[:, cols] + mod_ref[0, :, cols] * f


def moe_combine(dest_tiles, out_rows, x, gates, gate_mod, *, n_rows, seq, n_batch):
    d = x.shape[1]
    tm = COMBINE_TILE
    n_tiles = n_rows // tm
    tiles_per_seq = seq // tm
    return pl.pallas_call(
        _combine_kernel,
        grid=(n_tiles,),
        in_specs=[pl.BlockSpec((1, 1, TOP_K * tm), lambda s: (s, 0, 0), memory_space=pltpu.SMEM),
                  pl.BlockSpec((1, 1, TOP_K * tm), lambda s: (jnp.minimum(s + 1, n_tiles - 1), 0, 0),
                               memory_space=pltpu.SMEM),
                  pl.BlockSpec(memory_space=pl.ANY),
                  pl.BlockSpec((tm, d), lambda s: (s, 0)),
                  pl.BlockSpec((tm, LANES), lambda s: (s, 0)),
                  pl.BlockSpec((1, 1, d), lambda s: (_group_of_tile(s, tiles_per_seq, n_batch), 0, 0))],
        out_specs=pl.BlockSpec((tm, d), lambda s: (s, 0)),
        out_shape=jax.ShapeDtypeStruct((n_rows, d), F32),
        scratch_shapes=[pltpu.VMEM((2, TOP_K * tm, ROW_SLABS, LANES), F32), pltpu.SemaphoreType.DMA((2,))],
        compiler_params=_cparams("arbitrary"),
        name="moe_combine",
    )(dest_tiles, dest_tiles, out_rows, x, gates, gate_mod)


def moe_layer(x, g, shift, scale, gate_mod, w_router, b_router, w_gu, b_gu, w_down, b_down,
              *, layer, n_rows, seq, n_batch):
    n_exp = w_router.shape[1]
    w_pad = jnp.pad(w_router, ((0, 0), (0, LANES - n_exp)))
    w_hi = w_pad.astype(BF16)
    w_lo = (w_pad - w_hi.astype(F32)).astype(BF16)
    b_pad = jnp.concatenate([b_router, jnp.full((LANES - n_exp,), NEG_BIG, F32)])[None, :]
    h3, meta, gates, cnt = moe_router(x, g, shift, scale, w_hi, w_lo, b_pad, n_rows=n_rows, seq=seq, n_batch=n_batch)
    top_idx, rank = meta[:, :TOP_K], meta[:, TOP_K:2 * TOP_K]
    counts = cnt[0, :n_exp].astype(I32)
    padded = (counts + MOE_BLOCK - 1) // MOE_BLOCK * MOE_BLOCK
    pad_end = jnp.cumsum(padded)
    pad_start = pad_end - padded
    onehot = top_idx[:, :, None] == jnp.arange(n_exp, dtype=I32)[None, None, :]
    dest = jnp.sum(jnp.where(onehot, pad_start[None, None, :], 0), axis=-1) + rank
    n_blocks = n_rows * TOP_K // MOE_BLOCK + n_exp
    block_row0 = jnp.arange(n_blocks, dtype=I32) * MOE_BLOCK
    block_exp = jnp.minimum(jnp.sum((pad_end[None, :] <= block_row0[:, None]).astype(I32), axis=1), n_exp - 1)
    n_active = (pad_end[-1:] // MOE_BLOCK).astype(I32)
    xs = moe_dispatch(pad_start + counts, padded - counts, n_active, dest, h3, n_blocks=n_blocks)
    out_rows = moe_ffn_blocks(block_exp, n_active, xs, w_gu, b_gu, w_down, b_down, layer=layer)
    tm = COMBINE_TILE
    dest_tiles = dest.reshape(n_rows // tm, tm, TOP_K).transpose(0, 2, 1).reshape(n_rows // tm, 1, TOP_K * tm)
    return moe_combine(dest_tiles, out_rows, x, gates, gate_mod, n_rows=n_rows, seq=seq, n_batch=n_batch)


def attention_layer(x, g, shift, scale, gate_mod, w_in, q_gain, k_gain, w_out, sink, *, n_heads, n_kv, head_dim,
                    seq, n_ctx, n_batch, need_ctx):
    n_qk = (n_heads + n_kv) * head_dim
    gain_row = jnp.concatenate([jnp.tile(q_gain, n_heads) * head_dim ** -0.5, jnp.tile(k_gain, n_kv),
                                jnp.ones((n_kv * head_dim,), F32)])[None, :]
    qkv = qkv_project(x, g, shift, scale, w_in.astype(BF16), gain_row, head_group_matrix(256, head_dim),
                      rope_tables(seq, head_dim, _proj_row_tile(seq, x.shape[0] - n_batch * seq)),
                      n_qk_cols=n_qk, head_dim=head_dim, seq=seq, n_batch=n_batch)
    if sink is not None:
        att = swa_attention(qkv, sink, n_batch=n_batch, seq=seq, n_ctx=n_ctx, with_ctx_queries=need_ctx)
    else:
        att = global_attention(qkv, n_batch=n_batch, seq=seq, n_ctx=n_ctx, with_ctx_queries=need_ctx)
    n_rows = n_batch * (seq + (n_ctx if need_ctx else 0))
    return out_project(att, w_out.astype(BF16), x, gate_mod, n_rows=n_rows, seq=seq, n_batch=n_batch)


def ssd_layer(x, g, shift, scale, gate_mod, w_in, conv_w, conv_b, dt_bias, a_log, d_skip, norm_g, w_out,
              *, seq, n_ctx, n_batch):
    t, d = x.shape
    n_g = SSD_GROUPS
    n_heads = d_skip.shape[0]
    inner = n_heads * SSD_HEAD_DIM
    n_main = inner + conv_w.shape[1]
    w_pad = jnp.pad(w_in, ((0, 0), (0, n_main + SSD_PROJ_COLS - w_in.shape[1]))).astype(BF16)
    main, dt_raw = ssd_in_project(x, g, shift, scale, w_pad, n_main=n_main, seq=seq, n_batch=n_batch)

    def to_groups(a):
        lead = a.shape[:-1]
        a = a.reshape(lead + (2, n_g, HEADS_PER_GROUP))
        a = jnp.moveaxis(a, -2, 0).reshape((n_g,) + lead + (DIR_LANES,))
        return jnp.pad(a, [(0, 0)] * (a.ndim - 1) + [(0, LANES - DIR_LANES)])

    prep = ssd_prep(to_groups(dt_raw[:, :2 * n_heads]), to_groups(dt_bias.reshape(1, -1)),
                    to_groups(a_log.reshape(1, -1)))
    dskip_g = jnp.repeat(d_skip, SSD_HEAD_DIM).reshape(n_g, 1, GROUP_COLS)
    normg_g = norm_g.reshape(n_g, 1, GROUP_COLS)
    n_lat = n_batch * seq
    xbc_c = ssd_conv(main, conv_w, conv_b, seq_len=n_ctx, n_seqs=n_batch, row_block0=n_lat // n_ctx, col0=inner)
    xbc_l = ssd_conv(main, conv_w, conv_b, seq_len=seq, n_seqs=n_batch, row_block0=0, col0=inner)
    zero_state = jnp.zeros((n_batch, n_g, 2 * PAIRS_PER_GROUP, SSD_STATE, LANES), F32)
    y = jnp.zeros((t, inner), BF16)
    y, ctx_state = ssd_scan(xbc_c, main, prep, dskip_g, normg_g, zero_state, y, seq_len=n_ctx, n_batch=n_batch,
                            row_block0=n_lat // n_ctx, n_rows_total=t, emit_state=True)
    (y,) = ssd_scan(xbc_l, main, prep, dskip_g, normg_g, ctx_state, y, seq_len=seq, n_batch=n_batch,
                    row_block0=0, n_rows_total=t, emit_state=False)
    return out_project(y, w_out.astype(BF16), x, gate_mod, n_rows=t, seq=seq, n_batch=n_batch)


def kernel(x, c, ctx, c_ctx, ada_w, ada_b, norm_mix, norm_ffn, swa_w_in, swa_q_norm, swa_k_norm, swa_sink,
           swa_w_out, ssd_w_in, ssd_conv_w, ssd_conv_b, ssd_dt_bias, ssd_a_log, ssd_d, ssd_norm, ssd_w_out,
           ga_w_in, ga_q_norm, ga_k_norm, ga_w_out, moe_w_router, moe_b_router, moe_w_gate_up, moe_b_gate_up,
           moe_w_down, moe_b_down):
    n_batch, seq, d = x.shape
    n_ctx = ctx.shape[1]
    n_lat = n_batch * seq
    xs = jnp.concatenate([x.reshape(n_lat, d), ctx.reshape(n_batch * n_ctx, d)], axis=0)
    cvec = jnp.concatenate([c, c_ctx[None, :], jnp.zeros((8 - n_batch - 1, d), F32)], axis=0)
    mods = adaln_all(cvec, ada_w, ada_b)[:, :n_batch + 1].reshape(DEPTH, n_batch + 1, 6, 1, d)
    dims = dict(seq=seq, n_ctx=n_ctx, n_batch=n_batch)
    w_gu_bf, w_down_bf = moe_w_gate_up.astype(BF16), moe_w_down.astype(BF16)
    for i in range(DEPTH):
        kind, j = i % N_MIXERS, i // N_MIXERS
        need_ctx = i < DEPTH - 1
        m = [mods[i, :, k] for k in range(6)]
        g_mix = norm_mix[i][None, :]
        if kind == 0:
            xs = attention_layer(xs, g_mix, m[0], m[1], m[2], swa_w_in[j], swa_q_norm[j], swa_k_norm[j],
                                 swa_w_out[j], swa_sink[j], n_heads=SWA_HEADS, n_kv=SWA_KV_HEADS,
                                 head_dim=SWA_HEAD_DIM, need_ctx=need_ctx, **dims)
        elif kind == 1:
            xs = ssd_layer(xs, g_mix, m[0], m[1], m[2], ssd_w_in[j], ssd_conv_w[j], ssd_conv_b[j], ssd_dt_bias[j],
                           ssd_a_log[j], ssd_d[j], ssd_norm[j], ssd_w_out[j], **dims)
        else:
            xs = attention_layer(xs, g_mix, m[0], m[1], m[2], ga_w_in[j], ga_q_norm[j], ga_k_norm[j], ga_w_out[j],
                                 None, n_heads=GA_HEADS, n_kv=GA_KV_HEADS, head_dim=GA_HEAD_DIM,
                                 need_ctx=need_ctx, **dims)
        n_rows = n_batch * (seq + (n_ctx if need_ctx else 0))
        xs = moe_layer(xs, norm_ffn[i][None, :], m[3], m[4], m[5], moe_w_router[i], moe_b_router[i],
                       w_gu_bf, moe_b_gate_up[:, :, None, :], w_down_bf, moe_b_down[:, :, None, :],
                       layer=i, n_rows=n_rows, seq=seq, n_batch=n_batch)
    return xs[:n_lat].reshape(n_batch, seq, d)
```

```python
import functools
import math

import jax
import jax.numpy as jnp
import numpy as np
from jax import lax
from jax.experimental import pallas as pl
from jax.experimental.pallas import tpu as pltpu

F32 = jnp.float32
BF16 = jnp.bfloat16
I32 = jnp.int32

DEPTH = 4
N_MIXERS = 3
GRID_W = 64
NORM_EPS = 1e-6
ROPE_THETA = 10000.0

SWA_HEADS, SWA_KV_HEADS, SWA_HEAD_DIM, WINDOW = 32, 4, 64, 128
SSD_HEAD_DIM, SSD_GROUPS, SSD_STATE, SSD_CONV, SSD_CHUNK = 64, 8, 128, 5, 128
GA_HEADS, GA_KV_HEADS, GA_HEAD_DIM = 16, 4, 128
N_EXPERTS, TOP_K, EXPERT_FF = 32, 4, 896
SWIGLU_ALPHA, SWIGLU_LIMIT = 1.702, 7.0
MOE_BLOCK = 256

LANES = 128
ROW_TILE = 512
NEG_BIG = -1e30
VMEM_LIMIT = 56 * 1024 * 1024


def _cparams(*sem):
    return pltpu.CompilerParams(dimension_semantics=sem, vmem_limit_bytes=VMEM_LIMIT)


def _adaln_kernel(c_ref, w_ref, b_ref, o_ref):
    a = c_ref[...]
    a = (a * jax.nn.sigmoid(a)).astype(BF16)
    o_ref[0] = jnp.dot(a, w_ref[0].astype(BF16), preferred_element_type=F32) + b_ref[0]


def adaln_all(cvec, ada_w, ada_b):
    n_layers, d, n = ada_w.shape
    tn = 1024
    return pl.pallas_call(
        _adaln_kernel,
        grid=(n_layers, n // tn),
        in_specs=[pl.BlockSpec((8, d), lambda l, j: (0, 0)),
                  pl.BlockSpec((1, d, tn), lambda l, j: (l, 0, j)),
                  pl.BlockSpec((1, 1, tn), lambda l, j: (l, 0, j))],
        out_specs=pl.BlockSpec((1, 8, tn), lambda l, j: (l, 0, j)),
        out_shape=jax.ShapeDtypeStruct((n_layers, 8, n), F32),
        compiler_params=_cparams("parallel", "parallel"),
        name="adaln",
    )(cvec, ada_w, ada_b.reshape(n_layers, 1, n))


def _norm_modulate(x_ref, g_ref, sh_ref, sc_ref):
    x = x_ref[...]
    ms = jnp.mean(x * x, axis=-1, keepdims=True)
    y = x * lax.rsqrt(ms + NORM_EPS) * g_ref[...]
    return y * (1.0 + sc_ref[0]) + sh_ref[0]


def _group_of_tile(i, tiles_per_seq, n_batch):
    return jnp.minimum(i // tiles_per_seq, n_batch)


def _qkv_kernel(x_ref, g_ref, sh_ref, sc_ref, w_ref, gain_ref, gmat_ref, cos_ref, sa_ref, sb_ref,
                o_ref, h_ref, *, n_qk_tiles, head_dim):
    j = pl.program_id(1)

    @pl.when(j == 0)
    def _():
        h_ref[...] = _norm_modulate(x_ref, g_ref, sh_ref, sc_ref).astype(BF16)

    sub = gmat_ref.shape[0]
    q4 = head_dim // 4
    for t in range(w_ref.shape[1] // sub):
        cols = slice(t * sub, (t + 1) * sub)
        acc = jnp.dot(h_ref[...], w_ref[:, cols], preferred_element_type=F32)
        sub_tile = j * (w_ref.shape[1] // sub) + t

        @pl.when(sub_tile < n_qk_tiles)
        def _(acc=acc, cols=cols):
            ss = jnp.dot((acc * acc).astype(BF16), gmat_ref[...], preferred_element_type=F32)
            yn = acc * lax.rsqrt(ss * (1.0 / head_dim) + NORM_EPS) * gain_ref[:, cols]
            cos, sa, sb = cos_ref[...], sa_ref[...], sb_ref[...]
            for c in range(sub // LANES):
                yc = yn[:, c * LANES:(c + 1) * LANES]
                up = pltpu.roll(yc, LANES - q4, axis=1)
                dn = pltpu.roll(yc, q4, axis=1)
                lanes = slice(cols.start + c * LANES, cols.start + (c + 1) * LANES)
                o_ref[:, lanes] = (yc * cos + up * sa + dn * sb).astype(o_ref.dtype)

        @pl.when(sub_tile >= n_qk_tiles)
        def _(acc=acc, cols=cols):
            o_ref[:, cols] = acc.astype(o_ref.dtype)


def _proj_row_tile(seq, n_ctx_rows):
    return next(tm for tm in (1024, 512) if seq % tm == 0 and n_ctx_rows % tm == 0)


def qkv_project(x, g, shift, scale, w, gain_row, gmat, rope_tabs, *, n_qk_cols, head_dim, seq, n_batch):
    t, d = x.shape
    n = w.shape[1]
    tm, tn = _proj_row_tile(seq, t - n_batch * seq), 512
    sub = gmat.shape[0]
    tiles_per_seq = seq // tm
    n_lat_tiles = n_batch * tiles_per_seq

    def tab_map(i, j):
        return (jnp.where(i < n_lat_tiles, i % tiles_per_seq, tiles_per_seq), 0)

    def mod_map(i, j):
        return (_group_of_tile(i, tiles_per_seq, n_batch), 0, 0)

    tab_spec = pl.BlockSpec((tm, LANES), tab_map)
    return pl.pallas_call(
        functools.partial(_qkv_kernel, n_qk_tiles=n_qk_cols // sub, head_dim=head_dim),
        grid=(t // tm, n // tn),
        in_specs=[pl.BlockSpec((tm, d), lambda i, j: (i, 0)),
                  pl.BlockSpec((1, d), lambda i, j: (0, 0)),
                  pl.BlockSpec((1, 1, d), mod_map),
                  pl.BlockSpec((1, 1, d), mod_map),
                  pl.BlockSpec((d, tn), lambda i, j: (0, j)),
                  pl.BlockSpec((1, tn), lambda i, j: (0, j)),
                  pl.BlockSpec((sub, sub), lambda i, j: (0, 0)),
                  tab_spec, tab_spec, tab_spec],
        out_specs=pl.BlockSpec((tm, tn), lambda i, j: (i, j)),
        out_shape=jax.ShapeDtypeStruct((t, n), BF16),
        scratch_shapes=[pltpu.VMEM((tm, d), BF16)],
        compiler_params=_cparams("parallel", "arbitrary"),
        name="qkv_project",
    )(x, g, shift, scale, w, gain_row, gmat, *rope_tabs)


def rope_tables(seq, head_dim, tile):
    q4 = head_dim // 4
    pos = np.arange(seq)
    inv = ROPE_THETA ** (-np.arange(q4, dtype=np.float64) / q4)
    lane = np.arange(LANES) % head_dim
    axis, half, idx = lane // (2 * q4), (lane // q4) % 2, lane % q4
    p = np.where(axis[None, :] == 0, (pos // GRID_W)[:, None], (pos % GRID_W)[:, None]).astype(np.float64)
    ang = (p.astype(np.float32) * inv.astype(np.float32)[idx][None, :]).astype(np.float32)
    cos, sin = np.cos(ang), np.sin(ang)
    sa = np.where(half[None, :] == 0, -sin, 0.0)
    sb = np.where(half[None, :] == 1, sin, 0.0)
    pad = np.zeros((tile, LANES), np.float32)
    return tuple(jnp.asarray(np.concatenate([a.astype(np.float32), b], axis=0))
                 for a, b in ((cos, pad + 1.0), (sa, pad), (sb, pad)))


def head_group_matrix(width, head_dim):
    r = np.arange(width) // head_dim
    return jnp.asarray((r[:, None] == r[None, :]).astype(np.float32), dtype=BF16)


def _oproj_kernel(a_ref, w_ref, x_ref, gate_ref, o_ref):
    acc = jnp.dot(a_ref[...], w_ref[...], preferred_element_type=F32)
    o_ref[...] = x_ref[...] + gate_ref[0] * acc


def out_project(a, w, x, gate, *, n_rows, seq, n_batch):
    k = a.shape[1]
    d = w.shape[1]
    tm, tn = _proj_row_tile(seq, a.shape[0] - n_batch * seq if a.shape[0] > n_batch * seq else seq), 512
    tiles_per_seq = seq // tm
    return pl.pallas_call(
        _oproj_kernel,
        grid=(n_rows // tm, d // tn),
        in_specs=[pl.BlockSpec((tm, k), lambda i, j: (i, 0)),
                  pl.BlockSpec((k, tn), lambda i, j: (0, j)),
                  pl.BlockSpec((tm, tn), lambda i, j: (i, j)),
                  pl.BlockSpec((1, 1, tn), lambda i, j: (_group_of_tile(i, tiles_per_seq, n_batch), 0, j))],
        out_specs=pl.BlockSpec((tm, tn), lambda i, j: (i, j)),
        out_shape=jax.ShapeDtypeStruct((n_rows, d), F32),
        compiler_params=_cparams("parallel", "parallel"),
        name="out_project",
    )(a, w, x, gate)


def _lane_halves(a, own_half):
    lane = lax.broadcasted_iota(I32, a.shape, 1)
    af = a.astype(F32)
    sw = pltpu.roll(af, LANES // 2, axis=1)
    lo_src, hi_src = (af, sw) if own_half == 0 else (sw, af)
    lo = jnp.where(lane < LANES // 2, lo_src, 0.0).astype(BF16)
    hi = jnp.where(lane >= LANES // 2, hi_src, 0.0).astype(BF16)
    return lo, hi


def _swa_kernel(sink_ref, q_ref, kp_ref, kc_ref, kn_ref, kx_ref, vp_ref, vc_ref, vn_ref, vx_ref, o_ref,
                *, n_blocks, blk, n_ctx):
    i = pl.program_id(1)
    is_lat = i < n_blocks
    kcat = jnp.concatenate([kp_ref[...], kc_ref[...], kn_ref[...], kx_ref[...]], axis=0)
    vcat = jnp.concatenate([vp_ref[...], vc_ref[...], vn_ref[...], vx_ref[...]], axis=0)
    n_keys = 3 * blk + n_ctx
    kj = lax.broadcasted_iota(I32, (n_keys, blk), 0)
    qi = lax.broadcasted_iota(I32, (n_keys, blk), 1)
    in_band = jnp.abs(kj - blk - qi) <= WINDOW
    first_ok = jnp.where(i > 0, 0, blk)
    last_ok = jnp.where(is_lat, jnp.where(i < n_blocks - 1, 3 * blk, 2 * blk), 0)
    valid = (kj >= 3 * blk) | (in_band & (kj >= first_ok) & (kj < last_ok))
    maskadd = jnp.where(valid, 0.0, NEG_BIG).astype(F32)
    rep = SWA_HEADS // SWA_KV_HEADS
    dn = (((1,), (1,)), ((), ()))
    drow = lax.broadcasted_iota(I32, (LANES, n_keys), 0)
    for kv in range(SWA_KV_HEADS):
        ch, half = kv // 2, kv % 2
        klo, khi = _lane_halves(kcat[:, ch * LANES:(ch + 1) * LANES], half)
        vt = vcat[:, ch * LANES:(ch + 1) * LANES].astype(F32).T
        vsw = pltpu.roll(vt, LANES // 2, axis=0)
        lo_src, hi_src = (vt, vsw) if half == 0 else (vsw, vt)
        vlo = jnp.where(drow < LANES // 2, lo_src, 0.0).astype(BF16)
        vhi = jnp.where(drow >= LANES // 2, hi_src, 0.0).astype(BF16)
        for c in range(rep // 2):
            col = (kv * (rep // 2) + c) * LANES
            qc = q_ref[:, col:col + LANES]
            out_t = None
            for kk, vv, head in ((klo, vlo, kv * rep + 2 * c), (khi, vhi, kv * rep + 2 * c + 1)):
                s = lax.dot_general(kk, qc, dn, preferred_element_type=F32) + maskadd
                sink = sink_ref[head]
                m = jnp.maximum(jnp.max(s, axis=0, keepdims=True), sink)
                p = jnp.exp(s - m)
                denom = jnp.sum(p, axis=0, keepdims=True) + jnp.exp(sink - m)
                o = jnp.dot(vv, p.astype(BF16), preferred_element_type=F32) * (1.0 / denom)
                out_t = o if out_t is None else out_t + o
            o_ref[:, col:col + LANES] = out_t.T.astype(o_ref.dtype)


def swa_attention(qkv, sink, *, n_batch, seq, n_ctx, with_ctx_queries):
    blk = WINDOW
    nb = seq // blk
    nbc = n_ctx // blk if with_ctx_queries else 0
    n_q = SWA_HEADS * SWA_HEAD_DIM
    kcol, vcol = n_q // 256, n_q // 256 + 1
    lat_blocks = n_batch * nb
    ctx_row0 = n_batch * seq

    def qrow(b, i):
        return jnp.where(i < nb, b * nb + i, lat_blocks + b * (n_ctx // blk) + (i - nb))

    def win(delta, col):
        return pl.BlockSpec((blk, 256), lambda b, i, s: (b * nb + jnp.clip(i + delta, 0, nb - 1), col))

    def ctx(col):
        return pl.BlockSpec((n_ctx, 256), lambda b, i, s: (ctx_row0 // n_ctx + b, col))

    n_rows = n_batch * (seq + (n_ctx if with_ctx_queries else 0))
    grid_spec = pltpu.PrefetchScalarGridSpec(
        num_scalar_prefetch=1,
        grid=(n_batch, nb + nbc),
        in_specs=[pl.BlockSpec((blk, n_q), lambda b, i, s: (qrow(b, i), 0)),
                  win(-1, kcol), win(0, kcol), win(1, kcol), ctx(kcol),
                  win(-1, vcol), win(0, vcol), win(1, vcol), ctx(vcol)],
        out_specs=pl.BlockSpec((blk, n_q), lambda b, i, s: (qrow(b, i), 0)),
    )
    return pl.pallas_call(
        functools.partial(_swa_kernel, n_blocks=nb, blk=blk, n_ctx=n_ctx),
        grid_spec=grid_spec,
        out_shape=jax.ShapeDtypeStruct((n_rows, n_q), BF16),
        compiler_params=_cparams("parallel", "parallel"),
        name="swa_attention",
    )(sink, qkv, qkv, qkv, qkv, qkv, qkv, qkv, qkv, qkv)


def _ga_kernel(q_ref, kl_ref, vl_ref, kx_ref, vx_ref, o_ref, vt_sc, vtx_sc, acc_sc, *, n_lat_blocks, tq, tk):
    i = pl.program_id(2)
    rep = GA_HEADS // GA_KV_HEADS
    hd = GA_HEAD_DIM
    n_ctx = kx_ref.shape[0]
    seq = kl_ref.shape[0]
    dn = (((1,), (1,)), ((), ()))

    @pl.when(i == 0)
    def _():
        vtx_sc[...] = vx_ref[...].astype(F32).T.astype(BF16)
        for c in range(seq // tk):
            vt_sc[c] = vl_ref[c * tk:(c + 1) * tk, :].astype(F32).T.astype(BF16)

    qs = jnp.concatenate([q_ref[:, r * hd:(r + 1) * hd] for r in range(rep)], axis=0)
    n_chunks = jnp.where(i < n_lat_blocks, seq // tk, 0)

    s_ctx = lax.dot_general(kx_ref[...], qs, dn, preferred_element_type=F32)

    def scores(c, m):
        r0 = pl.multiple_of(c * tk, tk)
        s = lax.dot_general(kl_ref[pl.ds(r0, tk), :], qs, dn, preferred_element_type=F32)
        return jnp.maximum(m, jnp.max(s, axis=0, keepdims=True))

    m = lax.fori_loop(0, n_chunks, scores, jnp.max(s_ctx, axis=0, keepdims=True))

    p = jnp.exp2(s_ctx - m)
    acc_sc[...] = jnp.dot(vtx_sc[...], p.astype(BF16), preferred_element_type=F32)

    def values(c, l):
        r0 = pl.multiple_of(c * tk, tk)
        s = lax.dot_general(kl_ref[pl.ds(r0, tk), :], qs, dn, preferred_element_type=F32)
        p = jnp.exp2(s - m)
        acc_sc[...] += jnp.dot(vt_sc[c], p.astype(BF16), preferred_element_type=F32)
        return l + jnp.sum(p, axis=0, keepdims=True)

    l = lax.fori_loop(0, n_chunks, values, jnp.sum(p, axis=0, keepdims=True))
    out = (acc_sc[...] * (1.0 / l)).T
    for r in range(rep):
        o_ref[:, r * hd:(r + 1) * hd] = out[r * tq:(r + 1) * tq].astype(o_ref.dtype)


def global_attention(qkv, *, n_batch, seq, n_ctx, with_ctx_queries):
    tq, tk = 256, 512
    hd = GA_HEAD_DIM
    rep = GA_HEADS // GA_KV_HEADS
    n_q = GA_HEADS * hd
    nq = seq // tq
    nqc = n_ctx // tq if with_ctx_queries else 0
    kcol0, vcol0 = n_q // hd, n_q // hd + GA_KV_HEADS
    ctx_blk0 = n_batch * seq // n_ctx

    def qrow(b, i):
        return jnp.where(i < nq, b * nq + i, n_batch * nq + b * (n_ctx // tq) + (i - nq))

    n_rows = n_batch * (seq + (n_ctx if with_ctx_queries else 0))
    return pl.pallas_call(
        functools.partial(_ga_kernel, n_lat_blocks=nq, tq=tq, tk=tk),
        grid=(n_batch, GA_KV_HEADS, nq + nqc),
        in_specs=[pl.BlockSpec((tq, rep * hd), lambda b, k, i: (qrow(b, i), k)),
                  pl.BlockSpec((seq, hd), lambda b, k, i: (b, kcol0 + k)),
                  pl.BlockSpec((seq, hd), lambda b, k, i: (b, vcol0 + k)),
                  pl.BlockSpec((n_ctx, hd), lambda b, k, i: (ctx_blk0 + b, kcol0 + k)),
                  pl.BlockSpec((n_ctx, hd), lambda b, k, i: (ctx_blk0 + b, vcol0 + k))],
        out_specs=pl.BlockSpec((tq, rep * hd), lambda b, k, i: (qrow(b, i), k)),
        out_shape=jax.ShapeDtypeStruct((n_rows, n_q), BF16),
        scratch_shapes=[pltpu.VMEM((seq // tk, hd, tk), BF16),
                        pltpu.VMEM((hd, n_ctx), BF16), pltpu.VMEM((hd, rep * tq), F32)],
        compiler_params=_cparams("parallel", "parallel", "arbitrary"),
        name="global_attention",
    )(qkv, qkv, qkv, qkv, qkv)


def _ssd_inproj_kernel(x_ref, g_ref, sh_ref, sc_ref, w_ref, o_ref, dt_ref, h_ref, *, n_main_tiles):
    j = pl.program_id(1)

    @pl.when(j == 0)
    def _():
        h_ref[...] = _norm_modulate(x_ref, g_ref, sh_ref, sc_ref).astype(BF16)

    acc = jnp.dot(h_ref[...], w_ref[...], preferred_element_type=F32)

    @pl.when(j < n_main_tiles)
    def _():
        o_ref[...] = acc.astype(o_ref.dtype)

    @pl.when(j == n_main_tiles)
    def _():
        dt_ref[...] = acc[:, :LANES]


SSD_PROJ_COLS = 512


def ssd_in_project(x, g, shift, scale, w_pad, *, n_main, seq, n_batch):
    t, d = x.shape
    tm, tn = _proj_row_tile(seq, t - n_batch * seq), SSD_PROJ_COLS
    tiles_per_seq = seq // tm
    n_main_tiles = n_main // tn

    def mod_map(i, j):
        return (_group_of_tile(i, tiles_per_seq, n_batch), 0, 0)

    return pl.pallas_call(
        functools.partial(_ssd_inproj_kernel, n_main_tiles=n_main_tiles),
        grid=(t // tm, n_main_tiles + 1),
        in_specs=[pl.BlockSpec((tm, d), lambda i, j: (i, 0)),
                  pl.BlockSpec((1, d), lambda i, j: (0, 0)),
                  pl.BlockSpec((1, 1, d), mod_map),
                  pl.BlockSpec((1, 1, d), mod_map),
                  pl.BlockSpec((d, tn), lambda i, j: (0, j))],
        out_specs=[pl.BlockSpec((tm, tn), lambda i, j: (i, jnp.minimum(j, n_main_tiles - 1))),
                   pl.BlockSpec((tm, LANES), lambda i, j: (i, 0))],
        out_shape=[jax.ShapeDtypeStruct((t, n_main), BF16), jax.ShapeDtypeStruct((t, LANES), F32)],
        scratch_shapes=[pltpu.VMEM((tm, d), BF16)],
        compiler_params=_cparams("parallel", "arbitrary"),
        name="ssd_in_project",
    )(x, g, shift, scale, w_pad)


CONV_HALO = 8
CONV_ROWS = 256


def _conv_kernel(u_ref, w_ref, b_ref, o_ref, pad_ref, *, seq_len):
    tc = u_ref.shape[1]
    zeros = jnp.zeros((CONV_HALO, tc), F32)
    pad_ref[0:CONV_HALO, :] = zeros
    pad_ref[CONV_HALO + seq_len:2 * CONV_HALO + seq_len, :] = zeros
    for c in range(seq_len // CONV_ROWS):
        r0 = c * CONV_ROWS
        pad_ref[CONV_HALO + r0:CONV_HALO + r0 + CONV_ROWS, :] = u_ref[r0:r0 + CONV_ROWS, :].astype(F32)
    w = w_ref[...]
    b = b_ref[...]
    for c in range(seq_len // CONV_ROWS):
        r0 = c * CONV_ROWS
        acc = b
        for k in range(SSD_CONV):
            off = CONV_HALO + r0 + k - SSD_CONV // 2
            acc = acc + w[k:k + 1, :] * pad_ref[off:off + CONV_ROWS, :]
        o_ref[r0:r0 + CONV_ROWS, :] = (acc * jax.nn.sigmoid(acc)).astype(o_ref.dtype)


def ssd_conv(main, conv_w, conv_b, *, seq_len, n_seqs, row_block0, col0):
    n_ch = conv_w.shape[1]
    tc = 256
    return pl.pallas_call(
        functools.partial(_conv_kernel, seq_len=seq_len),
        grid=(n_seqs, n_ch // tc),
        in_specs=[pl.BlockSpec((seq_len, tc), lambda b, j: (row_block0 + b, col0 // tc + j)),
                  pl.BlockSpec((SSD_CONV, tc), lambda b, j: (0, j)),
                  pl.BlockSpec((1, tc), lambda b, j: (0, j))],
        out_specs=pl.BlockSpec((seq_len, tc), lambda b, j: (b, j)),
        out_shape=jax.ShapeDtypeStruct((n_seqs * seq_len, n_ch), BF16),
        scratch_shapes=[pltpu.VMEM((seq_len + 2 * CONV_HALO, tc), F32)],
        compiler_params=_cparams("parallel", "parallel"),
        name="ssd_conv",
    )(main, conv_w, conv_b.reshape(1, n_ch))


DIR_LANES = 16


def _prep_kernel(raw_ref, bias_ref, alog_ref, acs_o, eacs_o, dtT_o, acsT_o, wT_o, etot_o):
    ch = SSD_CHUNK
    row = lax.broadcasted_iota(I32, (ch, LANES), 0)
    lane = lax.broadcasted_iota(I32, (ch, LANES), 1)
    is_fwd = (lane % DIR_LANES) < DIR_LANES // 2
    neg_a = -jnp.exp(alog_ref[0])
    for c in range(raw_ref.shape[1] // ch):
        rows = slice(c * ch, (c + 1) * ch)
        v = raw_ref[0, rows, :] + bias_ref[0]
        dt = jnp.maximum(v, 0.0) + jnp.log1p(jnp.exp(-jnp.abs(v)))
        a = dt * neg_a
        fwd, rev = a, a
        s = 1
        while s < ch:
            fwd = fwd + jnp.where(row >= s, pltpu.roll(fwd, s, axis=0), 0.0)
            rev = rev + jnp.where(row < ch - s, pltpu.roll(rev, ch - s, axis=0), 0.0)
            s *= 2
        acs = jnp.where(is_fwd, fwd, rev)
        tot = jnp.where(is_fwd[0:1], fwd[ch - 1:ch, :], rev[0:1, :])
        acs_o[0, rows, :] = acs
        eacs_o[0, rows, :] = jnp.exp(acs)
        dtT_o[0, c] = dt.T[:DIR_LANES]
        acsT_o[0, c] = acs.T[:DIR_LANES]
        wT_o[0, c] = (jnp.exp(tot - acs) * dt).T[:DIR_LANES]
        etot_o[0, c] = jnp.broadcast_to(jnp.exp(tot), (8, LANES))


def ssd_prep(dt_g, bias_g, alog_g):
    n_g, t, _ = dt_g.shape
    ch = SSD_CHUNK
    nck_all = t // ch
    per_step = max(k for k in range(1, 9) if nck_all % k == 0)
    nck = nck_all // per_step
    vec = pl.BlockSpec((1, 1, LANES), lambda g, c: (g, 0, 0))
    rows = pl.BlockSpec((1, per_step * ch, LANES), lambda g, c: (g, c, 0))
    tr = pl.BlockSpec((1, per_step, DIR_LANES, LANES), lambda g, c: (g, c, 0, 0))
    return pl.pallas_call(
        _prep_kernel,
        grid=(n_g, nck),
        in_specs=[rows, vec, vec],
        out_specs=[rows, rows, tr, tr, tr, pl.BlockSpec((1, per_step, 8, LANES), lambda g, c: (g, c, 0, 0))],
        out_shape=[jax.ShapeDtypeStruct((n_g, t, LANES), F32), jax.ShapeDtypeStruct((n_g, t, LANES), F32),
                   jax.ShapeDtypeStruct((n_g, nck_all, DIR_LANES, LANES), F32),
                   jax.ShapeDtypeStruct((n_g, nck_all, DIR_LANES, LANES), F32),
                   jax.ShapeDtypeStruct((n_g, nck_all, DIR_LANES, LANES), F32),
                   jax.ShapeDtypeStruct((n_g, nck_all, 8, LANES), F32)],
        compiler_params=_cparams("parallel", "parallel"),
        name="ssd_prep",
    )(dt_g, bias_g, alog_g)


HEADS_PER_GROUP = 8
PAIRS_PER_GROUP = HEADS_PER_GROUP // 2
GROUP_COLS = HEADS_PER_GROUP * SSD_HEAD_DIM


def _scan_kernel(*refs, n_chunks, emit_state, alias_out):
    (x_ref, b_ref, c_ref, z_ref, acs_ref, eacs_ref, dtT_ref, acsT_ref, wT_ref, etot_ref,
     dskip_ref, ng_ref, init_ref) = refs[:13]
    rest = refs[13 + (1 if alias_out else 0):]
    y_ref = rest[0]
    fin_ref = rest[1] if emit_state else None
    ysc, st = rest[-2], rest[-1]
    ch = SSD_CHUNK
    st[...] = init_ref[0, 0]
    lo = lax.broadcasted_iota(I32, (1, LANES), 1) < LANES // 2
    li = lax.broadcasted_iota(I32, (ch, ch), 0)
    si = lax.broadcasted_iota(I32, (ch, ch), 1)
    dn = (((1,), (1,)), ((), ()))
    for d in range(2):
        mask = (si <= li) if d == 0 else (si >= li)

        def chunk(cc, carry, d=d, mask=mask):
            c = cc if d == 0 else n_chunks - 1 - cc
            rows = pl.ds(pl.multiple_of(c * ch, ch), ch)
            bc, cm = b_ref[rows, :], c_ref[rows, :]
            cb = lax.dot_general(cm, bc, dn, preferred_element_type=F32)
            bt = bc.astype(F32).T
            cf = cm.astype(F32)
            acs_blk = acs_ref[0, rows, :]
            eacs_blk = eacs_ref[0, rows, :]
            et = etot_ref[0, c]
            for pr in range(PAIRS_PER_GROUP):
                cols = slice(pr * LANES, (pr + 1) * LANES)
                x2 = x_ref[rows, cols]
                zero = jnp.zeros_like(x2)
                s2 = st[d * PAIRS_PER_GROUP + pr]
                r_lo = d * HEADS_PER_GROUP + 2 * pr
                y2 = jnp.zeros((ch, LANES), F32)
                snew = s2 * jnp.where(lo, et[0:1, r_lo:r_lo + 1], et[0:1, r_lo + 1:r_lo + 2])
                for hh in range(2):
                    r = r_lo + hh
                    keep = lo if hh == 0 else jnp.logical_not(lo)
                    xm = jnp.where(keep, x2, zero)
                    sm = jnp.where(keep, s2, 0.0).astype(BF16)
                    lm = jnp.exp(jnp.where(mask, acs_blk[:, r:r + 1] - acsT_ref[0, c, r:r + 1, :], NEG_BIG))
                    m_h = (cb * lm * dtT_ref[0, c, r:r + 1, :]).astype(BF16)
                    c_h = (cf * eacs_blk[:, r:r + 1]).astype(BF16)
                    y2 = y2 + jnp.dot(m_h, xm, preferred_element_type=F32)
                    y2 = y2 + jnp.dot(c_h, sm, preferred_element_type=F32)
                    b_h = (bt * wT_ref[0, c, r:r + 1, :]).astype(BF16)
                    snew = snew + jnp.dot(b_h, xm, preferred_element_type=F32)
                st[d * PAIRS_PER_GROUP + pr] = snew
                if d == 0:
                    ysc[rows, cols] = y2
                else:
                    ysc[rows, cols] = ysc[rows, cols] + y2
            if d == 1:
                y = ysc[rows, :] + dskip_ref[0] * x_ref[rows, :].astype(F32)
                zf = z_ref[rows, :].astype(F32)
                y = y * (zf * jax.nn.sigmoid(zf))
                y = y * lax.rsqrt(jnp.mean(y * y, axis=-1, keepdims=True) + NORM_EPS) * ng_ref[0]
                y_ref[rows, :] = y.astype(y_ref.dtype)
            return carry

        lax.fori_loop(0, n_chunks, chunk, 0)
    if emit_state:
        fin_ref[0, 0] = st[...]


def ssd_scan(xbc, main, prep, dskip_g, normg_g, init_state, y_prev, *, seq_len, n_batch, row_block0,
             n_rows_total, emit_state):
    acs, eacs, dt_t, acs_t, w_t, etot = prep
    n_g = SSD_GROUPS
    nck = seq_len // SSD_CHUNK
    inner = n_g * GROUP_COLS
    bcol0 = inner // LANES
    ccol0 = bcol0 + n_g * SSD_STATE // LANES
    rb = row_block0

    def tr_spec(h):
        return pl.BlockSpec((1, nck, h, LANES), lambda b, g: (g, rb + b, 0, 0))

    vec = pl.BlockSpec((1, 1, GROUP_COLS), lambda b, g: (g, 0, 0))
    st_spec = pl.BlockSpec((1, 1, 2 * PAIRS_PER_GROUP, SSD_STATE, LANES), lambda b, g: (b, g, 0, 0, 0))
    in_specs = [pl.BlockSpec((seq_len, GROUP_COLS), lambda b, g: (b, g)),
                pl.BlockSpec((seq_len, LANES), lambda b, g: (b, bcol0 + g)),
                pl.BlockSpec((seq_len, LANES), lambda b, g: (b, ccol0 + g)),
                pl.BlockSpec((seq_len, GROUP_COLS), lambda b, g: (rb + b, g)),
                pl.BlockSpec((1, seq_len, LANES), lambda b, g: (g, rb + b, 0)),
                pl.BlockSpec((1, seq_len, LANES), lambda b, g: (g, rb + b, 0)),
                tr_spec(DIR_LANES), tr_spec(DIR_LANES), tr_spec(DIR_LANES), tr_spec(8),
                vec, vec, st_spec]
    args = [xbc, xbc, xbc, main, acs, eacs, dt_t, acs_t, w_t, etot, dskip_g, normg_g, init_state]
    aliases = {}
    if y_prev is not None:
        in_specs.append(pl.BlockSpec(memory_space=pl.ANY))
        args.append(y_prev)
        aliases = {len(args) - 1: 0}
    out_specs = [pl.BlockSpec((seq_len, GROUP_COLS), lambda b, g: (rb + b, g))]
    out_shape = [jax.ShapeDtypeStruct((n_rows_total, inner), BF16)]
    if emit_state:
        out_specs.append(st_spec)
        out_shape.append(jax.ShapeDtypeStruct(init_state.shape, F32))
    return pl.pallas_call(
        functools.partial(_scan_kernel, n_chunks=nck, emit_state=emit_state, alias_out=y_prev is not None),
        grid=(n_batch, n_g),
        in_specs=in_specs,
        out_specs=out_specs,
        out_shape=out_shape,
        scratch_shapes=[pltpu.VMEM((seq_len, GROUP_COLS), F32),
                        pltpu.VMEM((2 * PAIRS_PER_GROUP, SSD_STATE, LANES), F32)],
        input_output_aliases=aliases,
        compiler_params=_cparams("parallel", "parallel"),
        name="ssd_scan",
    )(*args)


def _router_kernel(x_ref, g_ref, sh_ref, sc_ref, whi_ref, wlo_ref, b_ref, tri_ref,
                   h_o, meta_o, gate_o, cnt_o, carry):
    i = pl.program_id(0)

    @pl.when(i == 0)
    def _():
        carry[...] = jnp.zeros_like(carry)

    h = _norm_modulate(x_ref, g_ref, sh_ref, sc_ref)
    _rows_to_slabs(h_o, h)
    h_hi = h.astype(BF16)
    h_lo = (h - h_hi.astype(F32)).astype(BF16)
    whi = whi_ref[...]
    logits = (jnp.dot(h_hi, whi, preferred_element_type=F32) + jnp.dot(h_lo, whi, preferred_element_type=F32)
              + jnp.dot(h_hi, wlo_ref[...], preferred_element_type=F32) + b_ref[...])
    tm = logits.shape[0]
    lane = lax.broadcasted_iota(I32, (tm, LANES), 1).astype(F32)
    work = logits
    vals, idxs = [], []
    for _ in range(TOP_K):
        m = jnp.max(work, axis=1, keepdims=True)
        idx = jnp.min(jnp.where(work == m, lane, float(LANES)), axis=1, keepdims=True)
        vals.append(m)
        idxs.append(idx)
        work = jnp.where(lane == idx, -jnp.inf, work)
    es = [jnp.exp(v - vals[0]) for v in vals]
    inv = 1.0 / (es[0] + es[1] + es[2] + es[3])
    multihot = sum(jnp.where(lane == idx, 1.0, 0.0) for idx in idxs)
    before = jnp.dot(tri_ref[...], multihot.astype(BF16), preferred_element_type=F32) + carry[...]
    meta = jnp.zeros((tm, LANES), F32)
    gates = jnp.zeros((tm, LANES), F32)
    for k in range(TOP_K):
        rank = jnp.sum(jnp.where(lane == idxs[k], before, 0.0), axis=1, keepdims=True)
        meta = meta + jnp.where(lane == float(k), idxs[k], 0.0) + jnp.where(lane == float(TOP_K + k), rank, 0.0)
        gates = gates + jnp.where(lane == float(k), es[k] * inv, 0.0)
    meta_o[...] = meta.astype(I32)
    gate_o[...] = gates
    carry[...] = carry[...] + jnp.sum(multihot, axis=0, keepdims=True)
    cnt_o[...] = carry[...]


def moe_router(x, g, shift, scale, w_hi, w_lo, b_pad, *, n_rows, seq, n_batch):
    d = x.shape[1]
    tm = ROW_TILE
    tiles_per_seq = seq // tm
    tri = jnp.asarray(np.tril(np.ones((tm, tm), np.float32), -1), dtype=BF16)

    def mod_map(i):
        return (_group_of_tile(i, tiles_per_seq, n_batch), 0, 0)

    full = lambda shape: pl.BlockSpec(shape, lambda i: (0,) * len(shape))
    return pl.pallas_call(
        _router_kernel,
        grid=(n_rows // tm,),
        in_specs=[pl.BlockSpec((tm, d), lambda i: (i, 0)), full((1, d)),
                  pl.BlockSpec((1, 1, d), mod_map), pl.BlockSpec((1, 1, d), mod_map),
                  full((d, LANES)), full((d, LANES)), full((1, LANES)), full((tm, tm))],
        out_specs=[pl.BlockSpec((tm, ROW_SLABS, LANES), lambda i: (i, 0, 0)),
                   pl.BlockSpec((tm, LANES), lambda i: (i, 0)),
                   pl.BlockSpec((tm, LANES), lambda i: (i, 0)), full((1, LANES))],
        out_shape=[jax.ShapeDtypeStruct((n_rows, ROW_SLABS, LANES), F32),
                   jax.ShapeDtypeStruct((n_rows, LANES), I32),
                   jax.ShapeDtypeStruct((n_rows, LANES), F32), jax.ShapeDtypeStruct((1, LANES), F32)],
        scratch_shapes=[pltpu.VMEM((1, LANES), F32)],
        compiler_params=_cparams("arbitrary"),
        name="moe_router",
    )(x, g, shift, scale, w_hi, w_lo, b_pad, tri)


ROW_SLABS = 16
DMA_UNROLL = 8
DISPATCH_TILE = 256
FF_CHUNK = 256
OUT_CHUNK = 512


def _rows_to_slabs(ref, val):
    for c in range(ROW_SLABS):
        ref[:, c, :] = val[:, c * LANES:(c + 1) * LANES]


def _slabs_to_rows(ref, rows):
    return jnp.concatenate([ref[rows, c, :] for c in range(ROW_SLABS)], axis=1)


def _wait_rows(n, like_hbm, sem):
    pltpu.make_async_copy(like_hbm.at[pl.ds(0, n)], like_hbm.at[pl.ds(0, n)], sem).wait()


def _dispatch_kernel(tail_start_ref, tail_len_ref, nact_ref, dst_ref, h_ref, xs_hbm, zbuf, sem, zsem,
                     *, n_exp, n_blocks):
    s = pl.program_id(0)
    tm = h_ref.shape[0]

    @pl.when(s == 0)
    def _():
        zbuf[...] = jnp.zeros_like(zbuf)
        half = MOE_BLOCK // 2

        def unused(b, wait):
            for part in range(2):
                cp = pltpu.make_async_copy(zbuf, xs_hbm.at[pl.ds(b * MOE_BLOCK + part * half, half)], zsem)
                cp.wait() if wait else cp.start()

        lax.fori_loop(nact_ref[0], n_blocks, lambda b, c: (unused(b, False), c)[1], 0)
        lax.fori_loop(nact_ref[0], n_blocks, lambda b, c: (unused(b, True), c)[1], 0)
        sizes = [1 << b for b in range(MOE_BLOCK.bit_length() - 1)]

        def tails(e, wait):
            start, length = tail_start_ref[e], tail_len_ref[e]
            for size in sizes:
                @pl.when((length & size) != 0)
                def _(size=size):
                    pos = start + (length & ~(2 * size - 1))
                    cp = pltpu.make_async_copy(zbuf.at[pl.ds(0, size)], xs_hbm.at[pl.ds(pos, size)], zsem)
                    cp.wait() if wait else cp.start()

        lax.fori_loop(0, n_exp, lambda e, c: (tails(e, False), c)[1], 0)
        lax.fori_loop(0, n_exp, lambda e, c: (tails(e, True), c)[1], 0)

    def issue(it, c):
        for u in range(DMA_UNROLL):
            tok = it * (DMA_UNROLL // TOP_K) + u // TOP_K
            pltpu.make_async_copy(h_ref.at[pl.ds(tok, 1)], xs_hbm.at[pl.ds(dst_ref[0, 0, it * DMA_UNROLL + u], 1)],
                                  sem).start(priority=u % 2)
        return c

    lax.fori_loop(0, TOP_K * tm // DMA_UNROLL, issue, 0)
    _wait_rows(TOP_K * tm, xs_hbm, sem)


def moe_dispatch(tail_start, tail_len, n_active, dest, h3, *, n_blocks):
    n_rows = h3.shape[0]
    tm = DISPATCH_TILE
    n_exp = tail_start.shape[0]
    grid_spec = pltpu.PrefetchScalarGridSpec(
        num_scalar_prefetch=3,
        grid=(n_rows // tm,),
        in_specs=[pl.BlockSpec((1, 1, TOP_K * tm), lambda s, a, b, c: (s, 0, 0), memory_space=pltpu.SMEM),
                  pl.BlockSpec((tm, ROW_SLABS, LANES), lambda s, a, b, c: (s, 0, 0))],
        out_specs=pl.BlockSpec(memory_space=pl.ANY),
        scratch_shapes=[pltpu.VMEM((MOE_BLOCK // 2, ROW_SLABS, LANES), F32),
                        pltpu.SemaphoreType.DMA(()), pltpu.SemaphoreType.DMA(())],
    )
    return pl.pallas_call(
        functools.partial(_dispatch_kernel, n_exp=n_exp, n_blocks=n_blocks),
        grid_spec=grid_spec,
        out_shape=jax.ShapeDtypeStruct((n_blocks * MOE_BLOCK, ROW_SLABS, LANES), F32),
        compiler_params=_cparams("arbitrary"),
        name="moe_dispatch",
    )(tail_start, tail_len, n_active, dest.reshape(n_rows // tm, 1, TOP_K * tm), h3)


def _ffn_kernel(bexp_ref, nact_ref, xs_hbm, wgu_ref, bgu_ref, wd_ref, bd_ref, o_hbm, xbuf, ybuf, sem, osem):
    s = pl.program_id(0)
    n_steps = pl.num_programs(0)
    slot = s % 2
    nact = nact_ref[0]

    def block_copies(blk, sl):
        rows = pl.ds(blk * MOE_BLOCK, MOE_BLOCK)
        return [pltpu.make_async_copy(xs_hbm.at[rows, c, :], xbuf.at[sl, :, c * LANES:(c + 1) * LANES], sem.at[sl])
                for c in range(ROW_SLABS)]

    def out_copies(blk, sl):
        rows = pl.ds(blk * MOE_BLOCK, MOE_BLOCK)
        return [pltpu.make_async_copy(ybuf.at[sl, :, c * LANES:(c + 1) * LANES], o_hbm.at[rows, c, :], osem.at[sl])
                for c in range(ROW_SLABS)]

    @pl.when(jnp.logical_and(s == 0, nact > 0))
    def _():
        for cp in block_copies(0, 0):
            cp.start()

    @pl.when(s + 1 < nact)
    def _():
        for cp in block_copies(s + 1, 1 - slot):
            cp.start()

    @pl.when(s < nact)
    def _():
        for cp in block_copies(s, slot):
            cp.wait()
        x = xbuf[slot].astype(BF16)
        ff = EXPERT_FF

        def swiglu(glu, lin):
            glu = jnp.minimum(glu, SWIGLU_LIMIT)
            lin = jnp.clip(lin, -SWIGLU_LIMIT, SWIGLU_LIMIT)
            return (glu * jax.nn.sigmoid(SWIGLU_ALPHA * glu) * (lin + 1.0)).astype(BF16)

        acts = []
        full = ff // FF_CHUNK * FF_CHUNK
        for lo in range(0, full, FF_CHUNK):
            hi = lo + FF_CHUNK
            glu = jnp.dot(x, wgu_ref[0, 0, :, lo:hi], preferred_element_type=F32) + bgu_ref[0, 0, :, lo:hi]
            lin = (jnp.dot(x, wgu_ref[0, 0, :, ff + lo:ff + hi], preferred_element_type=F32)
                   + bgu_ref[0, 0, :, ff + lo:ff + hi])
            acts.append(swiglu(glu, lin))
        if full < ff:
            rem = ff - full
            w_rem = jnp.concatenate([wgu_ref[0, 0, :, full:ff], wgu_ref[0, 0, :, ff + full:2 * ff]], axis=1)
            b_rem = jnp.concatenate([bgu_ref[0, 0, :, full:ff], bgu_ref[0, 0, :, ff + full:2 * ff]], axis=1)
            gl = jnp.dot(x, w_rem, preferred_element_type=F32) + b_rem
            acts.append(swiglu(gl[:, :rem], gl[:, rem:]))
        act = jnp.concatenate(acts, axis=1)
        for n0 in range(0, ybuf.shape[2], OUT_CHUNK):
            y = jnp.dot(act, wd_ref[0, 0, :, n0:n0 + OUT_CHUNK], preferred_element_type=F32)
            ybuf[slot, :, n0:n0 + OUT_CHUNK] = y + bd_ref[0, 0, :, n0:n0 + OUT_CHUNK]

    @pl.when(s >= nact)
    def _():
        ybuf[slot] = jnp.zeros(ybuf.shape[1:], F32)

    @pl.when(s >= 1)
    def _():
        for cp in out_copies(s - 1, 1 - slot):
            cp.wait()

    for cp in out_copies(s, slot):
        cp.start()

    @pl.when(s == n_steps - 1)
    def _():
        for cp in out_copies(s, slot):
            cp.wait()


def moe_ffn_blocks(block_exp, n_active, xs, w_gu, b_gu, w_down, b_down, *, layer):
    n_blocks = xs.shape[0] // MOE_BLOCK
    d, ff2 = w_gu.shape[2], w_gu.shape[3]
    wmap = lambda s, be, na: (layer, be[s], 0, 0)
    grid_spec = pltpu.PrefetchScalarGridSpec(
        num_scalar_prefetch=2,
        grid=(n_blocks,),
        in_specs=[pl.BlockSpec(memory_space=pl.ANY),
                  pl.BlockSpec((1, 1, d, ff2), wmap), pl.BlockSpec((1, 1, 1, ff2), wmap),
                  pl.BlockSpec((1, 1, ff2 // 2, d), wmap), pl.BlockSpec((1, 1, 1, d), wmap)],
        out_specs=pl.BlockSpec(memory_space=pl.ANY),
        scratch_shapes=[pltpu.VMEM((2, MOE_BLOCK, d), F32), pltpu.VMEM((2, MOE_BLOCK, d), F32),
                        pltpu.SemaphoreType.DMA((2,)), pltpu.SemaphoreType.DMA((2,))],
    )
    return pl.pallas_call(
        _ffn_kernel,
        grid_spec=grid_spec,
        out_shape=jax.ShapeDtypeStruct(xs.shape, F32),
        compiler_params=_cparams("arbitrary"),
        name="moe_ffn",
    )(block_exp, n_active, xs, w_gu, b_gu, w_down, b_down)


COMBINE_TILE = 128


def _combine_kernel(d0_ref, d1_ref, g_ref, rows_hbm, x_hbm, mod_ref, o_hbm, gbuf, xbuf, obuf, gsem, xsem, osem,
                    *, n_tiles):
    s = pl.program_id(0)
    slot = s % 2
    tm = COMBINE_TILE

    def gather(idx_ref, sl):
        def issue(it, c):
            for u in range(DMA_UNROLL):
                r = it * DMA_UNROLL + u
                pltpu.make_async_copy(rows_hbm.at[pl.ds(idx_ref[0, 0, r], 1)], gbuf.at[sl, pl.ds(r, 1)],
                                      gsem.at[sl]).start(priority=u % 2)
            return c

        lax.fori_loop(0, TOP_K * tm // DMA_UNROLL, issue, 0)

    def x_copies(tile, sl):
        rows = pl.ds(tile * tm, tm)
        return [pltpu.make_async_copy(x_hbm.at[rows, c * LANES:(c + 1) * LANES], xbuf.at[sl, :, c, :], xsem.at[sl])
                for c in range(ROW_SLABS)]

    def out_copies(tile, sl):
        rows = pl.ds(tile * tm, tm)
        return [pltpu.make_async_copy(obuf.at[sl, :, c, :], o_hbm.at[rows, c * LANES:(c + 1) * LANES], osem.at[sl])
                for c in range(ROW_SLABS)]

    @pl.when(s == 0)
    def _():
        gather(d0_ref, 0)
        for cp in x_copies(0, 0):
            cp.start()

    @pl.when(s + 1 < n_tiles)
    def _():
        gather(d1_ref, 1 - slot)
        for cp in x_copies(s + 1, 1 - slot):
            cp.start()

    _wait_rows(TOP_K * tm, rows_hbm, gsem.at[slot])
    for cp in x_copies(s, slot):
        cp.wait()
    mod = mod_ref[0]

    def token(t, c):
        acc = g_ref[0, 0, TOP_K * t] * gbuf[slot, t]
        for k in range(1, TOP_K):
            acc = acc + g_ref[0, 0, TOP_K * t + k] * gbuf[slot, k * tm + t]
        obuf[slot, t] = xbuf[slot, t] + mod * acc
        return c

    lax.fori_loop(0, tm, token, 0, unroll=4)

    @pl.when(s >= 1)
    def _():
        for cp in out_copies(s - 1, 1 - slot):
            cp.wait()

    for cp in out_copies(s, slot):
        cp.start()

    @pl.when(s == n_tiles - 1)
    def _():
        for cp in out_copies(s, slot):
            cp.wait()


def moe_combine(dest_tiles, out_rows, x, gates, gate_mod, *, n_rows, seq, n_batch):
    d = x.shape[1]
    tm = COMBINE_TILE
    n_tiles = n_rows // tm
    tiles_per_seq = seq // tm
    gate_tiles = gates[:n_rows, :TOP_K].reshape(n_tiles, 1, TOP_K * tm)
    mod_slabs = gate_mod.reshape(gate_mod.shape[0], ROW_SLABS, LANES)

    def smem_spec(ahead):
        return pl.BlockSpec((1, 1, TOP_K * tm), lambda s: (jnp.minimum(s + ahead, n_tiles - 1), 0, 0),
                            memory_space=pltpu.SMEM)

    slab_buf = pltpu.VMEM((2, tm, ROW_SLABS, LANES), F32)
    return pl.pallas_call(
        functools.partial(_combine_kernel, n_tiles=n_tiles),
        grid=(n_tiles,),
        in_specs=[smem_spec(0), smem_spec(1), smem_spec(0),
                  pl.BlockSpec(memory_space=pl.ANY), pl.BlockSpec(memory_space=pl.ANY),
                  pl.BlockSpec((1, ROW_SLABS, LANES), lambda s: (_group_of_tile(s, tiles_per_seq, n_batch), 0, 0))],
        out_specs=pl.BlockSpec(memory_space=pl.ANY),
        out_shape=jax.ShapeDtypeStruct((n_rows, d), F32),
        scratch_shapes=[pltpu.VMEM((2, TOP_K * tm, ROW_SLABS, LANES), F32), slab_buf, slab_buf,
                        pltpu.SemaphoreType.DMA((2,)), pltpu.SemaphoreType.DMA((2,)), pltpu.SemaphoreType.DMA((2,))],
        compiler_params=_cparams("arbitrary"),
        name="moe_combine",
    )(dest_tiles, dest_tiles, gate_tiles, out_rows, x, mod_slabs)


def moe_layer(x, g, shift, scale, gate_mod, w_router, b_router, w_gu, b_gu, w_down, b_down,
              *, layer, n_rows, seq, n_batch):
    n_exp = w_router.shape[1]
    w_pad = jnp.pad(w_router, ((0, 0), (0, LANES - n_exp)))
    w_hi = w_pad.astype(BF16)
    w_lo = (w_pad - w_hi.astype(F32)).astype(BF16)
    b_pad = jnp.concatenate([b_router, jnp.full((LANES - n_exp,), NEG_BIG, F32)])[None, :]
    h3, meta, gates, cnt = moe_router(x, g, shift, scale, w_hi, w_lo, b_pad, n_rows=n_rows, seq=seq, n_batch=n_batch)
    top_idx, rank = meta[:, :TOP_K], meta[:, TOP_K:2 * TOP_K]
    counts = cnt[0, :n_exp].astype(I32)
    padded = (counts + MOE_BLOCK - 1) // MOE_BLOCK * MOE_BLOCK
    pad_end = jnp.cumsum(padded)
    pad_start = pad_end - padded
    onehot = top_idx[:, :, None] == jnp.arange(n_exp, dtype=I32)[None, None, :]
    dest = jnp.sum(jnp.where(onehot, pad_start[None, None, :], 0), axis=-1) + rank
    n_blocks = n_rows * TOP_K // MOE_BLOCK + n_exp
    block_row0 = jnp.arange(n_blocks, dtype=I32) * MOE_BLOCK
    block_exp = jnp.minimum(jnp.sum((pad_end[None, :] <= block_row0[:, None]).astype(I32), axis=1), n_exp - 1)
    n_active = (pad_end[-1:] // MOE_BLOCK).astype(I32)
    xs = moe_dispatch(pad_start + counts, padded - counts, n_active, dest, h3, n_blocks=n_blocks)
    out_rows = moe_ffn_blocks(block_exp, n_active, xs, w_gu, b_gu, w_down, b_down, layer=layer)
    tm = COMBINE_TILE
    dest_tiles = dest.reshape(n_rows // tm, tm, TOP_K).transpose(0, 2, 1).reshape(n_rows // tm, 1, TOP_K * tm)
    return moe_combine(dest_tiles, out_rows, x, gates, gate_mod, n_rows=n_rows, seq=seq, n_batch=n_batch)


def attention_layer(x, g, shift, scale, gate_mod, w_in, q_gain, k_gain, w_out, sink, *, n_heads, n_kv, head_dim,
                    seq, n_ctx, n_batch, need_ctx):
    n_qk = (n_heads + n_kv) * head_dim
    q_scale = head_dim ** -0.5 * (math.log2(math.e) if sink is None else 1.0)
    gain_row = jnp.concatenate([jnp.tile(q_gain, n_heads) * q_scale, jnp.tile(k_gain, n_kv),
                                jnp.ones((n_kv * head_dim,), F32)])[None, :]
    qkv = qkv_project(x, g, shift, scale, w_in.astype(BF16), gain_row, head_group_matrix(256, head_dim),
                      rope_tables(seq, head_dim, _proj_row_tile(seq, x.shape[0] - n_batch * seq)),
                      n_qk_cols=n_qk, head_dim=head_dim, seq=seq, n_batch=n_batch)
    if sink is not None:
        att = swa_attention(qkv, sink, n_batch=n_batch, seq=seq, n_ctx=n_ctx, with_ctx_queries=need_ctx)
    else:
        att = global_attention(qkv, n_batch=n_batch, seq=seq, n_ctx=n_ctx, with_ctx_queries=need_ctx)
    n_rows = n_batch * (seq + (n_ctx if need_ctx else 0))
    return out_project(att, w_out.astype(BF16), x, gate_mod, n_rows=n_rows, seq=seq, n_batch=n_batch)


def ssd_layer(x, g, shift, scale, gate_mod, w_in, conv_w, conv_b, dt_bias, a_log, d_skip, norm_g, w_out,
              *, seq, n_ctx, n_batch):
    t, d = x.shape
    n_g = SSD_GROUPS
    n_heads = d_skip.shape[0]
    inner = n_heads * SSD_HEAD_DIM
    n_main = inner + conv_w.shape[1]
    w_pad = jnp.pad(w_in, ((0, 0), (0, n_main + SSD_PROJ_COLS - w_in.shape[1]))).astype(BF16)
    main, dt_raw = ssd_in_project(x, g, shift, scale, w_pad, n_main=n_main, seq=seq, n_batch=n_batch)

    def to_groups(a):
        lead = a.shape[:-1]
        a = a.reshape(lead + (2, n_g, HEADS_PER_GROUP))
        a = jnp.moveaxis(a, -2, 0).reshape((n_g,) + lead + (DIR_LANES,))
        return jnp.pad(a, [(0, 0)] * (a.ndim - 1) + [(0, LANES - DIR_LANES)])

    prep = ssd_prep(to_groups(dt_raw[:, :2 * n_heads]), to_groups(dt_bias.reshape(1, -1)),
                    to_groups(a_log.reshape(1, -1)))
    dskip_g = jnp.repeat(d_skip, SSD_HEAD_DIM).reshape(n_g, 1, GROUP_COLS)
    normg_g = norm_g.reshape(n_g, 1, GROUP_COLS)
    n_lat = n_batch * seq
    xbc_c = ssd_conv(main, conv_w, conv_b, seq_len=n_ctx, n_seqs=n_batch, row_block0=n_lat // n_ctx, col0=inner)
    xbc_l = ssd_conv(main, conv_w, conv_b, seq_len=seq, n_seqs=n_batch, row_block0=0, col0=inner)
    zero_state = jnp.zeros((n_batch, n_g, 2 * PAIRS_PER_GROUP, SSD_STATE, LANES), F32)
    y = jnp.zeros((t, inner), BF16)
    y, ctx_state = ssd_scan(xbc_c, main, prep, dskip_g, normg_g, zero_state, y, seq_len=n_ctx, n_batch=n_batch,
                            row_block0=n_lat // n_ctx, n_rows_total=t, emit_state=True)
    (y,) = ssd_scan(xbc_l, main, prep, dskip_g, normg_g, ctx_state, y, seq_len=seq, n_batch=n_batch,
                    row_block0=0, n_rows_total=t, emit_state=False)
    return out_project(y, w_out.astype(BF16), x, gate_mod, n_rows=t, seq=seq, n_batch=n_batch)


def kernel(x, c, ctx, c_ctx, ada_w, ada_b, norm_mix, norm_ffn, swa_w_in, swa_q_norm, swa_k_norm, swa_sink,
           swa_w_out, ssd_w_in, ssd_conv_w, ssd_conv_b, ssd_dt_bias, ssd_a_log, ssd_d, ssd_norm, ssd_w_out,
           ga_w_in, ga_q_norm, ga_k_norm, ga_w_out, moe_w_router, moe_b_router, moe_w_gate_up, moe_b_gate_up,
           moe_w_down, moe_b_down):
    n_batch, seq, d = x.shape
    n_ctx = ctx.shape[1]
    n_lat = n_batch * seq
    xs = jnp.concatenate([x.reshape(n_lat, d), ctx.reshape(n_batch * n_ctx, d)], axis=0)
    cvec = jnp.concatenate([c, c_ctx[None, :], jnp.zeros((8 - n_batch - 1, d), F32)], axis=0)
    mods = adaln_all(cvec, ada_w, ada_b)[:, :n_batch + 1].reshape(DEPTH, n_batch + 1, 6, 1, d)
    dims = dict(seq=seq, n_ctx=n_ctx, n_batch=n_batch)
    w_gu_bf, w_down_bf = moe_w_gate_up.astype(BF16), moe_w_down.astype(BF16)
    for i in range(DEPTH):
        kind, j = i % N_MIXERS, i // N_MIXERS
        need_ctx = i < DEPTH - 1
        m = [mods[i, :, k] for k in range(6)]
        g_mix = norm_mix[i][None, :]
        if kind == 0:
            xs = attention_layer(xs, g_mix, m[0], m[1], m[2], swa_w_in[j], swa_q_norm[j], swa_k_norm[j],
                                 swa_w_out[j], swa_sink[j], n_heads=SWA_HEADS, n_kv=SWA_KV_HEADS,
                                 head_dim=SWA_HEAD_DIM, need_ctx=need_ctx, **dims)
        elif kind == 1:
            xs = ssd_layer(xs, g_mix, m[0], m[1], m[2], ssd_w_in[j], ssd_conv_w[j], ssd_conv_b[j], ssd_dt_bias[j],
                           ssd_a_log[j], ssd_d[j], ssd_norm[j], ssd_w_out[j], **dims)
        else:
            xs = attention_layer(xs, g_mix, m[0], m[1], m[2], ga_w_in[j], ga_q_norm[j], ga_k_norm[j], ga_w_out[j],
                                 None, n_heads=GA_HEADS, n_kv=GA_KV_HEADS, head_dim=GA_HEAD_DIM,
                                 need_ctx=need_ctx, **dims)
        n_rows = n_batch * (seq + (n_ctx if need_ctx else 0))
        xs = moe_layer(xs, norm_ffn[i][None, :], m[3], m[4], m[5], moe_w_router[i], moe_b_router[i],
                       w_gu_bf, moe_b_gate_up[:, :, None, :], w_down_bf, moe_b_down[:, :, None, :],
                       layer=i, n_rows=n_rows, seq=seq, n_batch=n_batch)
    return xs[:n_lat].reshape(n_batch, seq, d)
```

```python
import functools
import math

import jax
import jax.numpy as jnp
import numpy as np
from jax import lax
from jax.experimental import pallas as pl
from jax.experimental.pallas import tpu as pltpu

F32 = jnp.float32
BF16 = jnp.bfloat16
I32 = jnp.int32

DEPTH = 4
N_MIXERS = 3
GRID_W = 64
NORM_EPS = 1e-6
ROPE_THETA = 10000.0

SWA_HEADS, SWA_KV_HEADS, SWA_HEAD_DIM, WINDOW = 32, 4, 64, 128
SSD_HEAD_DIM, SSD_GROUPS, SSD_STATE, SSD_CONV, SSD_CHUNK = 64, 8, 128, 5, 128
GA_HEADS, GA_KV_HEADS, GA_HEAD_DIM = 16, 4, 128
N_EXPERTS, TOP_K, EXPERT_FF = 32, 4, 896
SWIGLU_ALPHA, SWIGLU_LIMIT = 1.702, 7.0
MOE_BLOCK = 256

LANES = 128
ROW_TILE = 512
NEG_BIG = -1e30
LOG2_E = math.log2(math.e)
VMEM_LIMIT = 56 * 1024 * 1024


def _cparams(*sem):
    return pltpu.CompilerParams(dimension_semantics=sem, vmem_limit_bytes=VMEM_LIMIT)


def _adaln_kernel(c_ref, w_ref, b_ref, o_ref):
    a = c_ref[...]
    a = (a * jax.nn.sigmoid(a)).astype(BF16)
    o_ref[0] = jnp.dot(a, w_ref[0].astype(BF16), preferred_element_type=F32) + b_ref[0]


def adaln_all(cvec, ada_w, ada_b):
    n_layers, d, n = ada_w.shape
    tn = 1024
    return pl.pallas_call(
        _adaln_kernel,
        grid=(n_layers, n // tn),
        in_specs=[pl.BlockSpec((8, d), lambda l, j: (0, 0)),
                  pl.BlockSpec((1, d, tn), lambda l, j: (l, 0, j)),
                  pl.BlockSpec((1, 1, tn), lambda l, j: (l, 0, j))],
        out_specs=pl.BlockSpec((1, 8, tn), lambda l, j: (l, 0, j)),
        out_shape=jax.ShapeDtypeStruct((n_layers, 8, n), F32),
        compiler_params=_cparams("parallel", "parallel"),
        name="adaln",
    )(cvec, ada_w, ada_b.reshape(n_layers, 1, n))


def _norm_modulate(x_ref, g_ref, sh_ref, sc_ref):
    x = x_ref[...]
    ms = jnp.mean(x * x, axis=-1, keepdims=True)
    y = x * lax.rsqrt(ms + NORM_EPS) * g_ref[...]
    return y * (1.0 + sc_ref[0]) + sh_ref[0]


def _group_of_tile(i, tiles_per_seq, n_batch):
    return jnp.minimum(i // tiles_per_seq, n_batch)


def _qkv_kernel(x_ref, g_ref, sh_ref, sc_ref, w_ref, gain_ref, gmat_ref, cos_ref, sa_ref, sb_ref,
                o_ref, h_ref, *, n_qk_tiles, head_dim):
    j = pl.program_id(1)

    @pl.when(j == 0)
    def _():
        h_ref[...] = _norm_modulate(x_ref, g_ref, sh_ref, sc_ref).astype(BF16)

    sub = gmat_ref.shape[0]
    q4 = head_dim // 4
    for t in range(w_ref.shape[1] // sub):
        cols = slice(t * sub, (t + 1) * sub)
        acc = jnp.dot(h_ref[...], w_ref[:, cols], preferred_element_type=F32)
        sub_tile = j * (w_ref.shape[1] // sub) + t

        @pl.when(sub_tile < n_qk_tiles)
        def _(acc=acc, cols=cols):
            ss = jnp.dot((acc * acc).astype(BF16), gmat_ref[...], preferred_element_type=F32)
            yn = acc * lax.rsqrt(ss * (1.0 / head_dim) + NORM_EPS) * gain_ref[:, cols]
            cos, sa, sb = cos_ref[...], sa_ref[...], sb_ref[...]
            for c in range(sub // LANES):
                yc = yn[:, c * LANES:(c + 1) * LANES]
                up = pltpu.roll(yc, LANES - q4, axis=1)
                dn = pltpu.roll(yc, q4, axis=1)
                lanes = slice(cols.start + c * LANES, cols.start + (c + 1) * LANES)
                o_ref[:, lanes] = (yc * cos + up * sa + dn * sb).astype(o_ref.dtype)

        @pl.when(sub_tile >= n_qk_tiles)
        def _(acc=acc, cols=cols):
            o_ref[:, cols] = acc.astype(o_ref.dtype)


def _proj_row_tile(seq, n_ctx_rows):
    return next(tm for tm in (1024, 512) if seq % tm == 0 and n_ctx_rows % tm == 0)


def qkv_project(x, g, shift, scale, w, gain_row, gmat, rope_tabs, *, n_qk_cols, head_dim, seq, n_batch):
    t, d = x.shape
    n = w.shape[1]
    tm, tn = _proj_row_tile(seq, t - n_batch * seq), 512
    sub = gmat.shape[0]
    tiles_per_seq = seq // tm
    n_lat_tiles = n_batch * tiles_per_seq

    def tab_map(i, j):
        return (jnp.where(i < n_lat_tiles, i % tiles_per_seq, tiles_per_seq), 0)

    def mod_map(i, j):
        return (_group_of_tile(i, tiles_per_seq, n_batch), 0, 0)

    tab_spec = pl.BlockSpec((tm, LANES), tab_map)
    return pl.pallas_call(
        functools.partial(_qkv_kernel, n_qk_tiles=n_qk_cols // sub, head_dim=head_dim),
        grid=(t // tm, n // tn),
        in_specs=[pl.BlockSpec((tm, d), lambda i, j: (i, 0)),
                  pl.BlockSpec((1, d), lambda i, j: (0, 0)),
                  pl.BlockSpec((1, 1, d), mod_map),
                  pl.BlockSpec((1, 1, d), mod_map),
                  pl.BlockSpec((d, tn), lambda i, j: (0, j)),
                  pl.BlockSpec((1, tn), lambda i, j: (0, j)),
                  pl.BlockSpec((sub, sub), lambda i, j: (0, 0)),
                  tab_spec, tab_spec, tab_spec],
        out_specs=pl.BlockSpec((tm, tn), lambda i, j: (i, j)),
        out_shape=jax.ShapeDtypeStruct((t, n), BF16),
        scratch_shapes=[pltpu.VMEM((tm, d), BF16)],
        compiler_params=_cparams("parallel", "arbitrary"),
        name="qkv_project",
    )(x, g, shift, scale, w, gain_row, gmat, *rope_tabs)


def rope_tables(seq, head_dim, tile):
    q4 = head_dim // 4
    pos = np.arange(seq)
    inv = ROPE_THETA ** (-np.arange(q4, dtype=np.float64) / q4)
    lane = np.arange(LANES) % head_dim
    axis, half, idx = lane // (2 * q4), (lane // q4) % 2, lane % q4
    p = np.where(axis[None, :] == 0, (pos // GRID_W)[:, None], (pos % GRID_W)[:, None]).astype(np.float64)
    ang = (p.astype(np.float32) * inv.astype(np.float32)[idx][None, :]).astype(np.float32)
    cos, sin = np.cos(ang), np.sin(ang)
    sa = np.where(half[None, :] == 0, -sin, 0.0)
    sb = np.where(half[None, :] == 1, sin, 0.0)
    pad = np.zeros((tile, LANES), np.float32)
    return tuple(jnp.asarray(np.concatenate([a.astype(np.float32), b], axis=0))
                 for a, b in ((cos, pad + 1.0), (sa, pad), (sb, pad)))


def head_group_matrix(width, head_dim):
    r = np.arange(width) // head_dim
    return jnp.asarray((r[:, None] == r[None, :]).astype(np.float32), dtype=BF16)


def _oproj_kernel(a_ref, w_ref, x_ref, gate_ref, o_ref):
    acc = jnp.dot(a_ref[...], w_ref[...], preferred_element_type=F32)
    o_ref[...] = x_ref[...] + gate_ref[0] * acc


def out_project(a, w, x, gate, *, n_rows, seq, n_batch):
    k = a.shape[1]
    d = w.shape[1]
    tm, tn = _proj_row_tile(seq, a.shape[0] - n_batch * seq if a.shape[0] > n_batch * seq else seq), 512
    tiles_per_seq = seq // tm
    return pl.pallas_call(
        _oproj_kernel,
        grid=(n_rows // tm, d // tn),
        in_specs=[pl.BlockSpec((tm, k), lambda i, j: (i, 0)),
                  pl.BlockSpec((k, tn), lambda i, j: (0, j)),
                  pl.BlockSpec((tm, tn), lambda i, j: (i, j)),
                  pl.BlockSpec((1, 1, tn), lambda i, j: (_group_of_tile(i, tiles_per_seq, n_batch), 0, j))],
        out_specs=pl.BlockSpec((tm, tn), lambda i, j: (i, j)),
        out_shape=jax.ShapeDtypeStruct((n_rows, d), F32),
        compiler_params=_cparams("parallel", "parallel"),
        name="out_project",
    )(a, w, x, gate)


def _lane_halves(a, own_half):
    lane = lax.broadcasted_iota(I32, a.shape, 1)
    af = a.astype(F32)
    sw = pltpu.roll(af, LANES // 2, axis=1)
    lo_src, hi_src = (af, sw) if own_half == 0 else (sw, af)
    lo = jnp.where(lane < LANES // 2, lo_src, 0.0).astype(BF16)
    hi = jnp.where(lane >= LANES // 2, hi_src, 0.0).astype(BF16)
    return lo, hi


def _swa_kernel(sink_ref, q_ref, kp_ref, kc_ref, kn_ref, kx_ref, vp_ref, vc_ref, vn_ref, vx_ref, o_ref,
                *, n_blocks, blk, n_ctx):
    i = pl.program_id(1)
    is_lat = i < n_blocks
    kcat = jnp.concatenate([kp_ref[...], kc_ref[...], kn_ref[...], kx_ref[...]], axis=0)
    vcat = jnp.concatenate([vp_ref[...], vc_ref[...], vn_ref[...], vx_ref[...]], axis=0)
    n_keys = 3 * blk + n_ctx
    kj = lax.broadcasted_iota(I32, (n_keys, blk), 0)
    qi = lax.broadcasted_iota(I32, (n_keys, blk), 1)
    in_band = jnp.abs(kj - blk - qi) <= WINDOW
    first_ok = jnp.where(i > 0, 0, blk)
    last_ok = jnp.where(is_lat, jnp.where(i < n_blocks - 1, 3 * blk, 2 * blk), 0)
    valid = (kj >= 3 * blk) | (in_band & (kj >= first_ok) & (kj < last_ok))
    maskadd = jnp.where(valid, 0.0, NEG_BIG).astype(F32)
    rep = SWA_HEADS // SWA_KV_HEADS
    dn = (((1,), (1,)), ((), ()))
    drow = lax.broadcasted_iota(I32, (LANES, n_keys), 0)
    for kv in range(SWA_KV_HEADS):
        ch, half = kv // 2, kv % 2
        klo, khi = _lane_halves(kcat[:, ch * LANES:(ch + 1) * LANES], half)
        vt = vcat[:, ch * LANES:(ch + 1) * LANES].astype(F32).T
        vsw = pltpu.roll(vt, LANES // 2, axis=0)
        lo_src, hi_src = (vt, vsw) if half == 0 else (vsw, vt)
        vlo = jnp.where(drow < LANES // 2, lo_src, 0.0).astype(BF16)
        vhi = jnp.where(drow >= LANES // 2, hi_src, 0.0).astype(BF16)
        for c in range(rep // 2):
            col = (kv * (rep // 2) + c) * LANES
            qc = q_ref[:, col:col + LANES]
            out_t = None
            for kk, vv, head in ((klo, vlo, kv * rep + 2 * c), (khi, vhi, kv * rep + 2 * c + 1)):
                s = lax.dot_general(kk, qc, dn, preferred_element_type=F32) + maskadd
                sink = sink_ref[head] * LOG2_E
                m = jnp.maximum(jnp.max(s, axis=0, keepdims=True), sink)
                p = jnp.exp2(s - m)
                denom = jnp.sum(p, axis=0, keepdims=True) + jnp.exp2(sink - m)
                o = jnp.dot(vv, p.astype(BF16), preferred_element_type=F32) * (1.0 / denom)
                out_t = o if out_t is None else out_t + o
            o_ref[:, col:col + LANES] = out_t.T.astype(o_ref.dtype)


def swa_attention(qkv, sink, *, n_batch, seq, n_ctx, with_ctx_queries):
    blk = WINDOW
    nb = seq // blk
    nbc = n_ctx // blk if with_ctx_queries else 0
    n_q = SWA_HEADS * SWA_HEAD_DIM
    kcol, vcol = n_q // 256, n_q // 256 + 1
    lat_blocks = n_batch * nb
    ctx_row0 = n_batch * seq

    def qrow(b, i):
        return jnp.where(i < nb, b * nb + i, lat_blocks + b * (n_ctx // blk) + (i - nb))

    def win(delta, col):
        return pl.BlockSpec((blk, 256), lambda b, i, s: (b * nb + jnp.clip(i + delta, 0, nb - 1), col))

    def ctx(col):
        return pl.BlockSpec((n_ctx, 256), lambda b, i, s: (ctx_row0 // n_ctx + b, col))

    n_rows = n_batch * (seq + (n_ctx if with_ctx_queries else 0))
    grid_spec = pltpu.PrefetchScalarGridSpec(
        num_scalar_prefetch=1,
        grid=(n_batch, nb + nbc),
        in_specs=[pl.BlockSpec((blk, n_q), lambda b, i, s: (qrow(b, i), 0)),
                  win(-1, kcol), win(0, kcol), win(1, kcol), ctx(kcol),
                  win(-1, vcol), win(0, vcol), win(1, vcol), ctx(vcol)],
        out_specs=pl.BlockSpec((blk, n_q), lambda b, i, s: (qrow(b, i), 0)),
    )
    return pl.pallas_call(
        functools.partial(_swa_kernel, n_blocks=nb, blk=blk, n_ctx=n_ctx),
        grid_spec=grid_spec,
        out_shape=jax.ShapeDtypeStruct((n_rows, n_q), BF16),
        compiler_params=_cparams("parallel", "parallel"),
        name="swa_attention",
    )(sink, qkv, qkv, qkv, qkv, qkv, qkv, qkv, qkv, qkv)


def _ga_kernel(q_ref, kl_ref, vl_ref, kx_ref, vx_ref, o_ref, s_sc, vt_sc, vtx_sc, acc_sc, *, n_lat_blocks, tq, tk):
    i = pl.program_id(2)
    rep = GA_HEADS // GA_KV_HEADS
    hd = GA_HEAD_DIM
    n_ctx = kx_ref.shape[0]
    seq = kl_ref.shape[0]
    dn = (((1,), (1,)), ((), ()))

    @pl.when(i == 0)
    def _():
        vtx_sc[...] = vx_ref[...].astype(F32).T.astype(BF16)
        for c in range(seq // tk):
            vt_sc[c] = vl_ref[c * tk:(c + 1) * tk, :].astype(F32).T.astype(BF16)

    qs = jnp.concatenate([q_ref[:, r * hd:(r + 1) * hd] for r in range(rep)], axis=0)

    n_chunks = jnp.where(i < n_lat_blocks, seq // tk, 0)

    s = lax.dot_general(kx_ref[...], qs, dn, preferred_element_type=F32)
    s_sc[0:n_ctx, :] = s

    def scores(c, m):
        r0 = pl.multiple_of(c * tk, tk)
        s = lax.dot_general(kl_ref[pl.ds(r0, tk), :], qs, dn, preferred_element_type=F32)
        s_sc[pl.ds(n_ctx + r0, tk), :] = s
        return jnp.maximum(m, jnp.max(s, axis=0, keepdims=True))

    m = lax.fori_loop(0, n_chunks, scores, jnp.max(s, axis=0, keepdims=True))

    p = jnp.exp2(s_sc[0:n_ctx, :] - m)
    acc_sc[...] = jnp.dot(vtx_sc[...], p.astype(BF16), preferred_element_type=F32)

    def values(c, l):
        r0 = pl.multiple_of(n_ctx + c * tk, LANES)
        p = jnp.exp2(s_sc[pl.ds(r0, tk), :] - m)
        acc_sc[...] += jnp.dot(vt_sc[c], p.astype(BF16), preferred_element_type=F32)
        return l + jnp.sum(p, axis=0, keepdims=True)

    l = lax.fori_loop(0, n_chunks, values, jnp.sum(p, axis=0, keepdims=True))
    out = (acc_sc[...] * (1.0 / l)).T
    for r in range(rep):
        o_ref[:, r * hd:(r + 1) * hd] = out[r * tq:(r + 1) * tq].astype(o_ref.dtype)


def global_attention(qkv, *, n_batch, seq, n_ctx, with_ctx_queries):
    tq, tk = 256, 512
    hd = GA_HEAD_DIM
    rep = GA_HEADS // GA_KV_HEADS
    n_q = GA_HEADS * hd
    nq = seq // tq
    nqc = n_ctx // tq if with_ctx_queries else 0
    kcol0, vcol0 = n_q // hd, n_q // hd + GA_KV_HEADS
    ctx_blk0 = n_batch * seq // n_ctx

    def qrow(b, i):
        return jnp.where(i < nq, b * nq + i, n_batch * nq + b * (n_ctx // tq) + (i - nq))

    n_rows = n_batch * (seq + (n_ctx if with_ctx_queries else 0))
    return pl.pallas_call(
        functools.partial(_ga_kernel, n_lat_blocks=nq, tq=tq, tk=tk),
        grid=(n_batch, GA_KV_HEADS, nq + nqc),
        in_specs=[pl.BlockSpec((tq, rep * hd), lambda b, k, i: (qrow(b, i), k)),
                  pl.BlockSpec((seq, hd), lambda b, k, i: (b, kcol0 + k)),
                  pl.BlockSpec((seq, hd), lambda b, k, i: (b, vcol0 + k)),
                  pl.BlockSpec((n_ctx, hd), lambda b, k, i: (ctx_blk0 + b, kcol0 + k)),
                  pl.BlockSpec((n_ctx, hd), lambda b, k, i: (ctx_blk0 + b, vcol0 + k))],
        out_specs=pl.BlockSpec((tq, rep * hd), lambda b, k, i: (qrow(b, i), k)),
        out_shape=jax.ShapeDtypeStruct((n_rows, n_q), BF16),
        scratch_shapes=[pltpu.VMEM((n_ctx + seq, rep * tq), F32), pltpu.VMEM((seq // tk, hd, tk), BF16),
                        pltpu.VMEM((hd, n_ctx), BF16), pltpu.VMEM((hd, rep * tq), F32)],
        compiler_params=_cparams("parallel", "parallel", "arbitrary"),
        name="global_attention",
    )(qkv, qkv, qkv, qkv, qkv)


def _ssd_inproj_kernel(x_ref, g_ref, sh_ref, sc_ref, w_ref, o_ref, dt_ref, h_ref, *, n_main_tiles):
    j = pl.program_id(1)

    @pl.when(j == 0)
    def _():
        h_ref[...] = _norm_modulate(x_ref, g_ref, sh_ref, sc_ref).astype(BF16)

    acc = jnp.dot(h_ref[...], w_ref[...], preferred_element_type=F32)

    @pl.when(j < n_main_tiles)
    def _():
        o_ref[...] = acc.astype(o_ref.dtype)

    @pl.when(j == n_main_tiles)
    def _():
        dt_ref[...] = acc[:, :LANES]


SSD_PROJ_COLS = 512


def ssd_in_project(x, g, shift, scale, w_pad, *, n_main, seq, n_batch):
    t, d = x.shape
    tm, tn = _proj_row_tile(seq, t - n_batch * seq), SSD_PROJ_COLS
    tiles_per_seq = seq // tm
    n_main_tiles = n_main // tn

    def mod_map(i, j):
        return (_group_of_tile(i, tiles_per_seq, n_batch), 0, 0)

    return pl.pallas_call(
        functools.partial(_ssd_inproj_kernel, n_main_tiles=n_main_tiles),
        grid=(t // tm, n_main_tiles + 1),
        in_specs=[pl.BlockSpec((tm, d), lambda i, j: (i, 0)),
                  pl.BlockSpec((1, d), lambda i, j: (0, 0)),
                  pl.BlockSpec((1, 1, d), mod_map),
                  pl.BlockSpec((1, 1, d), mod_map),
                  pl.BlockSpec((d, tn), lambda i, j: (0, j))],
        out_specs=[pl.BlockSpec((tm, tn), lambda i, j: (i, jnp.minimum(j, n_main_tiles - 1))),
                   pl.BlockSpec((tm, LANES), lambda i, j: (i, 0))],
        out_shape=[jax.ShapeDtypeStruct((t, n_main), BF16), jax.ShapeDtypeStruct((t, LANES), F32)],
        scratch_shapes=[pltpu.VMEM((tm, d), BF16)],
        compiler_params=_cparams("parallel", "arbitrary"),
        name="ssd_in_project",
    )(x, g, shift, scale, w_pad)


CONV_HALO = 8
CONV_ROWS = 256


def _conv_kernel(u_ref, w_ref, b_ref, o_ref, pad_ref, *, seq_len):
    tc = u_ref.shape[1]
    zeros = jnp.zeros((CONV_HALO, tc), F32)
    pad_ref[0:CONV_HALO, :] = zeros
    pad_ref[CONV_HALO + seq_len:2 * CONV_HALO + seq_len, :] = zeros
    for c in range(seq_len // CONV_ROWS):
        r0 = c * CONV_ROWS
        pad_ref[CONV_HALO + r0:CONV_HALO + r0 + CONV_ROWS, :] = u_ref[r0:r0 + CONV_ROWS, :].astype(F32)
    w = w_ref[...]
    b = b_ref[...]
    for c in range(seq_len // CONV_ROWS):
        r0 = c * CONV_ROWS
        acc = b
        for k in range(SSD_CONV):
            off = CONV_HALO + r0 + k - SSD_CONV // 2
            acc = acc + w[k:k + 1, :] * pad_ref[off:off + CONV_ROWS, :]
        o_ref[r0:r0 + CONV_ROWS, :] = (acc * jax.nn.sigmoid(acc)).astype(o_ref.dtype)


def ssd_conv(main, conv_w, conv_b, *, seq_len, n_seqs, row_block0, col0):
    n_ch = conv_w.shape[1]
    tc = 256
    return pl.pallas_call(
        functools.partial(_conv_kernel, seq_len=seq_len),
        grid=(n_seqs, n_ch // tc),
        in_specs=[pl.BlockSpec((seq_len, tc), lambda b, j: (row_block0 + b, col0 // tc + j)),
                  pl.BlockSpec((SSD_CONV, tc), lambda b, j: (0, j)),
                  pl.BlockSpec((1, tc), lambda b, j: (0, j))],
        out_specs=pl.BlockSpec((seq_len, tc), lambda b, j: (b, j)),
        out_shape=jax.ShapeDtypeStruct((n_seqs * seq_len, n_ch), BF16),
        scratch_shapes=[pltpu.VMEM((seq_len + 2 * CONV_HALO, tc), F32)],
        compiler_params=_cparams("parallel", "parallel"),
        name="ssd_conv",
    )(main, conv_w, conv_b.reshape(1, n_ch))


DIR_LANES = 16


def _prep_kernel(raw_ref, bias_ref, alog_ref, acs_o, eacs_o, dtT_o, acsT_o, wT_o, etot_o):
    ch = SSD_CHUNK
    row = lax.broadcasted_iota(I32, (ch, LANES), 0)
    lane = lax.broadcasted_iota(I32, (ch, LANES), 1)
    is_fwd = (lane % DIR_LANES) < DIR_LANES // 2
    neg_a = -jnp.exp(alog_ref[0])
    for c in range(raw_ref.shape[1] // ch):
        rows = slice(c * ch, (c + 1) * ch)
        v = raw_ref[0, rows, :] + bias_ref[0]
        dt = jnp.maximum(v, 0.0) + jnp.log1p(jnp.exp(-jnp.abs(v)))
        a = dt * neg_a
        fwd, rev = a, a
        s = 1
        while s < ch:
            fwd = fwd + jnp.where(row >= s, pltpu.roll(fwd, s, axis=0), 0.0)
            rev = rev + jnp.where(row < ch - s, pltpu.roll(rev, ch - s, axis=0), 0.0)
            s *= 2
        acs = jnp.where(is_fwd, fwd, rev)
        tot = jnp.where(is_fwd[0:1], fwd[ch - 1:ch, :], rev[0:1, :])
        acs_o[0, rows, :] = acs
        eacs_o[0, rows, :] = jnp.exp(acs)
        dtT_o[0, c] = dt.T[:DIR_LANES]
        acsT_o[0, c] = acs.T[:DIR_LANES]
        wT_o[0, c] = (jnp.exp(tot - acs) * dt).T[:DIR_LANES]
        etot_o[0, c] = jnp.broadcast_to(jnp.exp(tot), (8, LANES))


def ssd_prep(dt_g, bias_g, alog_g):
    n_g, t, _ = dt_g.shape
    ch = SSD_CHUNK
    nck_all = t // ch
    per_step = max(k for k in range(1, 9) if nck_all % k == 0)
    nck = nck_all // per_step
    vec = pl.BlockSpec((1, 1, LANES), lambda g, c: (g, 0, 0))
    rows = pl.BlockSpec((1, per_step * ch, LANES), lambda g, c: (g, c, 0))
    tr = pl.BlockSpec((1, per_step, DIR_LANES, LANES), lambda g, c: (g, c, 0, 0))
    return pl.pallas_call(
        _prep_kernel,
        grid=(n_g, nck),
        in_specs=[rows, vec, vec],
        out_specs=[rows, rows, tr, tr, tr, pl.BlockSpec((1, per_step, 8, LANES), lambda g, c: (g, c, 0, 0))],
        out_shape=[jax.ShapeDtypeStruct((n_g, t, LANES), F32), jax.ShapeDtypeStruct((n_g, t, LANES), F32),
                   jax.ShapeDtypeStruct((n_g, nck_all, DIR_LANES, LANES), F32),
                   jax.ShapeDtypeStruct((n_g, nck_all, DIR_LANES, LANES), F32),
                   jax.ShapeDtypeStruct((n_g, nck_all, DIR_LANES, LANES), F32),
                   jax.ShapeDtypeStruct((n_g, nck_all, 8, LANES), F32)],
        compiler_params=_cparams("parallel", "parallel"),
        name="ssd_prep",
    )(dt_g, bias_g, alog_g)


HEADS_PER_GROUP = 8
PAIRS_PER_GROUP = HEADS_PER_GROUP // 2
GROUP_COLS = HEADS_PER_GROUP * SSD_HEAD_DIM


def _scan_kernel(*refs, n_chunks, emit_state, alias_out):
    (x_ref, b_ref, c_ref, z_ref, acs_ref, eacs_ref, dtT_ref, acsT_ref, wT_ref, etot_ref,
     dskip_ref, ng_ref, init_ref) = refs[:13]
    rest = refs[13 + (1 if alias_out else 0):]
    y_ref = rest[0]
    fin_ref = rest[1] if emit_state else None
    ysc, st = rest[-2], rest[-1]
    ch = SSD_CHUNK
    st[...] = init_ref[0, 0]
    lo = lax.broadcasted_iota(I32, (1, LANES), 1) < LANES // 2
    li = lax.broadcasted_iota(I32, (ch, ch), 0)
    si = lax.broadcasted_iota(I32, (ch, ch), 1)
    dn = (((1,), (1,)), ((), ()))
    for d in range(2):
        mask = (si <= li) if d == 0 else (si >= li)

        def chunk(cc, carry, d=d, mask=mask):
            c = cc if d == 0 else n_chunks - 1 - cc
            rows = pl.ds(pl.multiple_of(c * ch, ch), ch)
            bc, cm = b_ref[rows, :], c_ref[rows, :]
            cb = lax.dot_general(cm, bc, dn, preferred_element_type=F32)
            bt = bc.astype(F32).T
            cf = cm.astype(F32)
            acs_blk = acs_ref[0, rows, :]
            eacs_blk = eacs_ref[0, rows, :]
            et = etot_ref[0, c]
            for pr in range(PAIRS_PER_GROUP):
                cols = slice(pr * LANES, (pr + 1) * LANES)
                x2 = x_ref[rows, cols]
                zero = jnp.zeros_like(x2)
                s2 = st[d * PAIRS_PER_GROUP + pr]
                r_lo = d * HEADS_PER_GROUP + 2 * pr
                y2 = jnp.zeros((ch, LANES), F32)
                snew = s2 * jnp.where(lo, et[0:1, r_lo:r_lo + 1], et[0:1, r_lo + 1:r_lo + 2])
                for hh in range(2):
                    r = r_lo + hh
                    keep = lo if hh == 0 else jnp.logical_not(lo)
                    xm = jnp.where(keep, x2, zero)
                    sm = jnp.where(keep, s2, 0.0).astype(BF16)
                    a_col = jnp.broadcast_to(acs_blk[:, r:r + 1], (ch, ch))
                    lm = jnp.exp(jnp.where(mask, a_col - acsT_ref[0, c, r:r + 1, :], NEG_BIG))
                    m_h = (cb * lm * dtT_ref[0, c, r:r + 1, :]).astype(BF16)
                    c_h = (cf * jnp.exp(a_col)).astype(BF16)
                    y2 = y2 + jnp.dot(m_h, xm, preferred_element_type=F32)
                    y2 = y2 + jnp.dot(c_h, sm, preferred_element_type=F32)
                    b_h = (bt * wT_ref[0, c, r:r + 1, :]).astype(BF16)
                    snew = snew + jnp.dot(b_h, xm, preferred_element_type=F32)
                st[d * PAIRS_PER_GROUP + pr] = snew
                if d == 0:
                    ysc[rows, cols] = y2
                else:
                    ysc[rows, cols] = ysc[rows, cols] + y2
            if d == 1:
                y = ysc[rows, :] + dskip_ref[0] * x_ref[rows, :].astype(F32)
                zf = z_ref[rows, :].astype(F32)
                y = y * (zf * jax.nn.sigmoid(zf))
                y = y * lax.rsqrt(jnp.mean(y * y, axis=-1, keepdims=True) + NORM_EPS) * ng_ref[0]
                y_ref[rows, :] = y.astype(y_ref.dtype)
            return carry

        lax.fori_loop(0, n_chunks, chunk, 0)
    if emit_state:
        fin_ref[0, 0] = st[...]


def ssd_scan(xbc, main, prep, dskip_g, normg_g, init_state, y_prev, *, seq_len, n_batch, row_block0,
             n_rows_total, emit_state):
    acs, eacs, dt_t, acs_t, w_t, etot = prep
    n_g = SSD_GROUPS
    nck = seq_len // SSD_CHUNK
    inner = n_g * GROUP_COLS
    bcol0 = inner // LANES
    ccol0 = bcol0 + n_g * SSD_STATE // LANES
    rb = row_block0

    def tr_spec(h):
        return pl.BlockSpec((1, nck, h, LANES), lambda b, g: (g, rb + b, 0, 0))

    vec = pl.BlockSpec((1, 1, GROUP_COLS), lambda b, g: (g, 0, 0))
    st_spec = pl.BlockSpec((1, 1, 2 * PAIRS_PER_GROUP, SSD_STATE, LANES), lambda b, g: (b, g, 0, 0, 0))
    in_specs = [pl.BlockSpec((seq_len, GROUP_COLS), lambda b, g: (b, g)),
                pl.BlockSpec((seq_len, LANES), lambda b, g: (b, bcol0 + g)),
                pl.BlockSpec((seq_len, LANES), lambda b, g: (b, ccol0 + g)),
                pl.BlockSpec((seq_len, GROUP_COLS), lambda b, g: (rb + b, g)),
                pl.BlockSpec((1, seq_len, LANES), lambda b, g: (g, rb + b, 0)),
                pl.BlockSpec((1, seq_len, LANES), lambda b, g: (g, rb + b, 0)),
                tr_spec(DIR_LANES), tr_spec(DIR_LANES), tr_spec(DIR_LANES), tr_spec(8),
                vec, vec, st_spec]
    args = [xbc, xbc, xbc, main, acs, eacs, dt_t, acs_t, w_t, etot, dskip_g, normg_g, init_state]
    aliases = {}
    if y_prev is not None:
        in_specs.append(pl.BlockSpec(memory_space=pl.ANY))
        args.append(y_prev)
        aliases = {len(args) - 1: 0}
    out_specs = [pl.BlockSpec((seq_len, GROUP_COLS), lambda b, g: (rb + b, g))]
    out_shape = [jax.ShapeDtypeStruct((n_rows_total, inner), BF16)]
    if emit_state:
        out_specs.append(st_spec)
        out_shape.append(jax.ShapeDtypeStruct(init_state.shape, F32))
    return pl.pallas_call(
        functools.partial(_scan_kernel, n_chunks=nck, emit_state=emit_state, alias_out=y_prev is not None),
        grid=(n_batch, n_g),
        in_specs=in_specs,
        out_specs=out_specs,
        out_shape=out_shape,
        scratch_shapes=[pltpu.VMEM((seq_len, GROUP_COLS), F32),
                        pltpu.VMEM((2 * PAIRS_PER_GROUP, SSD_STATE, LANES), F32)],
        input_output_aliases=aliases,
        compiler_params=_cparams("parallel", "parallel"),
        name="ssd_scan",
    )(*args)


def _router_kernel(x_ref, g_ref, sh_ref, sc_ref, whi_ref, wlo_ref, b_ref, tri_ref,
                   h_hbm, meta_o, gate_o, cnt_o, carry, hbuf, hsem, *, n_tiles):
    i = pl.program_id(0)
    slot = i % 2
    tm = x_ref.shape[0]

    @pl.when(i == 0)
    def _():
        carry[...] = jnp.zeros_like(carry)

    def h_copies(tile, sl):
        rows = pl.ds(tile * tm, tm)
        return [pltpu.make_async_copy(hbuf.at[sl, :, c * LANES:(c + 1) * LANES], h_hbm.at[rows, c, :], hsem.at[sl])
                for c in range(ROW_SLABS)]

    h = _norm_modulate(x_ref, g_ref, sh_ref, sc_ref)
    hbuf[slot] = h

    @pl.when(i >= 1)
    def _():
        for cp in h_copies(i - 1, 1 - slot):
            cp.wait()

    for cp in h_copies(i, slot):
        cp.start()

    @pl.when(i == n_tiles - 1)
    def _():
        for cp in h_copies(i, slot):
            cp.wait()

    h_hi = h.astype(BF16)
    h_lo = (h - h_hi.astype(F32)).astype(BF16)
    whi = whi_ref[...]
    logits = (jnp.dot(h_hi, whi, preferred_element_type=F32) + jnp.dot(h_lo, whi, preferred_element_type=F32)
              + jnp.dot(h_hi, wlo_ref[...], preferred_element_type=F32) + b_ref[...])
    tm = logits.shape[0]
    lane = lax.broadcasted_iota(I32, (tm, LANES), 1).astype(F32)
    work = logits
    vals, idxs = [], []
    for _ in range(TOP_K):
        m = jnp.max(work, axis=1, keepdims=True)
        idx = jnp.min(jnp.where(work == m, lane, float(LANES)), axis=1, keepdims=True)
        vals.append(m)
        idxs.append(idx)
        work = jnp.where(lane == idx, -jnp.inf, work)
    es = [jnp.exp(v - vals[0]) for v in vals]
    inv = 1.0 / (es[0] + es[1] + es[2] + es[3])
    multihot = sum(jnp.where(lane == idx, 1.0, 0.0) for idx in idxs)
    before = jnp.dot(tri_ref[...], multihot.astype(BF16), preferred_element_type=F32) + carry[...]
    meta = jnp.zeros((tm, LANES), F32)
    gates = jnp.zeros((tm, LANES), F32)
    for k in range(TOP_K):
        rank = jnp.sum(jnp.where(lane == idxs[k], before, 0.0), axis=1, keepdims=True)
        meta = meta + jnp.where(lane == float(k), idxs[k], 0.0) + jnp.where(lane == float(TOP_K + k), rank, 0.0)
        gates = gates + jnp.where(lane == float(k), es[k] * inv, 0.0)
    meta_o[...] = meta.astype(I32)
    gate_o[...] = gates
    carry[...] = carry[...] + jnp.sum(multihot, axis=0, keepdims=True)
    cnt_o[...] = carry[...]


def moe_router(x, g, shift, scale, w_hi, w_lo, b_pad, *, n_rows, seq, n_batch):
    d = x.shape[1]
    tm = ROW_TILE
    tiles_per_seq = seq // tm
    tri = jnp.asarray(np.tril(np.ones((tm, tm), np.float32), -1), dtype=BF16)

    def mod_map(i):
        return (_group_of_tile(i, tiles_per_seq, n_batch), 0, 0)

    full = lambda shape: pl.BlockSpec(shape, lambda i: (0,) * len(shape))
    return pl.pallas_call(
        functools.partial(_router_kernel, n_tiles=n_rows // tm),
        grid=(n_rows // tm,),
        in_specs=[pl.BlockSpec((tm, d), lambda i: (i, 0)), full((1, d)),
                  pl.BlockSpec((1, 1, d), mod_map), pl.BlockSpec((1, 1, d), mod_map),
                  full((d, LANES)), full((d, LANES)), full((1, LANES)), full((tm, tm))],
        out_specs=[pl.BlockSpec(memory_space=pl.ANY),
                   pl.BlockSpec((tm, LANES), lambda i: (i, 0)),
                   pl.BlockSpec((tm, LANES), lambda i: (i, 0)), full((1, LANES))],
        out_shape=[jax.ShapeDtypeStruct((n_rows, ROW_SLABS, LANES), F32),
                   jax.ShapeDtypeStruct((n_rows, LANES), I32),
                   jax.ShapeDtypeStruct((n_rows, LANES), F32), jax.ShapeDtypeStruct((1, LANES), F32)],
        scratch_shapes=[pltpu.VMEM((1, LANES), F32), pltpu.VMEM((2, tm, d), F32), pltpu.SemaphoreType.DMA((2,))],
        compiler_params=_cparams("arbitrary"),
        name="moe_router",
    )(x, g, shift, scale, w_hi, w_lo, b_pad, tri)


ROW_SLABS = 16
DMA_UNROLL = 8
DISPATCH_TILE = 256
FF_CHUNK = 256
OUT_CHUNK = 512


def _rows_to_slabs(ref, val):
    for c in range(ROW_SLABS):
        ref[:, c, :] = val[:, c * LANES:(c + 1) * LANES]


def _slabs_to_rows(ref, rows):
    return jnp.concatenate([ref[rows, c, :] for c in range(ROW_SLABS)], axis=1)


def _wait_rows(n, like_hbm, sem):
    pltpu.make_async_copy(like_hbm.at[pl.ds(0, n)], like_hbm.at[pl.ds(0, n)], sem).wait()


def _dispatch_kernel(tail_start_ref, tail_len_ref, nact_ref, dst_ref, h_ref, xs_hbm, zbuf, sem, zsem,
                     *, n_exp, n_blocks):
    s = pl.program_id(0)
    tm = h_ref.shape[0]

    @pl.when(s == 0)
    def _():
        zbuf[...] = jnp.zeros_like(zbuf)
        half = MOE_BLOCK // 2

        def unused(b, wait):
            for part in range(2):
                cp = pltpu.make_async_copy(zbuf, xs_hbm.at[pl.ds(b * MOE_BLOCK + part * half, half)], zsem)
                cp.wait() if wait else cp.start()

        lax.fori_loop(nact_ref[0], n_blocks, lambda b, c: (unused(b, False), c)[1], 0)
        lax.fori_loop(nact_ref[0], n_blocks, lambda b, c: (unused(b, True), c)[1], 0)
        sizes = [1 << b for b in range(MOE_BLOCK.bit_length() - 1)]

        def tails(e, wait):
            start, length = tail_start_ref[e], tail_len_ref[e]
            for size in sizes:
                @pl.when((length & size) != 0)
                def _(size=size):
                    pos = start + (length & ~(2 * size - 1))
                    cp = pltpu.make_async_copy(zbuf.at[pl.ds(0, size)], xs_hbm.at[pl.ds(pos, size)], zsem)
                    cp.wait() if wait else cp.start()

        lax.fori_loop(0, n_exp, lambda e, c: (tails(e, False), c)[1], 0)
        lax.fori_loop(0, n_exp, lambda e, c: (tails(e, True), c)[1], 0)

    def issue(it, c):
        for u in range(DMA_UNROLL):
            tok = it * (DMA_UNROLL // TOP_K) + u // TOP_K
            pltpu.make_async_copy(h_ref.at[pl.ds(tok, 1)], xs_hbm.at[pl.ds(dst_ref[0, 0, it * DMA_UNROLL + u], 1)],
                                  sem).start(priority=u % 2)
        return c

    lax.fori_loop(0, TOP_K * tm // DMA_UNROLL, issue, 0)
    _wait_rows(TOP_K * tm, xs_hbm, sem)


def moe_dispatch(tail_start, tail_len, n_active, dest, h3, *, n_blocks):
    n_rows = h3.shape[0]
    tm = DISPATCH_TILE
    n_exp = tail_start.shape[0]
    grid_spec = pltpu.PrefetchScalarGridSpec(
        num_scalar_prefetch=3,
        grid=(n_rows // tm,),
        in_specs=[pl.BlockSpec((1, 1, TOP_K * tm), lambda s, a, b, c: (s, 0, 0), memory_space=pltpu.SMEM),
                  pl.BlockSpec((tm, ROW_SLABS, LANES), lambda s, a, b, c: (s, 0, 0))],
        out_specs=pl.BlockSpec(memory_space=pl.ANY),
        scratch_shapes=[pltpu.VMEM((MOE_BLOCK // 2, ROW_SLABS, LANES), F32),
                        pltpu.SemaphoreType.DMA(()), pltpu.SemaphoreType.DMA(())],
    )
    return pl.pallas_call(
        functools.partial(_dispatch_kernel, n_exp=n_exp, n_blocks=n_blocks),
        grid_spec=grid_spec,
        out_shape=jax.ShapeDtypeStruct((n_blocks * MOE_BLOCK, ROW_SLABS, LANES), F32),
        compiler_params=_cparams("arbitrary"),
        name="moe_dispatch",
    )(tail_start, tail_len, n_active, dest.reshape(n_rows // tm, 1, TOP_K * tm), h3)


def _ffn_kernel(bexp_ref, nact_ref, xs_hbm, wgu_ref, bgu_ref, wd_ref, bd_ref, o_hbm, xbuf, ybuf, sem, osem):
    s = pl.program_id(0)
    n_steps = pl.num_programs(0)
    slot = s % 2
    nact = nact_ref[0]

    def block_copies(blk, sl):
        rows = pl.ds(blk * MOE_BLOCK, MOE_BLOCK)
        return [pltpu.make_async_copy(xs_hbm.at[rows, c, :], xbuf.at[sl, :, c * LANES:(c + 1) * LANES], sem.at[sl])
                for c in range(ROW_SLABS)]

    def out_copies(blk, sl):
        rows = pl.ds(blk * MOE_BLOCK, MOE_BLOCK)
        return [pltpu.make_async_copy(ybuf.at[sl, :, c * LANES:(c + 1) * LANES], o_hbm.at[rows, c, :], osem.at[sl])
                for c in range(ROW_SLABS)]

    @pl.when(jnp.logical_and(s == 0, nact > 0))
    def _():
        for cp in block_copies(0, 0):
            cp.start()

    @pl.when(s + 1 < nact)
    def _():
        for cp in block_copies(s + 1, 1 - slot):
            cp.start()

    @pl.when(s < nact)
    def _():
        for cp in block_copies(s, slot):
            cp.wait()
        x = xbuf[slot].astype(BF16)
        ff = EXPERT_FF

        def swiglu(glu, lin):
            glu = jnp.minimum(glu, SWIGLU_LIMIT)
            lin = jnp.clip(lin, -SWIGLU_LIMIT, SWIGLU_LIMIT)
            return (glu * jax.nn.sigmoid(SWIGLU_ALPHA * glu) * (lin + 1.0)).astype(BF16)

        acts = []
        full = ff // FF_CHUNK * FF_CHUNK
        for lo in range(0, full, FF_CHUNK):
            hi = lo + FF_CHUNK
            glu = jnp.dot(x, wgu_ref[0, 0, :, lo:hi], preferred_element_type=F32) + bgu_ref[0, 0, :, lo:hi]
            lin = (jnp.dot(x, wgu_ref[0, 0, :, ff + lo:ff + hi], preferred_element_type=F32)
                   + bgu_ref[0, 0, :, ff + lo:ff + hi])
            acts.append(swiglu(glu, lin))
        if full < ff:
            rem = ff - full
            w_rem = jnp.concatenate([wgu_ref[0, 0, :, full:ff], wgu_ref[0, 0, :, ff + full:2 * ff]], axis=1)
            b_rem = jnp.concatenate([bgu_ref[0, 0, :, full:ff], bgu_ref[0, 0, :, ff + full:2 * ff]], axis=1)
            gl = jnp.dot(x, w_rem, preferred_element_type=F32) + b_rem
            acts.append(swiglu(gl[:, :rem], gl[:, rem:]))
        act = jnp.concatenate(acts, axis=1)
        for n0 in range(0, ybuf.shape[2], OUT_CHUNK):
            y = jnp.dot(act, wd_ref[0, 0, :, n0:n0 + OUT_CHUNK], preferred_element_type=F32)
            ybuf[slot, :, n0:n0 + OUT_CHUNK] = y + bd_ref[0, 0, :, n0:n0 + OUT_CHUNK]

    @pl.when(s >= nact)
    def _():
        ybuf[slot] = jnp.zeros(ybuf.shape[1:], F32)

    @pl.when(s >= 1)
    def _():
        for cp in out_copies(s - 1, 1 - slot):
            cp.wait()

    for cp in out_copies(s, slot):
        cp.start()

    @pl.when(s == n_steps - 1)
    def _():
        for cp in out_copies(s, slot):
            cp.wait()


def moe_ffn_blocks(block_exp, n_active, xs, w_gu, b_gu, w_down, b_down, *, layer):
    n_blocks = xs.shape[0] // MOE_BLOCK
    d, ff2 = w_gu.shape[2], w_gu.shape[3]
    wmap = lambda s, be, na: (layer, be[s], 0, 0)
    grid_spec = pltpu.PrefetchScalarGridSpec(
        num_scalar_prefetch=2,
        grid=(n_blocks,),
        in_specs=[pl.BlockSpec(memory_space=pl.ANY),
                  pl.BlockSpec((1, 1, d, ff2), wmap), pl.BlockSpec((1, 1, 1, ff2), wmap),
                  pl.BlockSpec((1, 1, ff2 // 2, d), wmap), pl.BlockSpec((1, 1, 1, d), wmap)],
        out_specs=pl.BlockSpec(memory_space=pl.ANY),
        scratch_shapes=[pltpu.VMEM((2, MOE_BLOCK, d), F32), pltpu.VMEM((2, MOE_BLOCK, d), F32),
                        pltpu.SemaphoreType.DMA((2,)), pltpu.SemaphoreType.DMA((2,))],
    )
    return pl.pallas_call(
        _ffn_kernel,
        grid_spec=grid_spec,
        out_shape=jax.ShapeDtypeStruct(xs.shape, F32),
        compiler_params=_cparams("arbitrary"),
        name="moe_ffn",
    )(block_exp, n_active, xs, w_gu, b_gu, w_down, b_down)


COMBINE_TILE = 128


def _combine_kernel(d0_ref, d1_ref, g_ref, rows_hbm, x_hbm, mod_ref, o_hbm, gbuf, xbuf, obuf, gsem, xsem, osem,
                    *, n_tiles):
    s = pl.program_id(0)
    slot = s % 2
    tm = COMBINE_TILE

    def gather(idx_ref, sl):
        def issue(it, c):
            for u in range(DMA_UNROLL):
                r = it * DMA_UNROLL + u
                pltpu.make_async_copy(rows_hbm.at[pl.ds(idx_ref[0, 0, r], 1)], gbuf.at[sl, pl.ds(r, 1)],
                                      gsem.at[sl]).start(priority=u % 2)
            return c

        lax.fori_loop(0, TOP_K * tm // DMA_UNROLL, issue, 0)

    def x_copies(tile, sl):
        rows = pl.ds(tile * tm, tm)
        return [pltpu.make_async_copy(x_hbm.at[rows, c * LANES:(c + 1) * LANES], xbuf.at[sl, :, c, :], xsem.at[sl])
                for c in range(ROW_SLABS)]

    def out_copies(tile, sl):
        rows = pl.ds(tile * tm, tm)
        return [pltpu.make_async_copy(obuf.at[sl, :, c, :], o_hbm.at[rows, c * LANES:(c + 1) * LANES], osem.at[sl])
                for c in range(ROW_SLABS)]

    @pl.when(s == 0)
    def _():
        gather(d0_ref, 0)
        for cp in x_copies(0, 0):
            cp.start()

    @pl.when(s + 1 < n_tiles)
    def _():
        gather(d1_ref, 1 - slot)
        for cp in x_copies(s + 1, 1 - slot):
            cp.start()

    _wait_rows(TOP_K * tm, rows_hbm, gsem.at[slot])
    for cp in x_copies(s, slot):
        cp.wait()
    mod = mod_ref[0]

    def token(t, c):
        acc = g_ref[0, 0, TOP_K * t] * gbuf[slot, t]
        for k in range(1, TOP_K):
            acc = acc + g_ref[0, 0, TOP_K * t + k] * gbuf[slot, k * tm + t]
        obuf[slot, t] = xbuf[slot, t] + mod * acc
        return c

    lax.fori_loop(0, tm, token, 0, unroll=4)

    @pl.when(s >= 1)
    def _():
        for cp in out_copies(s - 1, 1 - slot):
            cp.wait()

    for cp in out_copies(s, slot):
        cp.start()

    @pl.when(s == n_tiles - 1)
    def _():
        for cp in out_copies(s, slot):
            cp.wait()


def moe_combine(dest_tiles, out_rows, x, gates, gate_mod, *, n_rows, seq, n_batch):
    d = x.shape[1]
    tm = COMBINE_TILE
    n_tiles = n_rows // tm
    tiles_per_seq = seq // tm
    gate_tiles = gates[:n_rows, :TOP_K].reshape(n_tiles, 1, TOP_K * tm)
    mod_slabs = gate_mod.reshape(gate_mod.shape[0], ROW_SLABS, LANES)

    def smem_spec(ahead):
        return pl.BlockSpec((1, 1, TOP_K * tm), lambda s: (jnp.minimum(s + ahead, n_tiles - 1), 0, 0),
                            memory_space=pltpu.SMEM)

    slab_buf = pltpu.VMEM((2, tm, ROW_SLABS, LANES), F32)
    return pl.pallas_call(
        functools.partial(_combine_kernel, n_tiles=n_tiles),
        grid=(n_tiles,),
        in_specs=[smem_spec(0), smem_spec(1), smem_spec(0),
                  pl.BlockSpec(memory_space=pl.ANY), pl.BlockSpec(memory_space=pl.ANY),
                  pl.BlockSpec((1, ROW_SLABS, LANES), lambda s: (_group_of_tile(s, tiles_per_seq, n_batch), 0, 0))],
        out_specs=pl.BlockSpec(memory_space=pl.ANY),
        out_shape=jax.ShapeDtypeStruct((n_rows, d), F32),
        scratch_shapes=[pltpu.VMEM((2, TOP_K * tm, ROW_SLABS, LANES), F32), slab_buf, slab_buf,
                        pltpu.SemaphoreType.DMA((2,)), pltpu.SemaphoreType.DMA((2,)), pltpu.SemaphoreType.DMA((2,))],
        compiler_params=_cparams("arbitrary"),
        name="moe_combine",
    )(dest_tiles, dest_tiles, gate_tiles, out_rows, x, mod_slabs)


def moe_layer(x, g, shift, scale, gate_mod, w_router, b_router, w_gu, b_gu, w_down, b_down,
              *, layer, n_rows, seq, n_batch):
    n_exp = w_router.shape[1]
    w_pad = jnp.pad(w_router, ((0, 0), (0, LANES - n_exp)))
    w_hi = w_pad.astype(BF16)
    w_lo = (w_pad - w_hi.astype(F32)).astype(BF16)
    b_pad = jnp.concatenate([b_router, jnp.full((LANES - n_exp,), NEG_BIG, F32)])[None, :]
    h3, meta, gates, cnt = moe_router(x, g, shift, scale, w_hi, w_lo, b_pad, n_rows=n_rows, seq=seq, n_batch=n_batch)
    top_idx, rank = meta[:, :TOP_K], meta[:, TOP_K:2 * TOP_K]
    counts = cnt[0, :n_exp].astype(I32)
    padded = (counts + MOE_BLOCK - 1) // MOE_BLOCK * MOE_BLOCK
    pad_end = jnp.cumsum(padded)
    pad_start = pad_end - padded
    onehot = top_idx[:, :, None] == jnp.arange(n_exp, dtype=I32)[None, None, :]
    dest = jnp.sum(jnp.where(onehot, pad_start[None, None, :], 0), axis=-1) + rank
    n_blocks = n_rows * TOP_K // MOE_BLOCK + n_exp
    block_row0 = jnp.arange(n_blocks, dtype=I32) * MOE_BLOCK
    block_exp = jnp.minimum(jnp.sum((pad_end[None, :] <= block_row0[:, None]).astype(I32), axis=1), n_exp - 1)
    n_active = (pad_end[-1:] // MOE_BLOCK).astype(I32)
    xs = moe_dispatch(pad_start + counts, padded - counts, n_active, dest, h3, n_blocks=n_blocks)
    out_rows = moe_ffn_blocks(block_exp, n_active, xs, w_gu, b_gu, w_down, b_down, layer=layer)
    tm = COMBINE_TILE
    dest_tiles = dest.reshape(n_rows // tm, tm, TOP_K).transpose(0, 2, 1).reshape(n_rows // tm, 1, TOP_K * tm)
    return moe_combine(dest_tiles, out_rows, x, gates, gate_mod, n_rows=n_rows, seq=seq, n_batch=n_batch)


def attention_layer(x, g, shift, scale, gate_mod, w_in, q_gain, k_gain, w_out, sink, *, n_heads, n_kv, head_dim,
                    seq, n_ctx, n_batch, need_ctx):
    n_qk = (n_heads + n_kv) * head_dim
    q_scale = head_dim ** -0.5 * LOG2_E
    gain_row = jnp.concatenate([jnp.tile(q_gain, n_heads) * q_scale, jnp.tile(k_gain, n_kv),
                                jnp.ones((n_kv * head_dim,), F32)])[None, :]
    qkv = qkv_project(x, g, shift, scale, w_in.astype(BF16), gain_row, head_group_matrix(256, head_dim),
                      rope_tables(seq, head_dim, _proj_row_tile(seq, x.shape[0] - n_batch * seq)),
                      n_qk_cols=n_qk, head_dim=head_dim, seq=seq, n_batch=n_batch)
    if sink is not None:
        att = swa_attention(qkv, sink, n_batch=n_batch, seq=seq, n_ctx=n_ctx, with_ctx_queries=need_ctx)
    else:
        att = global_attention(qkv, n_batch=n_batch, seq=seq, n_ctx=n_ctx, with_ctx_queries=need_ctx)
    n_rows = n_batch * (seq + (n_ctx if need_ctx else 0))
    return out_project(att, w_out.astype(BF16), x, gate_mod, n_rows=n_rows, seq=seq, n_batch=n_batch)


def ssd_layer(x, g, shift, scale, gate_mod, w_in, conv_w, conv_b, dt_bias, a_log, d_skip, norm_g, w_out,
              *, seq, n_ctx, n_batch):
    t, d = x.shape
    n_g = SSD_GROUPS
    n_heads = d_skip.shape[0]
    inner = n_heads * SSD_HEAD_DIM
    n_main = inner + conv_w.shape[1]
    w_pad = jnp.pad(w_in, ((0, 0), (0, n_main + SSD_PROJ_COLS - w_in.shape[1]))).astype(BF16)
    main, dt_raw = ssd_in_project(x, g, shift, scale, w_pad, n_main=n_main, seq=seq, n_batch=n_batch)

    def to_groups(a):
        lead = a.shape[:-1]
        a = a.reshape(lead + (2, n_g, HEADS_PER_GROUP))
        a = jnp.moveaxis(a, -2, 0).reshape((n_g,) + lead + (DIR_LANES,))
        return jnp.pad(a, [(0, 0)] * (a.ndim - 1) + [(0, LANES - DIR_LANES)])

    prep = ssd_prep(to_groups(dt_raw[:, :2 * n_heads]), to_groups(dt_bias.reshape(1, -1)),
                    to_groups(a_log.reshape(1, -1)))
    dskip_g = jnp.repeat(d_skip, SSD_HEAD_DIM).reshape(n_g, 1, GROUP_COLS)
    normg_g = norm_g.reshape(n_g, 1, GROUP_COLS)
    n_lat = n_batch * seq
    xbc_c = ssd_conv(main, conv_w, conv_b, seq_len=n_ctx, n_seqs=n_batch, row_block0=n_lat // n_ctx, col0=inner)
    xbc_l = ssd_conv(main, conv_w, conv_b, seq_len=seq, n_seqs=n_batch, row_block0=0, col0=inner)
    zero_state = jnp.zeros((n_batch, n_g, 2 * PAIRS_PER_GROUP, SSD_STATE, LANES), F32)
    y = jnp.zeros((t, inner), BF16)
    y, ctx_state = ssd_scan(xbc_c, main, prep, dskip_g, normg_g, zero_state, y, seq_len=n_ctx, n_batch=n_batch,
                            row_block0=n_lat // n_ctx, n_rows_total=t, emit_state=True)
    (y,) = ssd_scan(xbc_l, main, prep, dskip_g, normg_g, ctx_state, y, seq_len=seq, n_batch=n_batch,
                    row_block0=0, n_rows_total=t, emit_state=False)
    return out_project(y, w_out.astype(BF16), x, gate_mod, n_rows=t, seq=seq, n_batch=n_batch)


def kernel(x, c, ctx, c_ctx, ada_w, ada_b, norm_mix, norm_ffn, swa_w_in, swa_q_norm, swa_k_norm, swa_sink,
           swa_w_out, ssd_w_in, ssd_conv_w, ssd_conv_b, ssd_dt_bias, ssd_a_log, ssd_d, ssd_norm, ssd_w_out,
           ga_w_in, ga_q_norm, ga_k_norm, ga_w_out, moe_w_router, moe_b_router, moe_w_gate_up, moe_b_gate_up,
           moe_w_down, moe_b_down):
    n_batch, seq, d = x.shape
    n_ctx = ctx.shape[1]
    n_lat = n_batch * seq
    xs = jnp.concatenate([x.reshape(n_lat, d), ctx.reshape(n_batch * n_ctx, d)], axis=0)
    cvec = jnp.concatenate([c, c_ctx[None, :], jnp.zeros((8 - n_batch - 1, d), F32)], axis=0)
    mods = adaln_all(cvec, ada_w, ada_b)[:, :n_batch + 1].reshape(DEPTH, n_batch + 1, 6, 1, d)
    dims = dict(seq=seq, n_ctx=n_ctx, n_batch=n_batch)
    w_gu_bf, w_down_bf = moe_w_gate_up.astype(BF16), moe_w_down.astype(BF16)
    for i in range(DEPTH):
        kind, j = i % N_MIXERS, i // N_MIXERS
        need_ctx = i < DEPTH - 1
        m = [mods[i, :, k] for k in range(6)]
        g_mix = norm_mix[i][None, :]
        if kind == 0:
            xs = attention_layer(xs, g_mix, m[0], m[1], m[2], swa_w_in[j], swa_q_norm[j], swa_k_norm[j],
                                 swa_w_out[j], swa_sink[j], n_heads=SWA_HEADS, n_kv=SWA_KV_HEADS,
                                 head_dim=SWA_HEAD_DIM, need_ctx=need_ctx, **dims)
        elif kind == 1:
            xs = ssd_layer(xs, g_mix, m[0], m[1], m[2], ssd_w_in[j], ssd_conv_w[j], ssd_conv_b[j], ssd_dt_bias[j],
                           ssd_a_log[j], ssd_d[j], ssd_norm[j], ssd_w_out[j], **dims)
        else:
            xs = attention_layer(xs, g_mix, m[0], m[1], m[2], ga_w_in[j], ga_q_norm[j], ga_k_norm[j], ga_w_out[j],
                                 None, n_heads=GA_HEADS, n_kv=GA_KV_HEADS, head_dim=GA_HEAD_DIM,
                                 need_ctx=need_ctx, **dims)
        n_rows = n_batch * (seq + (n_ctx if need_ctx else 0))
        xs = moe_layer(xs, norm_ffn[i][None, :], m[3], m[4], m[5], moe_w_router[i], moe_b_router[i],
                       w_gu_bf, moe_b_gate_up[:, :, None, :], w_down_bf, moe_b_down[:, :, None, :],
                       layer=i, n_rows=n_rows, seq=seq, n_batch=n_batch)
    return xs[:n_lat].reshape(n_batch, seq, d)
```

```python
import functools
import math

import jax
import jax.numpy as jnp
import numpy as np
from jax import lax
from jax.experimental import pallas as pl
from jax.experimental.pallas import tpu as pltpu

F32 = jnp.float32
BF16 = jnp.bfloat16
I32 = jnp.int32

DEPTH = 4
N_MIXERS = 3
GRID_W = 64
NORM_EPS = 1e-6
ROPE_THETA = 10000.0

SWA_HEADS, SWA_KV_HEADS, SWA_HEAD_DIM, WINDOW = 32, 4, 64, 128
SSD_HEAD_DIM, SSD_GROUPS, SSD_STATE, SSD_CONV, SSD_CHUNK = 64, 8, 128, 5, 128
GA_HEADS, GA_KV_HEADS, GA_HEAD_DIM = 16, 4, 128
N_EXPERTS, TOP_K, EXPERT_FF = 32, 4, 896
SWIGLU_ALPHA, SWIGLU_LIMIT = 1.702, 7.0
MOE_BLOCK = 256

LANES = 128
ROW_TILE = 512
NEG_BIG = -1e30
LOG2_E = math.log2(math.e)
VMEM_LIMIT = 56 * 1024 * 1024


def _cparams(*sem):
    return pltpu.CompilerParams(dimension_semantics=sem, vmem_limit_bytes=VMEM_LIMIT)


def _adaln_kernel(c_ref, w_ref, b_ref, o_ref):
    a = c_ref[...]
    a = (a * jax.nn.sigmoid(a)).astype(BF16)
    o_ref[0] = jnp.dot(a, w_ref[0].astype(BF16), preferred_element_type=F32) + b_ref[0]


def adaln_all(cvec, ada_w, ada_b):
    n_layers, d, n = ada_w.shape
    tn = 1024
    return pl.pallas_call(
        _adaln_kernel,
        grid=(n_layers, n // tn),
        in_specs=[pl.BlockSpec((8, d), lambda l, j: (0, 0)),
                  pl.BlockSpec((1, d, tn), lambda l, j: (l, 0, j)),
                  pl.BlockSpec((1, 1, tn), lambda l, j: (l, 0, j))],
        out_specs=pl.BlockSpec((1, 8, tn), lambda l, j: (l, 0, j)),
        out_shape=jax.ShapeDtypeStruct((n_layers, 8, n), F32),
        compiler_params=_cparams("parallel", "parallel"),
        name="adaln",
    )(cvec, ada_w, ada_b.reshape(n_layers, 1, n))


def _norm_modulate(x_ref, g_ref, sh_ref, sc_ref):
    x = x_ref[...]
    ms = jnp.mean(x * x, axis=-1, keepdims=True)
    y = x * lax.rsqrt(ms + NORM_EPS) * g_ref[...]
    return y * (1.0 + sc_ref[0]) + sh_ref[0]


def _group_of_tile(i, tiles_per_seq, n_batch):
    return jnp.minimum(i // tiles_per_seq, n_batch)


def _qkv_kernel(x_ref, g_ref, sh_ref, sc_ref, w_ref, gain_ref, gmat_ref, cos_ref, sa_ref, sb_ref,
                o_ref, h_ref, *, n_qk_tiles, head_dim):
    j = pl.program_id(1)

    @pl.when(j == 0)
    def _():
        h_ref[...] = _norm_modulate(x_ref, g_ref, sh_ref, sc_ref).astype(BF16)

    sub = gmat_ref.shape[0]
    q4 = head_dim // 4
    for t in range(w_ref.shape[1] // sub):
        cols = slice(t * sub, (t + 1) * sub)
        acc = jnp.dot(h_ref[...], w_ref[:, cols], preferred_element_type=F32)
        sub_tile = j * (w_ref.shape[1] // sub) + t

        @pl.when(sub_tile < n_qk_tiles)
        def _(acc=acc, cols=cols):
            ss = jnp.dot((acc * acc).astype(BF16), gmat_ref[...], preferred_element_type=F32)
            yn = acc * lax.rsqrt(ss * (1.0 / head_dim) + NORM_EPS) * gain_ref[:, cols]
            cos, sa, sb = cos_ref[...], sa_ref[...], sb_ref[...]
            for c in range(sub // LANES):
                yc = yn[:, c * LANES:(c + 1) * LANES]
                up = pltpu.roll(yc, LANES - q4, axis=1)
                dn = pltpu.roll(yc, q4, axis=1)
                lanes = slice(cols.start + c * LANES, cols.start + (c + 1) * LANES)
                o_ref[:, lanes] = (yc * cos + up * sa + dn * sb).astype(o_ref.dtype)

        @pl.when(sub_tile >= n_qk_tiles)
        def _(acc=acc, cols=cols):
            o_ref[:, cols] = acc.astype(o_ref.dtype)


def _proj_row_tile(seq, n_ctx_rows):
    return next(tm for tm in (1024, 512) if seq % tm == 0 and n_ctx_rows % tm == 0)


def qkv_project(x, g, shift, scale, w, gain_row, gmat, rope_tabs, *, n_qk_cols, head_dim, seq, n_batch):
    t, d = x.shape
    n = w.shape[1]
    tm, tn = _proj_row_tile(seq, t - n_batch * seq), 512
    sub = gmat.shape[0]
    tiles_per_seq = seq // tm
    n_lat_tiles = n_batch * tiles_per_seq

    def tab_map(i, j):
        return (jnp.where(i < n_lat_tiles, i % tiles_per_seq, tiles_per_seq), 0)

    def mod_map(i, j):
        return (_group_of_tile(i, tiles_per_seq, n_batch), 0, 0)

    tab_spec = pl.BlockSpec((tm, LANES), tab_map)
    return pl.pallas_call(
        functools.partial(_qkv_kernel, n_qk_tiles=n_qk_cols // sub, head_dim=head_dim),
        grid=(t // tm, n // tn),
        in_specs=[pl.BlockSpec((tm, d), lambda i, j: (i, 0)),
                  pl.BlockSpec((1, d), lambda i, j: (0, 0)),
                  pl.BlockSpec((1, 1, d), mod_map),
                  pl.BlockSpec((1, 1, d), mod_map),
                  pl.BlockSpec((d, tn), lambda i, j: (0, j)),
                  pl.BlockSpec((1, tn), lambda i, j: (0, j)),
                  pl.BlockSpec((sub, sub), lambda i, j: (0, 0)),
                  tab_spec, tab_spec, tab_spec],
        out_specs=pl.BlockSpec((tm, tn), lambda i, j: (i, j)),
        out_shape=jax.ShapeDtypeStruct((t, n), BF16),
        scratch_shapes=[pltpu.VMEM((tm, d), BF16)],
        compiler_params=_cparams("parallel", "arbitrary"),
        name="qkv_project",
    )(x, g, shift, scale, w, gain_row, gmat, *rope_tabs)


def rope_tables(seq, head_dim, tile):
    q4 = head_dim // 4
    pos = np.arange(seq)
    inv = ROPE_THETA ** (-np.arange(q4, dtype=np.float64) / q4)
    lane = np.arange(LANES) % head_dim
    axis, half, idx = lane // (2 * q4), (lane // q4) % 2, lane % q4
    p = np.where(axis[None, :] == 0, (pos // GRID_W)[:, None], (pos % GRID_W)[:, None]).astype(np.float64)
    ang = (p.astype(np.float32) * inv.astype(np.float32)[idx][None, :]).astype(np.float32)
    cos, sin = np.cos(ang), np.sin(ang)
    sa = np.where(half[None, :] == 0, -sin, 0.0)
    sb = np.where(half[None, :] == 1, sin, 0.0)
    pad = np.zeros((tile, LANES), np.float32)
    return tuple(jnp.asarray(np.concatenate([a.astype(np.float32), b], axis=0))
                 for a, b in ((cos, pad + 1.0), (sa, pad), (sb, pad)))


def head_group_matrix(width, head_dim):
    r = np.arange(width) // head_dim
    return jnp.asarray((r[:, None] == r[None, :]).astype(np.float32), dtype=BF16)


def _oproj_kernel(a_ref, w_ref, x_ref, gate_ref, o_ref):
    acc = jnp.dot(a_ref[...], w_ref[...], preferred_element_type=F32)
    o_ref[...] = x_ref[...] + gate_ref[0] * acc


def out_project(a, w, x, gate, *, n_rows, seq, n_batch):
    k = a.shape[1]
    d = w.shape[1]
    tm, tn = _proj_row_tile(seq, a.shape[0] - n_batch * seq if a.shape[0] > n_batch * seq else seq), 512
    tiles_per_seq = seq // tm
    return pl.pallas_call(
        _oproj_kernel,
        grid=(n_rows // tm, d // tn),
        in_specs=[pl.BlockSpec((tm, k), lambda i, j: (i, 0)),
                  pl.BlockSpec((k, tn), lambda i, j: (0, j)),
                  pl.BlockSpec((tm, tn), lambda i, j: (i, j)),
                  pl.BlockSpec((1, 1, tn), lambda i, j: (_group_of_tile(i, tiles_per_seq, n_batch), 0, j))],
        out_specs=pl.BlockSpec((tm, tn), lambda i, j: (i, j)),
        out_shape=jax.ShapeDtypeStruct((n_rows, d), F32),
        compiler_params=_cparams("parallel", "parallel"),
        name="out_project",
    )(a, w, x, gate)


def _lane_halves(a, own_half):
    lane = lax.broadcasted_iota(I32, a.shape, 1)
    af = a.astype(F32)
    sw = pltpu.roll(af, LANES // 2, axis=1)
    lo_src, hi_src = (af, sw) if own_half == 0 else (sw, af)
    lo = jnp.where(lane < LANES // 2, lo_src, 0.0).astype(BF16)
    hi = jnp.where(lane >= LANES // 2, hi_src, 0.0).astype(BF16)
    return lo, hi


def _swa_kernel(sink_ref, q_ref, kp_ref, kc_ref, kn_ref, kx_ref, vp_ref, vc_ref, vn_ref, vx_ref, o_ref,
                *, n_blocks, blk, n_ctx):
    i = pl.program_id(1)
    is_lat = i < n_blocks
    kcat = jnp.concatenate([kp_ref[...], kc_ref[...], kn_ref[...], kx_ref[...]], axis=0)
    vcat = jnp.concatenate([vp_ref[...], vc_ref[...], vn_ref[...], vx_ref[...]], axis=0)
    n_keys = 3 * blk + n_ctx
    kj = lax.broadcasted_iota(I32, (n_keys, blk), 0)
    qi = lax.broadcasted_iota(I32, (n_keys, blk), 1)
    in_band = jnp.abs(kj - blk - qi) <= WINDOW
    first_ok = jnp.where(i > 0, 0, blk)
    last_ok = jnp.where(is_lat, jnp.where(i < n_blocks - 1, 3 * blk, 2 * blk), 0)
    valid = (kj >= 3 * blk) | (in_band & (kj >= first_ok) & (kj < last_ok))
    maskadd = jnp.where(valid, 0.0, NEG_BIG).astype(F32)
    rep = SWA_HEADS // SWA_KV_HEADS
    dn = (((1,), (1,)), ((), ()))
    drow = lax.broadcasted_iota(I32, (LANES, n_keys), 0)
    for kv in range(SWA_KV_HEADS):
        ch, half = kv // 2, kv % 2
        klo, khi = _lane_halves(kcat[:, ch * LANES:(ch + 1) * LANES], half)
        vt = vcat[:, ch * LANES:(ch + 1) * LANES].astype(F32).T
        vsw = pltpu.roll(vt, LANES // 2, axis=0)
        lo_src, hi_src = (vt, vsw) if half == 0 else (vsw, vt)
        vlo = jnp.where(drow < LANES // 2, lo_src, 0.0).astype(BF16)
        vhi = jnp.where(drow >= LANES // 2, hi_src, 0.0).astype(BF16)
        for c in range(rep // 2):
            col = (kv * (rep // 2) + c) * LANES
            qc = q_ref[:, col:col + LANES]
            out_t = None
            for kk, vv, head in ((klo, vlo, kv * rep + 2 * c), (khi, vhi, kv * rep + 2 * c + 1)):
                s = lax.dot_general(kk, qc, dn, preferred_element_type=F32) + maskadd
                sink = sink_ref[head] * LOG2_E
                m = jnp.maximum(jnp.max(s, axis=0, keepdims=True), sink)
                p = jnp.exp2(s - m)
                denom = jnp.sum(p, axis=0, keepdims=True) + jnp.exp2(sink - m)
                o = jnp.dot(vv, p.astype(BF16), preferred_element_type=F32) * (1.0 / denom)
                out_t = o if out_t is None else out_t + o
            o_ref[:, col:col + LANES] = out_t.T.astype(o_ref.dtype)


def swa_attention(qkv, sink, *, n_batch, seq, n_ctx, with_ctx_queries):
    blk = WINDOW
    nb = seq // blk
    nbc = n_ctx // blk if with_ctx_queries else 0
    n_q = SWA_HEADS * SWA_HEAD_DIM
    kcol, vcol = n_q // 256, n_q // 256 + 1
    lat_blocks = n_batch * nb
    ctx_row0 = n_batch * seq

    def qrow(b, i):
        return jnp.where(i < nb, b * nb + i, lat_blocks + b * (n_ctx // blk) + (i - nb))

    def win(delta, col):
        return pl.BlockSpec((blk, 256), lambda b, i, s: (b * nb + jnp.clip(i + delta, 0, nb - 1), col))

    def ctx(col):
        return pl.BlockSpec((n_ctx, 256), lambda b, i, s: (ctx_row0 // n_ctx + b, col))

    n_rows = n_batch * (seq + (n_ctx if with_ctx_queries else 0))
    grid_spec = pltpu.PrefetchScalarGridSpec(
        num_scalar_prefetch=1,
        grid=(n_batch, nb + nbc),
        in_specs=[pl.BlockSpec((blk, n_q), lambda b, i, s: (qrow(b, i), 0)),
                  win(-1, kcol), win(0, kcol), win(1, kcol), ctx(kcol),
                  win(-1, vcol), win(0, vcol), win(1, vcol), ctx(vcol)],
        out_specs=pl.BlockSpec((blk, n_q), lambda b, i, s: (qrow(b, i), 0)),
    )
    return pl.pallas_call(
        functools.partial(_swa_kernel, n_blocks=nb, blk=blk, n_ctx=n_ctx),
        grid_spec=grid_spec,
        out_shape=jax.ShapeDtypeStruct((n_rows, n_q), BF16),
        compiler_params=_cparams("parallel", "parallel"),
        name="swa_attention",
    )(sink, qkv, qkv, qkv, qkv, qkv, qkv, qkv, qkv, qkv)


def _ga_kernel(q_ref, kl_ref, vl_ref, kx_ref, vx_ref, o_ref, s_sc, vt_sc, vtx_sc, acc_sc, *, n_lat_blocks, tq, tk):
    i = pl.program_id(2)
    rep = GA_HEADS // GA_KV_HEADS
    hd = GA_HEAD_DIM
    n_ctx = kx_ref.shape[0]
    seq = kl_ref.shape[0]
    dn = (((1,), (1,)), ((), ()))

    @pl.when(i == 0)
    def _():
        vtx_sc[...] = vx_ref[...].astype(F32).T.astype(BF16)
        for c in range(seq // tk):
            vt_sc[c] = vl_ref[c * tk:(c + 1) * tk, :].astype(F32).T.astype(BF16)

    qs = jnp.concatenate([q_ref[:, r * hd:(r + 1) * hd] for r in range(rep)], axis=0)

    n_chunks = jnp.where(i < n_lat_blocks, seq // tk, 0)

    s = lax.dot_general(kx_ref[...], qs, dn, preferred_element_type=F32)
    s_sc[0:n_ctx, :] = s

    def scores(c, m):
        r0 = pl.multiple_of(c * tk, tk)
        s = lax.dot_general(kl_ref[pl.ds(r0, tk), :], qs, dn, preferred_element_type=F32)
        s_sc[pl.ds(n_ctx + r0, tk), :] = s
        return jnp.maximum(m, jnp.max(s, axis=0, keepdims=True))

    m = lax.fori_loop(0, n_chunks, scores, jnp.max(s, axis=0, keepdims=True))

    p = jnp.exp2(s_sc[0:n_ctx, :] - m)
    acc_sc[...] = jnp.dot(vtx_sc[...], p.astype(BF16), preferred_element_type=F32)

    def values(c, l):
        r0 = pl.multiple_of(n_ctx + c * tk, LANES)
        p = jnp.exp2(s_sc[pl.ds(r0, tk), :] - m)
        acc_sc[...] += jnp.dot(vt_sc[c], p.astype(BF16), preferred_element_type=F32)
        return l + jnp.sum(p, axis=0, keepdims=True)

    l = lax.fori_loop(0, n_chunks, values, jnp.sum(p, axis=0, keepdims=True))
    out = (acc_sc[...] * (1.0 / l)).T
    for r in range(rep):
        o_ref[:, r * hd:(r + 1) * hd] = out[r * tq:(r + 1) * tq].astype(o_ref.dtype)


GA_KEY_CHUNK = 2048


def global_attention(qkv, *, n_batch, seq, n_ctx, with_ctx_queries):
    tq, tk = 256, min(GA_KEY_CHUNK, seq)
    hd = GA_HEAD_DIM
    rep = GA_HEADS // GA_KV_HEADS
    n_q = GA_HEADS * hd
    nq = seq // tq
    nqc = n_ctx // tq if with_ctx_queries else 0
    kcol0, vcol0 = n_q // hd, n_q // hd + GA_KV_HEADS
    ctx_blk0 = n_batch * seq // n_ctx

    def qrow(b, i):
        return jnp.where(i < nq, b * nq + i, n_batch * nq + b * (n_ctx // tq) + (i - nq))

    n_rows = n_batch * (seq + (n_ctx if with_ctx_queries else 0))
    return pl.pallas_call(
        functools.partial(_ga_kernel, n_lat_blocks=nq, tq=tq, tk=tk),
        grid=(n_batch, GA_KV_HEADS, nq + nqc),
        in_specs=[pl.BlockSpec((tq, rep * hd), lambda b, k, i: (qrow(b, i), k)),
                  pl.BlockSpec((seq, hd), lambda b, k, i: (b, kcol0 + k)),
                  pl.BlockSpec((seq, hd), lambda b, k, i: (b, vcol0 + k)),
                  pl.BlockSpec((n_ctx, hd), lambda b, k, i: (ctx_blk0 + b, kcol0 + k)),
                  pl.BlockSpec((n_ctx, hd), lambda b, k, i: (ctx_blk0 + b, vcol0 + k))],
        out_specs=pl.BlockSpec((tq, rep * hd), lambda b, k, i: (qrow(b, i), k)),
        out_shape=jax.ShapeDtypeStruct((n_rows, n_q), BF16),
        scratch_shapes=[pltpu.VMEM((n_ctx + seq, rep * tq), F32), pltpu.VMEM((seq // tk, hd, tk), BF16),
                        pltpu.VMEM((hd, n_ctx), BF16), pltpu.VMEM((hd, rep * tq), F32)],
        compiler_params=_cparams("parallel", "parallel", "arbitrary"),
        name="global_attention",
    )(qkv, qkv, qkv, qkv, qkv)


def _ssd_inproj_kernel(x_ref, g_ref, sh_ref, sc_ref, w_ref, o_ref, dt_ref, h_ref, *, n_main_tiles):
    j = pl.program_id(1)

    @pl.when(j == 0)
    def _():
        h_ref[...] = _norm_modulate(x_ref, g_ref, sh_ref, sc_ref).astype(BF16)

    acc = jnp.dot(h_ref[...], w_ref[...], preferred_element_type=F32)

    @pl.when(j < n_main_tiles)
    def _():
        o_ref[...] = acc.astype(o_ref.dtype)

    @pl.when(j == n_main_tiles)
    def _():
        dt_ref[...] = acc[:, :LANES]


SSD_PROJ_COLS = 512


def ssd_in_project(x, g, shift, scale, w_pad, *, n_main, seq, n_batch):
    t, d = x.shape
    tm, tn = _proj_row_tile(seq, t - n_batch * seq), SSD_PROJ_COLS
    tiles_per_seq = seq // tm
    n_main_tiles = n_main // tn

    def mod_map(i, j):
        return (_group_of_tile(i, tiles_per_seq, n_batch), 0, 0)

    return pl.pallas_call(
        functools.partial(_ssd_inproj_kernel, n_main_tiles=n_main_tiles),
        grid=(t // tm, n_main_tiles + 1),
        in_specs=[pl.BlockSpec((tm, d), lambda i, j: (i, 0)),
                  pl.BlockSpec((1, d), lambda i, j: (0, 0)),
                  pl.BlockSpec((1, 1, d), mod_map),
                  pl.BlockSpec((1, 1, d), mod_map),
                  pl.BlockSpec((d, tn), lambda i, j: (0, j))],
        out_specs=[pl.BlockSpec((tm, tn), lambda i, j: (i, jnp.minimum(j, n_main_tiles - 1))),
                   pl.BlockSpec((tm, LANES), lambda i, j: (i, 0))],
        out_shape=[jax.ShapeDtypeStruct((t, n_main), BF16), jax.ShapeDtypeStruct((t, LANES), F32)],
        scratch_shapes=[pltpu.VMEM((tm, d), BF16)],
        compiler_params=_cparams("parallel", "arbitrary"),
        name="ssd_in_project",
    )(x, g, shift, scale, w_pad)


CONV_HALO = 8
CONV_ROWS = 256


def _conv_kernel(u_ref, w_ref, b_ref, o_ref, pad_ref, *, seq_len):
    tc = u_ref.shape[1]
    zeros = jnp.zeros((CONV_HALO, tc), F32)
    pad_ref[0:CONV_HALO, :] = zeros
    pad_ref[CONV_HALO + seq_len:2 * CONV_HALO + seq_len, :] = zeros
    for c in range(seq_len // CONV_ROWS):
        r0 = c * CONV_ROWS
        pad_ref[CONV_HALO + r0:CONV_HALO + r0 + CONV_ROWS, :] = u_ref[r0:r0 + CONV_ROWS, :].astype(F32)
    w = w_ref[...]
    b = b_ref[...]
    for c in range(seq_len // CONV_ROWS):
        r0 = c * CONV_ROWS
        acc = b
        for k in range(SSD_CONV):
            off = CONV_HALO + r0 + k - SSD_CONV // 2
            acc = acc + w[k:k + 1, :] * pad_ref[off:off + CONV_ROWS, :]
        o_ref[r0:r0 + CONV_ROWS, :] = (acc * jax.nn.sigmoid(acc)).astype(o_ref.dtype)


def ssd_conv(main, conv_w, conv_b, *, seq_len, n_seqs, row_block0, col0):
    n_ch = conv_w.shape[1]
    tc = 256
    return pl.pallas_call(
        functools.partial(_conv_kernel, seq_len=seq_len),
        grid=(n_seqs, n_ch // tc),
        in_specs=[pl.BlockSpec((seq_len, tc), lambda b, j: (row_block0 + b, col0 // tc + j)),
                  pl.BlockSpec((SSD_CONV, tc), lambda b, j: (0, j)),
                  pl.BlockSpec((1, tc), lambda b, j: (0, j))],
        out_specs=pl.BlockSpec((seq_len, tc), lambda b, j: (b, j)),
        out_shape=jax.ShapeDtypeStruct((n_seqs * seq_len, n_ch), BF16),
        scratch_shapes=[pltpu.VMEM((seq_len + 2 * CONV_HALO, tc), F32)],
        compiler_params=_cparams("parallel", "parallel"),
        name="ssd_conv",
    )(main, conv_w, conv_b.reshape(1, n_ch))


DIR_LANES = 16


def _prep_kernel(raw_ref, bias_ref, alog_ref, acs_o, eacs_o, dtT_o, acsT_o, wT_o, etot_o):
    ch = SSD_CHUNK
    row = lax.broadcasted_iota(I32, (ch, LANES), 0)
    lane = lax.broadcasted_iota(I32, (ch, LANES), 1)
    is_fwd = (lane % DIR_LANES) < DIR_LANES // 2
    neg_a = -jnp.exp(alog_ref[0])
    for c in range(raw_ref.shape[1] // ch):
        rows = slice(c * ch, (c + 1) * ch)
        v = raw_ref[0, rows, :] + bias_ref[0]
        dt = jnp.maximum(v, 0.0) + jnp.log1p(jnp.exp(-jnp.abs(v)))
        a = dt * neg_a
        fwd, rev = a, a
        s = 1
        while s < ch:
            fwd = fwd + jnp.where(row >= s, pltpu.roll(fwd, s, axis=0), 0.0)
            rev = rev + jnp.where(row < ch - s, pltpu.roll(rev, ch - s, axis=0), 0.0)
            s *= 2
        acs = jnp.where(is_fwd, fwd, rev)
        tot = jnp.where(is_fwd[0:1], fwd[ch - 1:ch, :], rev[0:1, :])
        acs_o[0, rows, :] = acs
        eacs_o[0, rows, :] = jnp.exp(acs)
        dtT_o[0, c] = dt.T[:DIR_LANES]
        acsT_o[0, c] = acs.T[:DIR_LANES]
        wT_o[0, c] = (jnp.exp(tot - acs) * dt).T[:DIR_LANES]
        etot_o[0, c] = jnp.broadcast_to(jnp.exp(tot), (8, LANES))


def ssd_prep(dt_g, bias_g, alog_g):
    n_g, t, _ = dt_g.shape
    ch = SSD_CHUNK
    nck_all = t // ch
    per_step = max(k for k in range(1, 9) if nck_all % k == 0)
    nck = nck_all // per_step
    vec = pl.BlockSpec((1, 1, LANES), lambda g, c: (g, 0, 0))
    rows = pl.BlockSpec((1, per_step * ch, LANES), lambda g, c: (g, c, 0))
    tr = pl.BlockSpec((1, per_step, DIR_LANES, LANES), lambda g, c: (g, c, 0, 0))
    return pl.pallas_call(
        _prep_kernel,
        grid=(n_g, nck),
        in_specs=[rows, vec, vec],
        out_specs=[rows, rows, tr, tr, tr, pl.BlockSpec((1, per_step, 8, LANES), lambda g, c: (g, c, 0, 0))],
        out_shape=[jax.ShapeDtypeStruct((n_g, t, LANES), F32), jax.ShapeDtypeStruct((n_g, t, LANES), F32),
                   jax.ShapeDtypeStruct((n_g, nck_all, DIR_LANES, LANES), F32),
                   jax.ShapeDtypeStruct((n_g, nck_all, DIR_LANES, LANES), F32),
                   jax.ShapeDtypeStruct((n_g, nck_all, DIR_LANES, LANES), F32),
                   jax.ShapeDtypeStruct((n_g, nck_all, 8, LANES), F32)],
        compiler_params=_cparams("parallel", "parallel"),
        name="ssd_prep",
    )(dt_g, bias_g, alog_g)


HEADS_PER_GROUP = 8
PAIRS_PER_GROUP = HEADS_PER_GROUP // 2
GROUP_COLS = HEADS_PER_GROUP * SSD_HEAD_DIM


def _scan_kernel(*refs, n_chunks, emit_state, alias_out):
    (x_ref, b_ref, c_ref, z_ref, acs_ref, eacs_ref, dtT_ref, acsT_ref, wT_ref, etot_ref,
     dskip_ref, ng_ref, init_ref) = refs[:13]
    rest = refs[13 + (1 if alias_out else 0):]
    y_ref = rest[0]
    fin_ref = rest[1] if emit_state else None
    ysc, st = rest[-2], rest[-1]
    ch = SSD_CHUNK
    st[...] = init_ref[0, 0]
    lo = lax.broadcasted_iota(I32, (1, LANES), 1) < LANES // 2
    li = lax.broadcasted_iota(I32, (ch, ch), 0)
    si = lax.broadcasted_iota(I32, (ch, ch), 1)
    dn = (((1,), (1,)), ((), ()))
    for d in range(2):
        mask = (si <= li) if d == 0 else (si >= li)

        def chunk(cc, carry, d=d, mask=mask):
            c = cc if d == 0 else n_chunks - 1 - cc
            rows = pl.ds(pl.multiple_of(c * ch, ch), ch)
            bc, cm = b_ref[rows, :], c_ref[rows, :]
            cb = lax.dot_general(cm, bc, dn, preferred_element_type=F32)
            bt = bc.astype(F32).T
            cf = cm.astype(F32)
            acs_blk = acs_ref[0, rows, :]
            eacs_blk = eacs_ref[0, rows, :]
            et = etot_ref[0, c]
            for pr in range(PAIRS_PER_GROUP):
                cols = slice(pr * LANES, (pr + 1) * LANES)
                x2 = x_ref[rows, cols]
                zero = jnp.zeros_like(x2)
                s2 = st[d * PAIRS_PER_GROUP + pr]
                r_lo = d * HEADS_PER_GROUP + 2 * pr
                y2 = jnp.zeros((ch, LANES), F32)
                snew = s2 * jnp.where(lo, et[0:1, r_lo:r_lo + 1], et[0:1, r_lo + 1:r_lo + 2])
                for hh in range(2):
                    r = r_lo + hh
                    keep = lo if hh == 0 else jnp.logical_not(lo)
                    xm = jnp.where(keep, x2, zero)
                    sm = jnp.where(keep, s2, 0.0).astype(BF16)
                    a_col = jnp.broadcast_to(acs_blk[:, r:r + 1], (ch, ch))
                    lm = jnp.exp(jnp.where(mask, a_col - acsT_ref[0, c, r:r + 1, :], NEG_BIG))
                    m_h = (cb * lm * dtT_ref[0, c, r:r + 1, :]).astype(BF16)
                    c_h = (cf * jnp.exp(a_col)).astype(BF16)
                    y2 = y2 + jnp.dot(m_h, xm, preferred_element_type=F32)
                    y2 = y2 + jnp.dot(c_h, sm, preferred_element_type=F32)
                    b_h = (bt * wT_ref[0, c, r:r + 1, :]).astype(BF16)
                    snew = snew + jnp.dot(b_h, xm, preferred_element_type=F32)
                st[d * PAIRS_PER_GROUP + pr] = snew
                if d == 0:
                    ysc[rows, cols] = y2
                else:
                    ysc[rows, cols] = ysc[rows, cols] + y2
            if d == 1:
                y = ysc[rows, :] + dskip_ref[0] * x_ref[rows, :].astype(F32)
                zf = z_ref[rows, :].astype(F32)
                y = y * (zf * jax.nn.sigmoid(zf))
                y = y * lax.rsqrt(jnp.mean(y * y, axis=-1, keepdims=True) + NORM_EPS) * ng_ref[0]
                y_ref[rows, :] = y.astype(y_ref.dtype)
            return carry

        lax.fori_loop(0, n_chunks, chunk, 0)
    if emit_state:
        fin_ref[0, 0] = st[...]


def ssd_scan(xbc, main, prep, dskip_g, normg_g, init_state, y_prev, *, seq_len, n_batch, row_block0,
             n_rows_total, emit_state):
    acs, eacs, dt_t, acs_t, w_t, etot = prep
    n_g = SSD_GROUPS
    nck = seq_len // SSD_CHUNK
    inner = n_g * GROUP_COLS
    bcol0 = inner // LANES
    ccol0 = bcol0 + n_g * SSD_STATE // LANES
    rb = row_block0

    def tr_spec(h):
        return pl.BlockSpec((1, nck, h, LANES), lambda b, g: (g, rb + b, 0, 0))

    vec = pl.BlockSpec((1, 1, GROUP_COLS), lambda b, g: (g, 0, 0))
    st_spec = pl.BlockSpec((1, 1, 2 * PAIRS_PER_GROUP, SSD_STATE, LANES), lambda b, g: (b, g, 0, 0, 0))
    in_specs = [pl.BlockSpec((seq_len, GROUP_COLS), lambda b, g: (b, g)),
                pl.BlockSpec((seq_len, LANES), lambda b, g: (b, bcol0 + g)),
                pl.BlockSpec((seq_len, LANES), lambda b, g: (b, ccol0 + g)),
                pl.BlockSpec((seq_len, GROUP_COLS), lambda b, g: (rb + b, g)),
                pl.BlockSpec((1, seq_len, LANES), lambda b, g: (g, rb + b, 0)),
                pl.BlockSpec((1, seq_len, LANES), lambda b, g: (g, rb + b, 0)),
                tr_spec(DIR_LANES), tr_spec(DIR_LANES), tr_spec(DIR_LANES), tr_spec(8),
                vec, vec, st_spec]
    args = [xbc, xbc, xbc, main, acs, eacs, dt_t, acs_t, w_t, etot, dskip_g, normg_g, init_state]
    aliases = {}
    if y_prev is not None:
        in_specs.append(pl.BlockSpec(memory_space=pl.ANY))
        args.append(y_prev)
        aliases = {len(args) - 1: 0}
    out_specs = [pl.BlockSpec((seq_len, GROUP_COLS), lambda b, g: (rb + b, g))]
    out_shape = [jax.ShapeDtypeStruct((n_rows_total, inner), BF16)]
    if emit_state:
        out_specs.append(st_spec)
        out_shape.append(jax.ShapeDtypeStruct(init_state.shape, F32))
    return pl.pallas_call(
        functools.partial(_scan_kernel, n_chunks=nck, emit_state=emit_state, alias_out=y_prev is not None),
        grid=(n_batch, n_g),
        in_specs=in_specs,
        out_specs=out_specs,
        out_shape=out_shape,
        scratch_shapes=[pltpu.VMEM((seq_len, GROUP_COLS), F32),
                        pltpu.VMEM((2 * PAIRS_PER_GROUP, SSD_STATE, LANES), F32)],
        input_output_aliases=aliases,
        compiler_params=_cparams("parallel", "parallel"),
        name="ssd_scan",
    )(*args)


def _router_kernel(x_ref, g_ref, sh_ref, sc_ref, whi_ref, wlo_ref, b_ref, tri_ref,
                   h_hbm, meta_o, gate_o, cnt_o, carry, hbuf, hsem, *, n_tiles):
    i = pl.program_id(0)
    slot = i % 2
    tm = x_ref.shape[0]

    @pl.when(i == 0)
    def _():
        carry[...] = jnp.zeros_like(carry)

    def h_copies(tile, sl):
        rows = pl.ds(tile * tm, tm)
        return [pltpu.make_async_copy(hbuf.at[sl, :, c * LANES:(c + 1) * LANES], h_hbm.at[rows, c, :], hsem.at[sl])
                for c in range(ROW_SLABS)]

    h = _norm_modulate(x_ref, g_ref, sh_ref, sc_ref)
    hbuf[slot] = h

    @pl.when(i >= 1)
    def _():
        for cp in h_copies(i - 1, 1 - slot):
            cp.wait()

    for cp in h_copies(i, slot):
        cp.start()

    @pl.when(i == n_tiles - 1)
    def _():
        for cp in h_copies(i, slot):
            cp.wait()

    h_hi = h.astype(BF16)
    h_lo = (h - h_hi.astype(F32)).astype(BF16)
    whi = whi_ref[...]
    logits = (jnp.dot(h_hi, whi, preferred_element_type=F32) + jnp.dot(h_lo, whi, preferred_element_type=F32)
              + jnp.dot(h_hi, wlo_ref[...], preferred_element_type=F32) + b_ref[...])
    tm = logits.shape[0]
    lane = lax.broadcasted_iota(I32, (tm, LANES), 1).astype(F32)
    work = logits
    vals, idxs = [], []
    for _ in range(TOP_K):
        m = jnp.max(work, axis=1, keepdims=True)
        idx = jnp.min(jnp.where(work == m, lane, float(LANES)), axis=1, keepdims=True)
        vals.append(m)
        idxs.append(idx)
        work = jnp.where(lane == idx, -jnp.inf, work)
    es = [jnp.exp(v - vals[0]) for v in vals]
    inv = 1.0 / (es[0] + es[1] + es[2] + es[3])
    multihot = sum(jnp.where(lane == idx, 1.0, 0.0) for idx in idxs)
    before = jnp.dot(tri_ref[...], multihot.astype(BF16), preferred_element_type=F32) + carry[...]
    meta = jnp.zeros((tm, LANES), F32)
    gates = jnp.zeros((tm, LANES), F32)
    for k in range(TOP_K):
        rank = jnp.sum(jnp.where(lane == idxs[k], before, 0.0), axis=1, keepdims=True)
        meta = meta + jnp.where(lane == float(k), idxs[k], 0.0) + jnp.where(lane == float(TOP_K + k), rank, 0.0)
        gates = gates + jnp.where(lane == float(k), es[k] * inv, 0.0)
    meta_o[...] = meta.astype(I32)
    gate_o[...] = gates
    carry[...] = carry[...] + jnp.sum(multihot, axis=0, keepdims=True)
    cnt_o[...] = carry[...]


def moe_router(x, g, shift, scale, w_hi, w_lo, b_pad, *, n_rows, seq, n_batch):
    d = x.shape[1]
    tm = ROW_TILE
    tiles_per_seq = seq // tm
    tri = jnp.asarray(np.tril(np.ones((tm, tm), np.float32), -1), dtype=BF16)

    def mod_map(i):
        return (_group_of_tile(i, tiles_per_seq, n_batch), 0, 0)

    full = lambda shape: pl.BlockSpec(shape, lambda i: (0,) * len(shape))
    return pl.pallas_call(
        functools.partial(_router_kernel, n_tiles=n_rows // tm),
        grid=(n_rows // tm,),
        in_specs=[pl.BlockSpec((tm, d), lambda i: (i, 0)), full((1, d)),
                  pl.BlockSpec((1, 1, d), mod_map), pl.BlockSpec((1, 1, d), mod_map),
                  full((d, LANES)), full((d, LANES)), full((1, LANES)), full((tm, tm))],
        out_specs=[pl.BlockSpec(memory_space=pl.ANY),
                   pl.BlockSpec((tm, LANES), lambda i: (i, 0)),
                   pl.BlockSpec((tm, LANES), lambda i: (i, 0)), full((1, LANES))],
        out_shape=[jax.ShapeDtypeStruct((n_rows, ROW_SLABS, LANES), F32),
                   jax.ShapeDtypeStruct((n_rows, LANES), I32),
                   jax.ShapeDtypeStruct((n_rows, LANES), F32), jax.ShapeDtypeStruct((1, LANES), F32)],
        scratch_shapes=[pltpu.VMEM((1, LANES), F32), pltpu.VMEM((2, tm, d), F32), pltpu.SemaphoreType.DMA((2,))],
        compiler_params=_cparams("arbitrary"),
        name="moe_router",
    )(x, g, shift, scale, w_hi, w_lo, b_pad, tri)


ROW_SLABS = 16
DMA_UNROLL = 8
DISPATCH_TILE = 256
FF_CHUNK = 256
OUT_CHUNK = 512


def _rows_to_slabs(ref, val):
    for c in range(ROW_SLABS):
        ref[:, c, :] = val[:, c * LANES:(c + 1) * LANES]


def _slabs_to_rows(ref, rows):
    return jnp.concatenate([ref[rows, c, :] for c in range(ROW_SLABS)], axis=1)


def _wait_rows(n, like_hbm, sem):
    pltpu.make_async_copy(like_hbm.at[pl.ds(0, n)], like_hbm.at[pl.ds(0, n)], sem).wait()


def _dispatch_kernel(tail_start_ref, tail_len_ref, nact_ref, dst_ref, h_ref, xs_hbm, zbuf, sem, zsem,
                     *, n_exp, n_blocks):
    s = pl.program_id(0)
    tm = h_ref.shape[0]

    @pl.when(s == 0)
    def _():
        zbuf[...] = jnp.zeros_like(zbuf)
        half = MOE_BLOCK // 2

        def unused(b, wait):
            for part in range(2):
                cp = pltpu.make_async_copy(zbuf, xs_hbm.at[pl.ds(b * MOE_BLOCK + part * half, half)], zsem)
                cp.wait() if wait else cp.start()

        lax.fori_loop(nact_ref[0], n_blocks, lambda b, c: (unused(b, False), c)[1], 0)
        lax.fori_loop(nact_ref[0], n_blocks, lambda b, c: (unused(b, True), c)[1], 0)
        sizes = [1 << b for b in range(MOE_BLOCK.bit_length() - 1)]

        def tails(e, wait):
            start, length = tail_start_ref[e], tail_len_ref[e]
            for size in sizes:
                @pl.when((length & size) != 0)
                def _(size=size):
                    pos = start + (length & ~(2 * size - 1))
                    cp = pltpu.make_async_copy(zbuf.at[pl.ds(0, size)], xs_hbm.at[pl.ds(pos, size)], zsem)
                    cp.wait() if wait else cp.start()

        lax.fori_loop(0, n_exp, lambda e, c: (tails(e, False), c)[1], 0)
        lax.fori_loop(0, n_exp, lambda e, c: (tails(e, True), c)[1], 0)

    def issue(it, c):
        for u in range(DMA_UNROLL):
            tok = it * (DMA_UNROLL // TOP_K) + u // TOP_K
            pltpu.make_async_copy(h_ref.at[pl.ds(tok, 1)], xs_hbm.at[pl.ds(dst_ref[0, 0, it * DMA_UNROLL + u], 1)],
                                  sem).start(priority=u % 2)
        return c

    lax.fori_loop(0, TOP_K * tm // DMA_UNROLL, issue, 0)
    _wait_rows(TOP_K * tm, xs_hbm, sem)


def moe_dispatch(tail_start, tail_len, n_active, dest, h3, *, n_blocks):
    n_rows = h3.shape[0]
    tm = DISPATCH_TILE
    n_exp = tail_start.shape[0]
    grid_spec = pltpu.PrefetchScalarGridSpec(
        num_scalar_prefetch=3,
        grid=(n_rows // tm,),
        in_specs=[pl.BlockSpec((1, 1, TOP_K * tm), lambda s, a, b, c: (s, 0, 0), memory_space=pltpu.SMEM),
                  pl.BlockSpec((tm, ROW_SLABS, LANES), lambda s, a, b, c: (s, 0, 0))],
        out_specs=pl.BlockSpec(memory_space=pl.ANY),
        scratch_shapes=[pltpu.VMEM((MOE_BLOCK // 2, ROW_SLABS, LANES), F32),
                        pltpu.SemaphoreType.DMA(()), pltpu.SemaphoreType.DMA(())],
    )
    return pl.pallas_call(
        functools.partial(_dispatch_kernel, n_exp=n_exp, n_blocks=n_blocks),
        grid_spec=grid_spec,
        out_shape=jax.ShapeDtypeStruct((n_blocks * MOE_BLOCK, ROW_SLABS, LANES), F32),
        compiler_params=_cparams("arbitrary"),
        name="moe_dispatch",
    )(tail_start, tail_len, n_active, dest.reshape(n_rows // tm, 1, TOP_K * tm), h3)


def _ffn_kernel(bexp_ref, nact_ref, xs_hbm, wgu_ref, bgu_ref, wd_ref, bd_ref, o_hbm, xbuf, ybuf, sem, osem):
    s = pl.program_id(0)
    n_steps = pl.num_programs(0)
    slot = s % 2
    nact = nact_ref[0]

    def block_copies(blk, sl):
        rows = pl.ds(blk * MOE_BLOCK, MOE_BLOCK)
        return [pltpu.make_async_copy(xs_hbm.at[rows, c, :], xbuf.at[sl, :, c * LANES:(c + 1) * LANES], sem.at[sl])
                for c in range(ROW_SLABS)]

    def out_copies(blk, sl):
        rows = pl.ds(blk * MOE_BLOCK, MOE_BLOCK)
        return [pltpu.make_async_copy(ybuf.at[sl, :, c * LANES:(c + 1) * LANES], o_hbm.at[rows, c, :], osem.at[sl])
                for c in range(ROW_SLABS)]

    @pl.when(jnp.logical_and(s == 0, nact > 0))
    def _():
        for cp in block_copies(0, 0):
            cp.start()

    @pl.when(s + 1 < nact)
    def _():
        for cp in block_copies(s + 1, 1 - slot):
            cp.start()

    @pl.when(s < nact)
    def _():
        for cp in block_copies(s, slot):
            cp.wait()
        x = xbuf[slot].astype(BF16)
        ff = EXPERT_FF

        def swiglu(glu, lin):
            glu = jnp.minimum(glu, SWIGLU_LIMIT)
            lin = jnp.clip(lin, -SWIGLU_LIMIT, SWIGLU_LIMIT)
            return (glu * jax.nn.sigmoid(SWIGLU_ALPHA * glu) * (lin + 1.0)).astype(BF16)

        acts = []
        full = ff // FF_CHUNK * FF_CHUNK
        for lo in range(0, full, FF_CHUNK):
            hi = lo + FF_CHUNK
            glu = jnp.dot(x, wgu_ref[0, 0, :, lo:hi], preferred_element_type=F32) + bgu_ref[0, 0, :, lo:hi]
            lin = (jnp.dot(x, wgu_ref[0, 0, :, ff + lo:ff + hi], preferred_element_type=F32)
                   + bgu_ref[0, 0, :, ff + lo:ff + hi])
            acts.append(swiglu(glu, lin))
        if full < ff:
            rem = ff - full
            w_rem = jnp.concatenate([wgu_ref[0, 0, :, full:ff], wgu_ref[0, 0, :, ff + full:2 * ff]], axis=1)
            b_rem = jnp.concatenate([bgu_ref[0, 0, :, full:ff], bgu_ref[0, 0, :, ff + full:2 * ff]], axis=1)
            gl = jnp.dot(x, w_rem, preferred_element_type=F32) + b_rem
            acts.append(swiglu(gl[:, :rem], gl[:, rem:]))
        act = jnp.concatenate(acts, axis=1)
        for n0 in range(0, ybuf.shape[2], OUT_CHUNK):
            y = jnp.dot(act, wd_ref[0, 0, :, n0:n0 + OUT_CHUNK], preferred_element_type=F32)
            ybuf[slot, :, n0:n0 + OUT_CHUNK] = y + bd_ref[0, 0, :, n0:n0 + OUT_CHUNK]

    @pl.when(s >= nact)
    def _():
        ybuf[slot] = jnp.zeros(ybuf.shape[1:], F32)

    @pl.when(s >= 1)
    def _():
        for cp in out_copies(s - 1, 1 - slot):
            cp.wait()

    for cp in out_copies(s, slot):
        cp.start()

    @pl.when(s == n_steps - 1)
    def _():
        for cp in out_copies(s, slot):
            cp.wait()


def moe_ffn_blocks(block_exp, n_active, xs, w_gu, b_gu, w_down, b_down, *, layer):
    n_blocks = xs.shape[0] // MOE_BLOCK
    d, ff2 = w_gu.shape[2], w_gu.shape[3]
    wmap = lambda s, be, na: (layer, be[s], 0, 0)
    grid_spec = pltpu.PrefetchScalarGridSpec(
        num_scalar_prefetch=2,
        grid=(n_blocks,),
        in_specs=[pl.BlockSpec(memory_space=pl.ANY),
                  pl.BlockSpec((1, 1, d, ff2), wmap), pl.BlockSpec((1, 1, 1, ff2), wmap),
                  pl.BlockSpec((1, 1, ff2 // 2, d), wmap), pl.BlockSpec((1, 1, 1, d), wmap)],
        out_specs=pl.BlockSpec(memory_space=pl.ANY),
        scratch_shapes=[pltpu.VMEM((2, MOE_BLOCK, d), F32), pltpu.VMEM((2, MOE_BLOCK, d), F32),
                        pltpu.SemaphoreType.DMA((2,)), pltpu.SemaphoreType.DMA((2,))],
    )
    return pl.pallas_call(
        _ffn_kernel,
        grid_spec=grid_spec,
        out_shape=jax.ShapeDtypeStruct(xs.shape, F32),
        compiler_params=_cparams("arbitrary"),
        name="moe_ffn",
    )(block_exp, n_active, xs, w_gu, b_gu, w_down, b_down)


COMBINE_TILE = 128


def _combine_kernel(d0_ref, d1_ref, g_ref, rows_hbm, x_hbm, mod_ref, o_hbm, gbuf, xbuf, obuf, gsem, xsem, osem,
                    *, n_tiles):
    s = pl.program_id(0)
    slot = s % 2
    tm = COMBINE_TILE

    def gather(idx_ref, sl):
        def issue(it, c):
            for u in range(DMA_UNROLL):
                r = it * DMA_UNROLL + u
                pltpu.make_async_copy(rows_hbm.at[pl.ds(idx_ref[0, 0, r], 1)], gbuf.at[sl, pl.ds(r, 1)],
                                      gsem.at[sl]).start(priority=u % 2)
            return c

        lax.fori_loop(0, TOP_K * tm // DMA_UNROLL, issue, 0)

    def x_copies(tile, sl):
        rows = pl.ds(tile * tm, tm)
        return [pltpu.make_async_copy(x_hbm.at[rows, c * LANES:(c + 1) * LANES], xbuf.at[sl, :, c, :], xsem.at[sl])
                for c in range(ROW_SLABS)]

    def out_copies(tile, sl):
        rows = pl.ds(tile * tm, tm)
        return [pltpu.make_async_copy(obuf.at[sl, :, c, :], o_hbm.at[rows, c * LANES:(c + 1) * LANES], osem.at[sl])
                for c in range(ROW_SLABS)]

    @pl.when(s == 0)
    def _():
        gather(d0_ref, 0)
        for cp in x_copies(0, 0):
            cp.start()

    @pl.when(s + 1 < n_tiles)
    def _():
        gather(d1_ref, 1 - slot)
        for cp in x_copies(s + 1, 1 - slot):
            cp.start()

    _wait_rows(TOP_K * tm, rows_hbm, gsem.at[slot])
    for cp in x_copies(s, slot):
        cp.wait()
    mod = mod_ref[0]

    def token(t, c):
        acc = g_ref[0, 0, TOP_K * t] * gbuf[slot, t]
        for k in range(1, TOP_K):
            acc = acc + g_ref[0, 0, TOP_K * t + k] * gbuf[slot, k * tm + t]
        obuf[slot, t] = xbuf[slot, t] + mod * acc
        return c

    lax.fori_loop(0, tm, token, 0, unroll=4)

    @pl.when(s >= 1)
    def _():
        for cp in out_copies(s - 1, 1 - slot):
            cp.wait()

    for cp in out_copies(s, slot):
        cp.start()

    @pl.when(s == n_tiles - 1)
    def _():
        for cp in out_copies(s, slot):
            cp.wait()


def moe_combine(dest_tiles, out_rows, x, gates, gate_mod, *, n_rows, seq, n_batch):
    d = x.shape[1]
    tm = COMBINE_TILE
    n_tiles = n_rows // tm
    tiles_per_seq = seq // tm
    gate_tiles = gates[:n_rows, :TOP_K].reshape(n_tiles, 1, TOP_K * tm)
    mod_slabs = gate_mod.reshape(gate_mod.shape[0], ROW_SLABS, LANES)

    def smem_spec(ahead):
        return pl.BlockSpec((1, 1, TOP_K * tm), lambda s: (jnp.minimum(s + ahead, n_tiles - 1), 0, 0),
                            memory_space=pltpu.SMEM)

    slab_buf = pltpu.VMEM((2, tm, ROW_SLABS, LANES), F32)
    return pl.pallas_call(
        functools.partial(_combine_kernel, n_tiles=n_tiles),
        grid=(n_tiles,),
        in_specs=[smem_spec(0), smem_spec(1), smem_spec(0),
                  pl.BlockSpec(memory_space=pl.ANY), pl.BlockSpec(memory_space=pl.ANY),
                  pl.BlockSpec((1, ROW_SLABS, LANES), lambda s: (_group_of_tile(s, tiles_per_seq, n_batch), 0, 0))],
        out_specs=pl.BlockSpec(memory_space=pl.ANY),
        out_shape=jax.ShapeDtypeStruct((n_rows, d), F32),
        scratch_shapes=[pltpu.VMEM((2, TOP_K * tm, ROW_SLABS, LANES), F32), slab_buf, slab_buf,
                        pltpu.SemaphoreType.DMA((2,)), pltpu.SemaphoreType.DMA((2,)), pltpu.SemaphoreType.DMA((2,))],
        compiler_params=_cparams("arbitrary"),
        name="moe_combine",
    )(dest_tiles, dest_tiles, gate_tiles, out_rows, x, mod_slabs)


def moe_layer(x, g, shift, scale, gate_mod, w_router, b_router, w_gu, b_gu, w_down, b_down,
              *, layer, n_rows, seq, n_batch):
    n_exp = w_router.shape[1]
    w_pad = jnp.pad(w_router, ((0, 0), (0, LANES - n_exp)))
    w_hi = w_pad.astype(BF16)
    w_lo = (w_pad - w_hi.astype(F32)).astype(BF16)
    b_pad = jnp.concatenate([b_router, jnp.full((LANES - n_exp,), NEG_BIG, F32)])[None, :]
    h3, meta, gates, cnt = moe_router(x, g, shift, scale, w_hi, w_lo, b_pad, n_rows=n_rows, seq=seq, n_batch=n_batch)
    top_idx, rank = meta[:, :TOP_K], meta[:, TOP_K:2 * TOP_K]
    counts = cnt[0, :n_exp].astype(I32)
    padded = (counts + MOE_BLOCK - 1) // MOE_BLOCK * MOE_BLOCK
    pad_end = jnp.cumsum(padded)
    pad_start = pad_end - padded
    onehot = top_idx[:, :, None] == jnp.arange(n_exp, dtype=I32)[None, None, :]
    dest = jnp.sum(jnp.where(onehot, pad_start[None, None, :], 0), axis=-1) + rank
    n_blocks = n_rows * TOP_K // MOE_BLOCK + n_exp
    block_row0 = jnp.arange(n_blocks, dtype=I32) * MOE_BLOCK
    block_exp = jnp.minimum(jnp.sum((pad_end[None, :] <= block_row0[:, None]).astype(I32), axis=1), n_exp - 1)
    n_active = (pad_end[-1:] // MOE_BLOCK).astype(I32)
    xs = moe_dispatch(pad_start + counts, padded - counts, n_active, dest, h3, n_blocks=n_blocks)
    out_rows = moe_ffn_blocks(block_exp, n_active, xs, w_gu, b_gu, w_down, b_down, layer=layer)
    tm = COMBINE_TILE
    dest_tiles = dest.reshape(n_rows // tm, tm, TOP_K).transpose(0, 2, 1).reshape(n_rows // tm, 1, TOP_K * tm)
    return moe_combine(dest_tiles, out_rows, x, gates, gate_mod, n_rows=n_rows, seq=seq, n_batch=n_batch)


def attention_layer(x, g, shift, scale, gate_mod, w_in, q_gain, k_gain, w_out, sink, *, n_heads, n_kv, head_dim,
                    seq, n_ctx, n_batch, need_ctx):
    n_qk = (n_heads + n_kv) * head_dim
    q_scale = head_dim ** -0.5 * LOG2_E
    gain_row = jnp.concatenate([jnp.tile(q_gain, n_heads) * q_scale, jnp.tile(k_gain, n_kv),
                                jnp.ones((n_kv * head_dim,), F32)])[None, :]
    qkv = qkv_project(x, g, shift, scale, w_in.astype(BF16), gain_row, head_group_matrix(256, head_dim),
                      rope_tables(seq, head_dim, _proj_row_tile(seq, x.shape[0] - n_batch * seq)),
                      n_qk_cols=n_qk, head_dim=head_dim, seq=seq, n_batch=n_batch)
    if sink is not None:
        att = swa_attention(qkv, sink, n_batch=n_batch, seq=seq, n_ctx=n_ctx, with_ctx_queries=need_ctx)
    else:
        att = global_attention(qkv, n_batch=n_batch, seq=seq, n_ctx=n_ctx, with_ctx_queries=need_ctx)
    n_rows = n_batch * (seq + (n_ctx if need_ctx else 0))
    return out_project(att, w_out.astype(BF16), x, gate_mod, n_rows=n_rows, seq=seq, n_batch=n_batch)


def ssd_layer(x, g, shift, scale, gate_mod, w_in, conv_w, conv_b, dt_bias, a_log, d_skip, norm_g, w_out,
              *, seq, n_ctx, n_batch):
    t, d = x.shape
    n_g = SSD_GROUPS
    n_heads = d_skip.shape[0]
    inner = n_heads * SSD_HEAD_DIM
    n_main = inner + conv_w.shape[1]
    w_pad = jnp.pad(w_in, ((0, 0), (0, n_main + SSD_PROJ_COLS - w_in.shape[1]))).astype(BF16)
    main, dt_raw = ssd_in_project(x, g, shift, scale, w_pad, n_main=n_main, seq=seq, n_batch=n_batch)

    def to_groups(a):
        lead = a.shape[:-1]
        a = a.reshape(lead + (2, n_g, HEADS_PER_GROUP))
        a = jnp.moveaxis(a, -2, 0).reshape((n_g,) + lead + (DIR_LANES,))
        return jnp.pad(a, [(0, 0)] * (a.ndim - 1) + [(0, LANES - DIR_LANES)])

    prep = ssd_prep(to_groups(dt_raw[:, :2 * n_heads]), to_groups(dt_bias.reshape(1, -1)),
                    to_groups(a_log.reshape(1, -1)))
    dskip_g = jnp.repeat(d_skip, SSD_HEAD_DIM).reshape(n_g, 1, GROUP_COLS)
    normg_g = norm_g.reshape(n_g, 1, GROUP_COLS)
    n_lat = n_batch * seq
    xbc_c = ssd_conv(main, conv_w, conv_b, seq_len=n_ctx, n_seqs=n_batch, row_block0=n_lat // n_ctx, col0=inner)
    xbc_l = ssd_conv(main, conv_w, conv_b, seq_len=seq, n_seqs=n_batch, row_block0=0, col0=inner)
    zero_state = jnp.zeros((n_batch, n_g, 2 * PAIRS_PER_GROUP, SSD_STATE, LANES), F32)
    y = jnp.zeros((t, inner), BF16)
    y, ctx_state = ssd_scan(xbc_c, main, prep, dskip_g, normg_g, zero_state, y, seq_len=n_ctx, n_batch=n_batch,
                            row_block0=n_lat // n_ctx, n_rows_total=t, emit_state=True)
    (y,) = ssd_scan(xbc_l, main, prep, dskip_g, normg_g, ctx_state, y, seq_len=seq, n_batch=n_batch,
                    row_block0=0, n_rows_total=t, emit_state=False)
    return out_project(y, w_out.astype(BF16), x, gate_mod, n_rows=t, seq=seq, n_batch=n_batch)


def kernel(x, c, ctx, c_ctx, ada_w, ada_b, norm_mix, norm_ffn, swa_w_in, swa_q_norm, swa_k_norm, swa_sink,
           swa_w_out, ssd_w_in, ssd_conv_w, ssd_conv_b, ssd_dt_bias, ssd_a_log, ssd_d, ssd_norm, ssd_w_out,
           ga_w_in, ga_q_norm, ga_k_norm, ga_w_out, moe_w_router, moe_b_router, moe_w_gate_up, moe_b_gate_up,
           moe_w_down, moe_b_down):
    n_batch, seq, d = x.shape
    n_ctx = ctx.shape[1]
    n_lat = n_batch * seq
    xs = jnp.concatenate([x.reshape(n_lat, d), ctx.reshape(n_batch * n_ctx, d)], axis=0)
    cvec = jnp.concatenate([c, c_ctx[None, :], jnp.zeros((8 - n_batch - 1, d), F32)], axis=0)
    mods = adaln_all(cvec, ada_w, ada_b)[:, :n_batch + 1].reshape(DEPTH, n_batch + 1, 6, 1, d)
    dims = dict(seq=seq, n_ctx=n_ctx, n_batch=n_batch)
    w_gu_bf, w_down_bf = moe_w_gate_up.astype(BF16), moe_w_down.astype(BF16)
    for i in range(DEPTH):
        kind, j = i % N_MIXERS, i // N_MIXERS
        need_ctx = i < DEPTH - 1
        m = [mods[i, :, k] for k in range(6)]
        g_mix = norm_mix[i][None, :]
        if kind == 0:
            xs = attention_layer(xs, g_mix, m[0], m[1], m[2], swa_w_in[j], swa_q_norm[j], swa_k_norm[j],
                                 swa_w_out[j], swa_sink[j], n_heads=SWA_HEADS, n_kv=SWA_KV_HEADS,
                                 head_dim=SWA_HEAD_DIM, need_ctx=need_ctx, **dims)
        elif kind == 1:
            xs = ssd_layer(xs, g_mix, m[0], m[1], m[2], ssd_w_in[j], ssd_conv_w[j], ssd_conv_b[j], ssd_dt_bias[j],
                           ssd_a_log[j], ssd_d[j], ssd_norm[j], ssd_w_out[j], **dims)
        else:
            xs = attention_layer(xs, g_mix, m[0], m[1], m[2], ga_w_in[j], ga_q_norm[j], ga_k_norm[j], ga_w_out[j],
                                 None, n_heads=GA_HEADS, n_kv=GA_KV_HEADS, head_dim=GA_HEAD_DIM,
                                 need_ctx=need_ctx, **dims)
        n_rows = n_batch * (seq + (n_ctx if need_ctx else 0))
        xs = moe_layer(xs, norm_ffn[i][None, :], m[3], m[4], m[5], moe_w_router[i], moe_b_router[i],
                       w_gu_bf, moe_b_gate_up[:, :, None, :], w_down_bf, moe_b_down[:, :, None, :],
                       layer=i, n_rows=n_rows, seq=seq, n_batch=n_batch)
    return xs[:n_lat].reshape(n_batch, seq, d)
```

```python
import functools
import math

import jax
import jax.numpy as jnp
import numpy as np
from jax import lax
from jax.experimental import pallas as pl
from jax.experimental.pallas import tpu as pltpu

F32 = jnp.float32
BF16 = jnp.bfloat16
I32 = jnp.int32

DEPTH = 4
N_MIXERS = 3
GRID_W = 64
NORM_EPS = 1e-6
ROPE_THETA = 10000.0

SWA_HEADS, SWA_KV_HEADS, SWA_HEAD_DIM, WINDOW = 32, 4, 64, 128
SSD_HEAD_DIM, SSD_GROUPS, SSD_STATE, SSD_CONV, SSD_CHUNK = 64, 8, 128, 5, 128
GA_HEADS, GA_KV_HEADS, GA_HEAD_DIM = 16, 4, 128
N_EXPERTS, TOP_K, EXPERT_FF = 32, 4, 896
SWIGLU_ALPHA, SWIGLU_LIMIT = 1.702, 7.0
MOE_BLOCK = 256

LANES = 128
ROW_TILE = 512
NEG_BIG = -1e30
LOG2_E = math.log2(math.e)
VMEM_LIMIT = 56 * 1024 * 1024


def _cparams(*sem):
    return pltpu.CompilerParams(dimension_semantics=sem, vmem_limit_bytes=VMEM_LIMIT)


def _adaln_kernel(c_ref, w_ref, b_ref, o_ref):
    a = c_ref[...]
    a = (a * jax.nn.sigmoid(a)).astype(BF16)
    o_ref[0] = jnp.dot(a, w_ref[0].astype(BF16), preferred_element_type=F32) + b_ref[0]


def adaln_all(cvec, ada_w, ada_b):
    n_layers, d, n = ada_w.shape
    tn = 1024
    return pl.pallas_call(
        _adaln_kernel,
        grid=(n_layers, n // tn),
        in_specs=[pl.BlockSpec((8, d), lambda l, j: (0, 0)),
                  pl.BlockSpec((1, d, tn), lambda l, j: (l, 0, j)),
                  pl.BlockSpec((1, 1, tn), lambda l, j: (l, 0, j))],
        out_specs=pl.BlockSpec((1, 8, tn), lambda l, j: (l, 0, j)),
        out_shape=jax.ShapeDtypeStruct((n_layers, 8, n), F32),
        compiler_params=_cparams("parallel", "parallel"),
        name="adaln",
    )(cvec, ada_w, ada_b.reshape(n_layers, 1, n))


def _norm_modulate(x_ref, g_ref, sh_ref, sc_ref):
    x = x_ref[...]
    ms = jnp.mean(x * x, axis=-1, keepdims=True)
    y = x * lax.rsqrt(ms + NORM_EPS) * g_ref[...]
    return y * (1.0 + sc_ref[0]) + sh_ref[0]


def _group_of_tile(i, tiles_per_seq, n_batch):
    return jnp.minimum(i // tiles_per_seq, n_batch)


def _qkv_kernel(x_ref, g_ref, sh_ref, sc_ref, w_ref, gain_ref, gmat_ref, cos_ref, sa_ref, sb_ref,
                o_ref, h_ref, *, n_qk_tiles, head_dim):
    j = pl.program_id(1)

    @pl.when(j == 0)
    def _():
        h_ref[...] = _norm_modulate(x_ref, g_ref, sh_ref, sc_ref).astype(BF16)

    sub = gmat_ref.shape[0]
    q4 = head_dim // 4
    for t in range(w_ref.shape[1] // sub):
        cols = slice(t * sub, (t + 1) * sub)
        acc = jnp.dot(h_ref[...], w_ref[:, cols], preferred_element_type=F32)
        sub_tile = j * (w_ref.shape[1] // sub) + t

        @pl.when(sub_tile < n_qk_tiles)
        def _(acc=acc, cols=cols):
            ss = jnp.dot((acc * acc).astype(BF16), gmat_ref[...], preferred_element_type=F32)
            yn = acc * lax.rsqrt(ss * (1.0 / head_dim) + NORM_EPS) * gain_ref[:, cols]
            cos, sa, sb = cos_ref[...], sa_ref[...], sb_ref[...]
            for c in range(sub // LANES):
                yc = yn[:, c * LANES:(c + 1) * LANES]
                up = pltpu.roll(yc, LANES - q4, axis=1)
                dn = pltpu.roll(yc, q4, axis=1)
                lanes = slice(cols.start + c * LANES, cols.start + (c + 1) * LANES)
                o_ref[:, lanes] = (yc * cos + up * sa + dn * sb).astype(o_ref.dtype)

        @pl.when(sub_tile >= n_qk_tiles)
        def _(acc=acc, cols=cols):
            o_ref[:, cols] = acc.astype(o_ref.dtype)


def _proj_row_tile(seq, n_ctx_rows):
    return next(tm for tm in (1024, 512) if seq % tm == 0 and n_ctx_rows % tm == 0)


def qkv_project(x, g, shift, scale, w, gain_row, gmat, rope_tabs, *, n_qk_cols, head_dim, seq, n_batch):
    t, d = x.shape
    n = w.shape[1]
    tm, tn = _proj_row_tile(seq, t - n_batch * seq), 512
    sub = gmat.shape[0]
    tiles_per_seq = seq // tm
    n_lat_tiles = n_batch * tiles_per_seq

    def tab_map(i, j):
        return (jnp.where(i < n_lat_tiles, i % tiles_per_seq, tiles_per_seq), 0)

    def mod_map(i, j):
        return (_group_of_tile(i, tiles_per_seq, n_batch), 0, 0)

    tab_spec = pl.BlockSpec((tm, LANES), tab_map)
    return pl.pallas_call(
        functools.partial(_qkv_kernel, n_qk_tiles=n_qk_cols // sub, head_dim=head_dim),
        grid=(t // tm, n // tn),
        in_specs=[pl.BlockSpec((tm, d), lambda i, j: (i, 0)),
                  pl.BlockSpec((1, d), lambda i, j: (0, 0)),
                  pl.BlockSpec((1, 1, d), mod_map),
                  pl.BlockSpec((1, 1, d), mod_map),
                  pl.BlockSpec((d, tn), lambda i, j: (0, j)),
                  pl.BlockSpec((1, tn), lambda i, j: (0, j)),
                  pl.BlockSpec((sub, sub), lambda i, j: (0, 0)),
                  tab_spec, tab_spec, tab_spec],
        out_specs=pl.BlockSpec((tm, tn), lambda i, j: (i, j)),
        out_shape=jax.ShapeDtypeStruct((t, n), BF16),
        scratch_shapes=[pltpu.VMEM((tm, d), BF16)],
        compiler_params=_cparams("parallel", "arbitrary"),
        name="qkv_project",
    )(x, g, shift, scale, w, gain_row, gmat, *rope_tabs)


def rope_tables(seq, head_dim, tile):
    q4 = head_dim // 4
    pos = np.arange(seq)
    inv = ROPE_THETA ** (-np.arange(q4, dtype=np.float64) / q4)
    lane = np.arange(LANES) % head_dim
    axis, half, idx = lane // (2 * q4), (lane // q4) % 2, lane % q4
    p = np.where(axis[None, :] == 0, (pos // GRID_W)[:, None], (pos % GRID_W)[:, None]).astype(np.float64)
    ang = (p.astype(np.float32) * inv.astype(np.float32)[idx][None, :]).astype(np.float32)
    cos, sin = np.cos(ang), np.sin(ang)
    sa = np.where(half[None, :] == 0, -sin, 0.0)
    sb = np.where(half[None, :] == 1, sin, 0.0)
    pad = np.zeros((tile, LANES), np.float32)
    return tuple(jnp.asarray(np.concatenate([a.astype(np.float32), b], axis=0))
                 for a, b in ((cos, pad + 1.0), (sa, pad), (sb, pad)))


def head_group_matrix(width, head_dim):
    r = np.arange(width) // head_dim
    return jnp.asarray((r[:, None] == r[None, :]).astype(np.float32), dtype=BF16)


def _oproj_kernel(a_ref, w_ref, x_ref, gate_ref, o_ref):
    acc = jnp.dot(a_ref[...], w_ref[...], preferred_element_type=F32)
    o_ref[...] = x_ref[...] + gate_ref[0] * acc


def out_project(a, w, x, gate, *, n_rows, seq, n_batch):
    k = a.shape[1]
    d = w.shape[1]
    tm, tn = _proj_row_tile(seq, a.shape[0] - n_batch * seq if a.shape[0] > n_batch * seq else seq), 512
    tiles_per_seq = seq // tm
    return pl.pallas_call(
        _oproj_kernel,
        grid=(n_rows // tm, d // tn),
        in_specs=[pl.BlockSpec((tm, k), lambda i, j: (i, 0)),
                  pl.BlockSpec((k, tn), lambda i, j: (0, j)),
                  pl.BlockSpec((tm, tn), lambda i, j: (i, j)),
                  pl.BlockSpec((1, 1, tn), lambda i, j: (_group_of_tile(i, tiles_per_seq, n_batch), 0, j))],
        out_specs=pl.BlockSpec((tm, tn), lambda i, j: (i, j)),
        out_shape=jax.ShapeDtypeStruct((n_rows, d), F32),
        compiler_params=_cparams("parallel", "parallel"),
        name="out_project",
    )(a, w, x, gate)


def _lane_halves(a, own_half):
    lane = lax.broadcasted_iota(I32, a.shape, 1)
    af = a.astype(F32)
    sw = pltpu.roll(af, LANES // 2, axis=1)
    lo_src, hi_src = (af, sw) if own_half == 0 else (sw, af)
    lo = jnp.where(lane < LANES // 2, lo_src, 0.0).astype(BF16)
    hi = jnp.where(lane >= LANES // 2, hi_src, 0.0).astype(BF16)
    return lo, hi


def _swa_kernel(sink_ref, q_ref, kp_ref, kc_ref, kn_ref, kx_ref, vp_ref, vc_ref, vn_ref, vx_ref, o_ref,
                *, n_blocks, blk, n_ctx):
    i = pl.program_id(1)
    is_lat = i < n_blocks
    kcat = jnp.concatenate([kp_ref[...], kc_ref[...], kn_ref[...], kx_ref[...]], axis=0)
    vcat = jnp.concatenate([vp_ref[...], vc_ref[...], vn_ref[...], vx_ref[...]], axis=0)
    n_keys = 3 * blk + n_ctx
    kj = lax.broadcasted_iota(I32, (n_keys, blk), 0)
    qi = lax.broadcasted_iota(I32, (n_keys, blk), 1)
    in_band = jnp.abs(kj - blk - qi) <= WINDOW
    first_ok = jnp.where(i > 0, 0, blk)
    last_ok = jnp.where(is_lat, jnp.where(i < n_blocks - 1, 3 * blk, 2 * blk), 0)
    valid = (kj >= 3 * blk) | (in_band & (kj >= first_ok) & (kj < last_ok))
    maskadd = jnp.where(valid, 0.0, NEG_BIG).astype(F32)
    rep = SWA_HEADS // SWA_KV_HEADS
    dn = (((1,), (1,)), ((), ()))
    drow = lax.broadcasted_iota(I32, (LANES, n_keys), 0)
    for kv in range(SWA_KV_HEADS):
        ch, half = kv // 2, kv % 2
        klo, khi = _lane_halves(kcat[:, ch * LANES:(ch + 1) * LANES], half)
        vt = vcat[:, ch * LANES:(ch + 1) * LANES].astype(F32).T
        vsw = pltpu.roll(vt, LANES // 2, axis=0)
        lo_src, hi_src = (vt, vsw) if half == 0 else (vsw, vt)
        vlo = jnp.where(drow < LANES // 2, lo_src, 0.0).astype(BF16)
        vhi = jnp.where(drow >= LANES // 2, hi_src, 0.0).astype(BF16)
        for c in range(rep // 2):
            col = (kv * (rep // 2) + c) * LANES
            qc = q_ref[:, col:col + LANES]
            out_t = None
            for kk, vv, head in ((klo, vlo, kv * rep + 2 * c), (khi, vhi, kv * rep + 2 * c + 1)):
                s = lax.dot_general(kk, qc, dn, preferred_element_type=F32) + maskadd
                sink = sink_ref[head] * LOG2_E
                m = jnp.maximum(jnp.max(s, axis=0, keepdims=True), sink)
                p = jnp.exp2(s - m)
                denom = jnp.sum(p, axis=0, keepdims=True) + jnp.exp2(sink - m)
                o = jnp.dot(vv, p.astype(BF16), preferred_element_type=F32) * (1.0 / denom)
                out_t = o if out_t is None else out_t + o
            o_ref[:, col:col + LANES] = out_t.T.astype(o_ref.dtype)


def swa_attention(qkv, sink, *, n_batch, seq, n_ctx, with_ctx_queries):
    blk = WINDOW
    nb = seq // blk
    nbc = n_ctx // blk if with_ctx_queries else 0
    n_q = SWA_HEADS * SWA_HEAD_DIM
    kcol, vcol = n_q // 256, n_q // 256 + 1
    lat_blocks = n_batch * nb
    ctx_row0 = n_batch * seq

    def qrow(b, i):
        return jnp.where(i < nb, b * nb + i, lat_blocks + b * (n_ctx // blk) + (i - nb))

    def win(delta, col):
        return pl.BlockSpec((blk, 256), lambda b, i, s: (b * nb + jnp.clip(i + delta, 0, nb - 1), col))

    def ctx(col):
        return pl.BlockSpec((n_ctx, 256), lambda b, i, s: (ctx_row0 // n_ctx + b, col))

    n_rows = n_batch * (seq + (n_ctx if with_ctx_queries else 0))
    grid_spec = pltpu.PrefetchScalarGridSpec(
        num_scalar_prefetch=1,
        grid=(n_batch, nb + nbc),
        in_specs=[pl.BlockSpec((blk, n_q), lambda b, i, s: (qrow(b, i), 0)),
                  win(-1, kcol), win(0, kcol), win(1, kcol), ctx(kcol),
                  win(-1, vcol), win(0, vcol), win(1, vcol), ctx(vcol)],
        out_specs=pl.BlockSpec((blk, n_q), lambda b, i, s: (qrow(b, i), 0)),
    )
    return pl.pallas_call(
        functools.partial(_swa_kernel, n_blocks=nb, blk=blk, n_ctx=n_ctx),
        grid_spec=grid_spec,
        out_shape=jax.ShapeDtypeStruct((n_rows, n_q), BF16),
        compiler_params=_cparams("parallel", "parallel"),
        name="swa_attention",
    )(sink, qkv, qkv, qkv, qkv, qkv, qkv, qkv, qkv, qkv)


def _ga_kernel(q_ref, kl_ref, vl_ref, kx_ref, vx_ref, o_ref, s_sc, vt_sc, vtx_sc, acc_sc, *, n_lat_blocks, tq, tk):
    i = pl.program_id(2)
    rep = GA_HEADS // GA_KV_HEADS
    hd = GA_HEAD_DIM
    n_ctx = kx_ref.shape[0]
    seq = kl_ref.shape[0]
    dn = (((1,), (1,)), ((), ()))

    @pl.when(i == 0)
    def _():
        vtx_sc[...] = vx_ref[...].astype(F32).T.astype(BF16)
        for c in range(seq // tk):
            vt_sc[c] = vl_ref[c * tk:(c + 1) * tk, :].astype(F32).T.astype(BF16)

    qs = jnp.concatenate([q_ref[:, r * hd:(r + 1) * hd] for r in range(rep)], axis=0)

    n_chunks = jnp.where(i < n_lat_blocks, seq // tk, 0)

    s = lax.dot_general(kx_ref[...], qs, dn, preferred_element_type=F32)
    s_sc[0:n_ctx, :] = s

    def scores(c, m):
        r0 = pl.multiple_of(c * tk, tk)
        s = lax.dot_general(kl_ref[pl.ds(r0, tk), :], qs, dn, preferred_element_type=F32)
        s_sc[pl.ds(n_ctx + r0, tk), :] = s
        return jnp.maximum(m, jnp.max(s, axis=0, keepdims=True))

    m = lax.fori_loop(0, n_chunks, scores, jnp.max(s, axis=0, keepdims=True))

    p = jnp.exp2(s_sc[0:n_ctx, :] - m)
    acc_sc[...] = jnp.dot(vtx_sc[...], p.astype(BF16), preferred_element_type=F32)

    def values(c, l):
        r0 = pl.multiple_of(n_ctx + c * tk, LANES)
        p = jnp.exp2(s_sc[pl.ds(r0, tk), :] - m)
        acc_sc[...] += jnp.dot(vt_sc[c], p.astype(BF16), preferred_element_type=F32)
        return l + jnp.sum(p, axis=0, keepdims=True)

    l = lax.fori_loop(0, n_chunks, values, jnp.sum(p, axis=0, keepdims=True))
    out = (acc_sc[...] * (1.0 / l)).T
    for r in range(rep):
        o_ref[:, r * hd:(r + 1) * hd] = out[r * tq:(r + 1) * tq].astype(o_ref.dtype)


GA_KEY_CHUNK = 2048


def global_attention(qkv, *, n_batch, seq, n_ctx, with_ctx_queries):
    tq, tk = 256, min(GA_KEY_CHUNK, seq)
    hd = GA_HEAD_DIM
    rep = GA_HEADS // GA_KV_HEADS
    n_q = GA_HEADS * hd
    nq = seq // tq
    nqc = n_ctx // tq if with_ctx_queries else 0
    kcol0, vcol0 = n_q // hd, n_q // hd + GA_KV_HEADS
    ctx_blk0 = n_batch * seq // n_ctx

    def qrow(b, i):
        return jnp.where(i < nq, b * nq + i, n_batch * nq + b * (n_ctx // tq) + (i - nq))

    n_rows = n_batch * (seq + (n_ctx if with_ctx_queries else 0))
    return pl.pallas_call(
        functools.partial(_ga_kernel, n_lat_blocks=nq, tq=tq, tk=tk),
        grid=(n_batch, GA_KV_HEADS, nq + nqc),
        in_specs=[pl.BlockSpec((tq, rep * hd), lambda b, k, i: (qrow(b, i), k)),
                  pl.BlockSpec((seq, hd), lambda b, k, i: (b, kcol0 + k)),
                  pl.BlockSpec((seq, hd), lambda b, k, i: (b, vcol0 + k)),
                  pl.BlockSpec((n_ctx, hd), lambda b, k, i: (ctx_blk0 + b, kcol0 + k)),
                  pl.BlockSpec((n_ctx, hd), lambda b, k, i: (ctx_blk0 + b, vcol0 + k))],
        out_specs=pl.BlockSpec((tq, rep * hd), lambda b, k, i: (qrow(b, i), k)),
        out_shape=jax.ShapeDtypeStruct((n_rows, n_q), BF16),
        scratch_shapes=[pltpu.VMEM((n_ctx + seq, rep * tq), F32), pltpu.VMEM((seq // tk, hd, tk), BF16),
                        pltpu.VMEM((hd, n_ctx), BF16), pltpu.VMEM((hd, rep * tq), F32)],
        compiler_params=_cparams("parallel", "parallel", "arbitrary"),
        name="global_attention",
    )(qkv, qkv, qkv, qkv, qkv)


def _ssd_inproj_kernel(x_ref, g_ref, sh_ref, sc_ref, w_ref, o_ref, dt_ref, h_ref, *, n_main_tiles):
    j = pl.program_id(1)

    @pl.when(j == 0)
    def _():
        h_ref[...] = _norm_modulate(x_ref, g_ref, sh_ref, sc_ref).astype(BF16)

    acc = jnp.dot(h_ref[...], w_ref[...], preferred_element_type=F32)

    @pl.when(j < n_main_tiles)
    def _():
        o_ref[...] = acc.astype(o_ref.dtype)

    @pl.when(j == n_main_tiles)
    def _():
        dt_ref[...] = acc[:, :LANES]


SSD_PROJ_COLS = 512


def ssd_in_project(x, g, shift, scale, w_pad, *, n_main, seq, n_batch):
    t, d = x.shape
    tm, tn = _proj_row_tile(seq, t - n_batch * seq), SSD_PROJ_COLS
    tiles_per_seq = seq // tm
    n_main_tiles = n_main // tn

    def mod_map(i, j):
        return (_group_of_tile(i, tiles_per_seq, n_batch), 0, 0)

    return pl.pallas_call(
        functools.partial(_ssd_inproj_kernel, n_main_tiles=n_main_tiles),
        grid=(t // tm, n_main_tiles + 1),
        in_specs=[pl.BlockSpec((tm, d), lambda i, j: (i, 0)),
                  pl.BlockSpec((1, d), lambda i, j: (0, 0)),
                  pl.BlockSpec((1, 1, d), mod_map),
                  pl.BlockSpec((1, 1, d), mod_map),
                  pl.BlockSpec((d, tn), lambda i, j: (0, j))],
        out_specs=[pl.BlockSpec((tm, tn), lambda i, j: (i, jnp.minimum(j, n_main_tiles - 1))),
                   pl.BlockSpec((tm, LANES), lambda i, j: (i, 0))],
        out_shape=[jax.ShapeDtypeStruct((t, n_main), BF16), jax.ShapeDtypeStruct((t, LANES), F32)],
        scratch_shapes=[pltpu.VMEM((tm, d), BF16)],
        compiler_params=_cparams("parallel", "arbitrary"),
        name="ssd_in_project",
    )(x, g, shift, scale, w_pad)


CONV_HALO = 8
CONV_ROWS = 256


def _conv_kernel(u_ref, w_ref, b_ref, o_ref, pad_ref, *, seq_len):
    tc = u_ref.shape[1]
    zeros = jnp.zeros((CONV_HALO, tc), F32)
    pad_ref[0:CONV_HALO, :] = zeros
    pad_ref[CONV_HALO + seq_len:2 * CONV_HALO + seq_len, :] = zeros
    for c in range(seq_len // CONV_ROWS):
        r0 = c * CONV_ROWS
        pad_ref[CONV_HALO + r0:CONV_HALO + r0 + CONV_ROWS, :] = u_ref[r0:r0 + CONV_ROWS, :].astype(F32)
    w = w_ref[...]
    b = b_ref[...]
    for c in range(seq_len // CONV_ROWS):
        r0 = c * CONV_ROWS
        acc = b
        for k in range(SSD_CONV):
            off = CONV_HALO + r0 + k - SSD_CONV // 2
            acc = acc + w[k:k + 1, :] * pad_ref[off:off + CONV_ROWS, :]
        o_ref[r0:r0 + CONV_ROWS, :] = (acc * jax.nn.sigmoid(acc)).astype(o_ref.dtype)


def ssd_conv(main, conv_w, conv_b, *, seq_len, n_seqs, row_block0, col0):
    n_ch = conv_w.shape[1]
    tc = 256
    return pl.pallas_call(
        functools.partial(_conv_kernel, seq_len=seq_len),
        grid=(n_seqs, n_ch // tc),
        in_specs=[pl.BlockSpec((seq_len, tc), lambda b, j: (row_block0 + b, col0 // tc + j)),
                  pl.BlockSpec((SSD_CONV, tc), lambda b, j: (0, j)),
                  pl.BlockSpec((1, tc), lambda b, j: (0, j))],
        out_specs=pl.BlockSpec((seq_len, tc), lambda b, j: (b, j)),
        out_shape=jax.ShapeDtypeStruct((n_seqs * seq_len, n_ch), BF16),
        scratch_shapes=[pltpu.VMEM((seq_len + 2 * CONV_HALO, tc), F32)],
        compiler_params=_cparams("parallel", "parallel"),
        name="ssd_conv",
    )(main, conv_w, conv_b.reshape(1, n_ch))


DIR_LANES = 16


def _prep_kernel(raw_ref, bias_ref, alog_ref, acs_o, eacs_o, dtT_o, acsT_o, wT_o, etot_o):
    ch = SSD_CHUNK
    row = lax.broadcasted_iota(I32, (ch, LANES), 0)
    lane = lax.broadcasted_iota(I32, (ch, LANES), 1)
    is_fwd = (lane % DIR_LANES) < DIR_LANES // 2
    neg_a = -jnp.exp(alog_ref[0])
    for c in range(raw_ref.shape[1] // ch):
        rows = slice(c * ch, (c + 1) * ch)
        v = raw_ref[0, rows, :] + bias_ref[0]
        dt = jnp.maximum(v, 0.0) + jnp.log1p(jnp.exp(-jnp.abs(v)))
        a = dt * neg_a
        fwd, rev = a, a
        s = 1
        while s < ch:
            fwd = fwd + jnp.where(row >= s, pltpu.roll(fwd, s, axis=0), 0.0)
            rev = rev + jnp.where(row < ch - s, pltpu.roll(rev, ch - s, axis=0), 0.0)
            s *= 2
        acs = jnp.where(is_fwd, fwd, rev)
        tot = jnp.where(is_fwd[0:1], fwd[ch - 1:ch, :], rev[0:1, :])
        acs_o[0, rows, :] = acs
        eacs_o[0, rows, :] = jnp.exp(acs)
        dtT_o[0, c] = dt.T[:DIR_LANES]
        acsT_o[0, c] = acs.T[:DIR_LANES]
        wT_o[0, c] = (jnp.exp(tot - acs) * dt).T[:DIR_LANES]
        etot_o[0, c] = jnp.broadcast_to(jnp.exp(tot), (8, LANES))


def ssd_prep(dt_g, bias_g, alog_g):
    n_g, t, _ = dt_g.shape
    ch = SSD_CHUNK
    nck_all = t // ch
    per_step = max(k for k in range(1, 9) if nck_all % k == 0)
    nck = nck_all // per_step
    vec = pl.BlockSpec((1, 1, LANES), lambda g, c: (g, 0, 0))
    rows = pl.BlockSpec((1, per_step * ch, LANES), lambda g, c: (g, c, 0))
    tr = pl.BlockSpec((1, per_step, DIR_LANES, LANES), lambda g, c: (g, c, 0, 0))
    return pl.pallas_call(
        _prep_kernel,
        grid=(n_g, nck),
        in_specs=[rows, vec, vec],
        out_specs=[rows, rows, tr, tr, tr, pl.BlockSpec((1, per_step, 8, LANES), lambda g, c: (g, c, 0, 0))],
        out_shape=[jax.ShapeDtypeStruct((n_g, t, LANES), F32), jax.ShapeDtypeStruct((n_g, t, LANES), F32),
                   jax.ShapeDtypeStruct((n_g, nck_all, DIR_LANES, LANES), F32),
                   jax.ShapeDtypeStruct((n_g, nck_all, DIR_LANES, LANES), F32),
                   jax.ShapeDtypeStruct((n_g, nck_all, DIR_LANES, LANES), F32),
                   jax.ShapeDtypeStruct((n_g, nck_all, 8, LANES), F32)],
        compiler_params=_cparams("parallel", "parallel"),
        name="ssd_prep",
    )(dt_g, bias_g, alog_g)


HEADS_PER_GROUP = 8
PAIRS_PER_GROUP = HEADS_PER_GROUP // 2
GROUP_COLS = HEADS_PER_GROUP * SSD_HEAD_DIM


def _scan_kernel(*refs, n_chunks, emit_state, alias_out):
    (x_ref, b_ref, c_ref, z_ref, acs_ref, eacs_ref, dtT_ref, acsT_ref, wT_ref, etot_ref,
     dskip_ref, ng_ref, init_ref) = refs[:13]
    rest = refs[13 + (1 if alias_out else 0):]
    y_ref = rest[0]
    fin_ref = rest[1] if emit_state else None
    ysc, st = rest[-2], rest[-1]
    ch = SSD_CHUNK
    st[...] = init_ref[0, 0]
    lo = lax.broadcasted_iota(I32, (1, LANES), 1) < LANES // 2
    li = lax.broadcasted_iota(I32, (ch, ch), 0)
    si = lax.broadcasted_iota(I32, (ch, ch), 1)
    dn = (((1,), (1,)), ((), ()))
    for d in range(2):
        mask = (si <= li) if d == 0 else (si >= li)

        def chunk(cc, carry, d=d, mask=mask):
            c = cc if d == 0 else n_chunks - 1 - cc
            rows = pl.ds(pl.multiple_of(c * ch, ch), ch)
            bc, cm = b_ref[rows, :], c_ref[rows, :]
            cb = lax.dot_general(cm, bc, dn, preferred_element_type=F32)
            bt = bc.astype(F32).T
            cf = cm.astype(F32)
            acs_blk = acs_ref[0, rows, :]
            eacs_blk = eacs_ref[0, rows, :]
            et = etot_ref[0, c]
            for pr in range(PAIRS_PER_GROUP):
                cols = slice(pr * LANES, (pr + 1) * LANES)
                x2 = x_ref[rows, cols]
                zero = jnp.zeros_like(x2)
                s2 = st[d * PAIRS_PER_GROUP + pr]
                r_lo = d * HEADS_PER_GROUP + 2 * pr
                y2 = jnp.zeros((ch, LANES), F32)
                snew = s2 * jnp.where(lo, et[0:1, r_lo:r_lo + 1], et[0:1, r_lo + 1:r_lo + 2])
                for hh in range(2):
                    r = r_lo + hh
                    keep = lo if hh == 0 else jnp.logical_not(lo)
                    xm = jnp.where(keep, x2, zero)
                    sm = jnp.where(keep, s2, 0.0).astype(BF16)
                    a_col = jnp.broadcast_to(acs_blk[:, r:r + 1], (ch, ch))
                    lm = jnp.exp(jnp.where(mask, a_col - acsT_ref[0, c, r:r + 1, :], NEG_BIG))
                    m_h = (cb * lm * dtT_ref[0, c, r:r + 1, :]).astype(BF16)
                    c_h = (cf * jnp.exp(a_col)).astype(BF16)
                    y2 = y2 + jnp.dot(m_h, xm, preferred_element_type=F32)
                    y2 = y2 + jnp.dot(c_h, sm, preferred_element_type=F32)
                    b_h = (bt * wT_ref[0, c, r:r + 1, :]).astype(BF16)
                    snew = snew + jnp.dot(b_h, xm, preferred_element_type=F32)
                st[d * PAIRS_PER_GROUP + pr] = snew
                if d == 0:
                    ysc[rows, cols] = y2
                else:
                    ysc[rows, cols] = ysc[rows, cols] + y2
            if d == 1:
                y = ysc[rows, :] + dskip_ref[0] * x_ref[rows, :].astype(F32)
                zf = z_ref[rows, :].astype(F32)
                y = y * (zf * jax.nn.sigmoid(zf))
                y = y * lax.rsqrt(jnp.mean(y * y, axis=-1, keepdims=True) + NORM_EPS) * ng_ref[0]
                y_ref[rows, :] = y.astype(y_ref.dtype)
            return carry

        lax.fori_loop(0, n_chunks, chunk, 0, unroll=2)
    if emit_state:
        fin_ref[0, 0] = st[...]


def ssd_scan(xbc, main, prep, dskip_g, normg_g, init_state, y_prev, *, seq_len, n_batch, row_block0,
             n_rows_total, emit_state):
    acs, eacs, dt_t, acs_t, w_t, etot = prep
    n_g = SSD_GROUPS
    nck = seq_len // SSD_CHUNK
    inner = n_g * GROUP_COLS
    bcol0 = inner // LANES
    ccol0 = bcol0 + n_g * SSD_STATE // LANES
    rb = row_block0

    def tr_spec(h):
        return pl.BlockSpec((1, nck, h, LANES), lambda b, g: (g, rb + b, 0, 0))

    vec = pl.BlockSpec((1, 1, GROUP_COLS), lambda b, g: (g, 0, 0))
    st_spec = pl.BlockSpec((1, 1, 2 * PAIRS_PER_GROUP, SSD_STATE, LANES), lambda b, g: (b, g, 0, 0, 0))
    in_specs = [pl.BlockSpec((seq_len, GROUP_COLS), lambda b, g: (b, g)),
                pl.BlockSpec((seq_len, LANES), lambda b, g: (b, bcol0 + g)),
                pl.BlockSpec((seq_len, LANES), lambda b, g: (b, ccol0 + g)),
                pl.BlockSpec((seq_len, GROUP_COLS), lambda b, g: (rb + b, g)),
                pl.BlockSpec((1, seq_len, LANES), lambda b, g: (g, rb + b, 0)),
                pl.BlockSpec((1, seq_len, LANES), lambda b, g: (g, rb + b, 0)),
                tr_spec(DIR_LANES), tr_spec(DIR_LANES), tr_spec(DIR_LANES), tr_spec(8),
                vec, vec, st_spec]
    args = [xbc, xbc, xbc, main, acs, eacs, dt_t, acs_t, w_t, etot, dskip_g, normg_g, init_state]
    aliases = {}
    if y_prev is not None:
        in_specs.append(pl.BlockSpec(memory_space=pl.ANY))
        args.append(y_prev)
        aliases = {len(args) - 1: 0}
    out_specs = [pl.BlockSpec((seq_len, GROUP_COLS), lambda b, g: (rb + b, g))]
    out_shape = [jax.ShapeDtypeStruct((n_rows_total, inner), BF16)]
    if emit_state:
        out_specs.append(st_spec)
        out_shape.append(jax.ShapeDtypeStruct(init_state.shape, F32))
    return pl.pallas_call(
        functools.partial(_scan_kernel, n_chunks=nck, emit_state=emit_state, alias_out=y_prev is not None),
        grid=(n_batch, n_g),
        in_specs=in_specs,
        out_specs=out_specs,
        out_shape=out_shape,
        scratch_shapes=[pltpu.VMEM((seq_len, GROUP_COLS), F32),
                        pltpu.VMEM((2 * PAIRS_PER_GROUP, SSD_STATE, LANES), F32)],
        input_output_aliases=aliases,
        compiler_params=_cparams("parallel", "parallel"),
        name="ssd_scan",
    )(*args)


def _router_kernel(x_ref, g_ref, sh_ref, sc_ref, whi_ref, wlo_ref, b_ref, tri_ref,
                   h_hbm, meta_o, gate_o, cnt_o, carry, hbuf, hsem, *, n_tiles):
    i = pl.program_id(0)
    slot = i % 2
    tm = x_ref.shape[0]

    @pl.when(i == 0)
    def _():
        carry[...] = jnp.zeros_like(carry)

    def h_copies(tile, sl):
        rows = pl.ds(tile * tm, tm)
        return [pltpu.make_async_copy(hbuf.at[sl, :, c * LANES:(c + 1) * LANES], h_hbm.at[rows, c, :], hsem.at[sl])
                for c in range(ROW_SLABS)]

    h = _norm_modulate(x_ref, g_ref, sh_ref, sc_ref)
    hbuf[slot] = h

    @pl.when(i >= 1)
    def _():
        for cp in h_copies(i - 1, 1 - slot):
            cp.wait()

    for cp in h_copies(i, slot):
        cp.start()

    @pl.when(i == n_tiles - 1)
    def _():
        for cp in h_copies(i, slot):
            cp.wait()

    h_hi = h.astype(BF16)
    h_lo = (h - h_hi.astype(F32)).astype(BF16)
    whi = whi_ref[...]
    logits = (jnp.dot(h_hi, whi, preferred_element_type=F32) + jnp.dot(h_lo, whi, preferred_element_type=F32)
              + jnp.dot(h_hi, wlo_ref[...], preferred_element_type=F32) + b_ref[...])
    tm = logits.shape[0]
    lane = lax.broadcasted_iota(I32, (tm, LANES), 1).astype(F32)
    work = logits
    vals, idxs = [], []
    for _ in range(TOP_K):
        m = jnp.max(work, axis=1, keepdims=True)
        idx = jnp.min(jnp.where(work == m, lane, float(LANES)), axis=1, keepdims=True)
        vals.append(m)
        idxs.append(idx)
        work = jnp.where(lane == idx, -jnp.inf, work)
    es = [jnp.exp(v - vals[0]) for v in vals]
    inv = 1.0 / (es[0] + es[1] + es[2] + es[3])
    multihot = sum(jnp.where(lane == idx, 1.0, 0.0) for idx in idxs)
    before = jnp.dot(tri_ref[...], multihot.astype(BF16), preferred_element_type=F32) + carry[...]
    meta = jnp.zeros((tm, LANES), F32)
    gates = jnp.zeros((tm, LANES), F32)
    for k in range(TOP_K):
        rank = jnp.sum(jnp.where(lane == idxs[k], before, 0.0), axis=1, keepdims=True)
        meta = meta + jnp.where(lane == float(k), idxs[k], 0.0) + jnp.where(lane == float(TOP_K + k), rank, 0.0)
        gates = gates + jnp.where(lane == float(k), es[k] * inv, 0.0)
    meta_o[...] = meta.astype(I32)
    gate_o[...] = gates
    carry[...] = carry[...] + jnp.sum(multihot, axis=0, keepdims=True)
    cnt_o[...] = carry[...]


def moe_router(x, g, shift, scale, w_hi, w_lo, b_pad, *, n_rows, seq, n_batch):
    d = x.shape[1]
    tm = ROW_TILE
    tiles_per_seq = seq // tm
    tri = jnp.asarray(np.tril(np.ones((tm, tm), np.float32), -1), dtype=BF16)

    def mod_map(i):
        return (_group_of_tile(i, tiles_per_seq, n_batch), 0, 0)

    full = lambda shape: pl.BlockSpec(shape, lambda i: (0,) * len(shape))
    return pl.pallas_call(
        functools.partial(_router_kernel, n_tiles=n_rows // tm),
        grid=(n_rows // tm,),
        in_specs=[pl.BlockSpec((tm, d), lambda i: (i, 0)), full((1, d)),
                  pl.BlockSpec((1, 1, d), mod_map), pl.BlockSpec((1, 1, d), mod_map),
                  full((d, LANES)), full((d, LANES)), full((1, LANES)), full((tm, tm))],
        out_specs=[pl.BlockSpec(memory_space=pl.ANY),
                   pl.BlockSpec((tm, LANES), lambda i: (i, 0)),
                   pl.BlockSpec((tm, LANES), lambda i: (i, 0)), full((1, LANES))],
        out_shape=[jax.ShapeDtypeStruct((n_rows, ROW_SLABS, LANES), F32),
                   jax.ShapeDtypeStruct((n_rows, LANES), I32),
                   jax.ShapeDtypeStruct((n_rows, LANES), F32), jax.ShapeDtypeStruct((1, LANES), F32)],
        scratch_shapes=[pltpu.VMEM((1, LANES), F32), pltpu.VMEM((2, tm, d), F32), pltpu.SemaphoreType.DMA((2,))],
        compiler_params=_cparams("arbitrary"),
        name="moe_router",
    )(x, g, shift, scale, w_hi, w_lo, b_pad, tri)


ROW_SLABS = 16
DMA_UNROLL = 8
DISPATCH_TILE = 256
FF_CHUNK = 256
OUT_CHUNK = 512


def _rows_to_slabs(ref, val):
    for c in range(ROW_SLABS):
        ref[:, c, :] = val[:, c * LANES:(c + 1) * LANES]


def _slabs_to_rows(ref, rows):
    return jnp.concatenate([ref[rows, c, :] for c in range(ROW_SLABS)], axis=1)


def _wait_rows(n, like_hbm, sem):
    pltpu.make_async_copy(like_hbm.at[pl.ds(0, n)], like_hbm.at[pl.ds(0, n)], sem).wait()


def _dispatch_kernel(tail_start_ref, tail_len_ref, nact_ref, dst_ref, h_ref, xs_hbm, zbuf, sem, zsem,
                     *, n_exp, n_blocks):
    s = pl.program_id(0)
    tm = h_ref.shape[0]

    @pl.when(s == 0)
    def _():
        zbuf[...] = jnp.zeros_like(zbuf)
        half = MOE_BLOCK // 2

        def unused(b, wait):
            for part in range(2):
                cp = pltpu.make_async_copy(zbuf, xs_hbm.at[pl.ds(b * MOE_BLOCK + part * half, half)], zsem)
                cp.wait() if wait else cp.start()

        lax.fori_loop(nact_ref[0], n_blocks, lambda b, c: (unused(b, False), c)[1], 0)
        lax.fori_loop(nact_ref[0], n_blocks, lambda b, c: (unused(b, True), c)[1], 0)
        sizes = [1 << b for b in range(MOE_BLOCK.bit_length() - 1)]

        def tails(e, wait):
            start, length = tail_start_ref[e], tail_len_ref[e]
            for size in sizes:
                @pl.when((length & size) != 0)
                def _(size=size):
                    pos = start + (length & ~(2 * size - 1))
                    cp = pltpu.make_async_copy(zbuf.at[pl.ds(0, size)], xs_hbm.at[pl.ds(pos, size)], zsem)
                    cp.wait() if wait else cp.start()

        lax.fori_loop(0, n_exp, lambda e, c: (tails(e, False), c)[1], 0)
        lax.fori_loop(0, n_exp, lambda e, c: (tails(e, True), c)[1], 0)

    def issue(it, c):
        for u in range(DMA_UNROLL):
            tok = it * (DMA_UNROLL // TOP_K) + u // TOP_K
            pltpu.make_async_copy(h_ref.at[pl.ds(tok, 1)], xs_hbm.at[pl.ds(dst_ref[0, 0, it * DMA_UNROLL + u], 1)],
                                  sem).start(priority=u % 2)
        return c

    lax.fori_loop(0, TOP_K * tm // DMA_UNROLL, issue, 0)
    _wait_rows(TOP_K * tm, xs_hbm, sem)


def moe_dispatch(tail_start, tail_len, n_active, dest, h3, *, n_blocks):
    n_rows = h3.shape[0]
    tm = DISPATCH_TILE
    n_exp = tail_start.shape[0]
    grid_spec = pltpu.PrefetchScalarGridSpec(
        num_scalar_prefetch=3,
        grid=(n_rows // tm,),
        in_specs=[pl.BlockSpec((1, 1, TOP_K * tm), lambda s, a, b, c: (s, 0, 0), memory_space=pltpu.SMEM),
                  pl.BlockSpec((tm, ROW_SLABS, LANES), lambda s, a, b, c: (s, 0, 0))],
        out_specs=pl.BlockSpec(memory_space=pl.ANY),
        scratch_shapes=[pltpu.VMEM((MOE_BLOCK // 2, ROW_SLABS, LANES), F32),
                        pltpu.SemaphoreType.DMA(()), pltpu.SemaphoreType.DMA(())],
    )
    return pl.pallas_call(
        functools.partial(_dispatch_kernel, n_exp=n_exp, n_blocks=n_blocks),
        grid_spec=grid_spec,
        out_shape=jax.ShapeDtypeStruct((n_blocks * MOE_BLOCK, ROW_SLABS, LANES), F32),
        compiler_params=_cparams("arbitrary"),
        name="moe_dispatch",
    )(tail_start, tail_len, n_active, dest.reshape(n_rows // tm, 1, TOP_K * tm), h3)


def _ffn_kernel(bexp_ref, nact_ref, xs_hbm, wgu_ref, bgu_ref, wd_ref, bd_ref, o_hbm, xbuf, ybuf, sem, osem):
    s = pl.program_id(0)
    n_steps = pl.num_programs(0)
    slot = s % 2
    nact = nact_ref[0]

    def block_copies(blk, sl):
        rows = pl.ds(blk * MOE_BLOCK, MOE_BLOCK)
        return [pltpu.make_async_copy(xs_hbm.at[rows, c, :], xbuf.at[sl, :, c * LANES:(c + 1) * LANES], sem.at[sl])
                for c in range(ROW_SLABS)]

    def out_copies(blk, sl):
        rows = pl.ds(blk * MOE_BLOCK, MOE_BLOCK)
        return [pltpu.make_async_copy(ybuf.at[sl, :, c * LANES:(c + 1) * LANES], o_hbm.at[rows, c, :], osem.at[sl])
                for c in range(ROW_SLABS)]

    @pl.when(jnp.logical_and(s == 0, nact > 0))
    def _():
        for cp in block_copies(0, 0):
            cp.start()

    @pl.when(s + 1 < nact)
    def _():
        for cp in block_copies(s + 1, 1 - slot):
            cp.start()

    @pl.when(s < nact)
    def _():
        for cp in block_copies(s, slot):
            cp.wait()
        x = xbuf[slot].astype(BF16)
        ff = EXPERT_FF

        def swiglu(glu, lin):
            glu = jnp.minimum(glu, SWIGLU_LIMIT)
            lin = jnp.clip(lin, -SWIGLU_LIMIT, SWIGLU_LIMIT)
            return (glu * jax.nn.sigmoid(SWIGLU_ALPHA * glu) * (lin + 1.0)).astype(BF16)

        acts = []
        full = ff // FF_CHUNK * FF_CHUNK
        for lo in range(0, full, FF_CHUNK):
            hi = lo + FF_CHUNK
            glu = jnp.dot(x, wgu_ref[0, 0, :, lo:hi], preferred_element_type=F32) + bgu_ref[0, 0, :, lo:hi]
            lin = (jnp.dot(x, wgu_ref[0, 0, :, ff + lo:ff + hi], preferred_element_type=F32)
                   + bgu_ref[0, 0, :, ff + lo:ff + hi])
            acts.append(swiglu(glu, lin))
        if full < ff:
            rem = ff - full
            w_rem = jnp.concatenate([wgu_ref[0, 0, :, full:ff], wgu_ref[0, 0, :, ff + full:2 * ff]], axis=1)
            b_rem = jnp.concatenate([bgu_ref[0, 0, :, full:ff], bgu_ref[0, 0, :, ff + full:2 * ff]], axis=1)
            gl = jnp.dot(x, w_rem, preferred_element_type=F32) + b_rem
            acts.append(swiglu(gl[:, :rem], gl[:, rem:]))
        act = jnp.concatenate(acts, axis=1)
        for n0 in range(0, ybuf.shape[2], OUT_CHUNK):
            y = jnp.dot(act, wd_ref[0, 0, :, n0:n0 + OUT_CHUNK], preferred_element_type=F32)
            ybuf[slot, :, n0:n0 + OUT_CHUNK] = y + bd_ref[0, 0, :, n0:n0 + OUT_CHUNK]

    @pl.when(s >= nact)
    def _():
        ybuf[slot] = jnp.zeros(ybuf.shape[1:], F32)

    @pl.when(s >= 1)
    def _():
        for cp in out_copies(s - 1, 1 - slot):
            cp.wait()

    for cp in out_copies(s, slot):
        cp.start()

    @pl.when(s == n_steps - 1)
    def _():
        for cp in out_copies(s, slot):
            cp.wait()


def moe_ffn_blocks(block_exp, n_active, xs, w_gu, b_gu, w_down, b_down, *, layer):
    n_blocks = xs.shape[0] // MOE_BLOCK
    d, ff2 = w_gu.shape[2], w_gu.shape[3]
    wmap = lambda s, be, na: (layer, be[s], 0, 0)
    grid_spec = pltpu.PrefetchScalarGridSpec(
        num_scalar_prefetch=2,
        grid=(n_blocks,),
        in_specs=[pl.BlockSpec(memory_space=pl.ANY),
                  pl.BlockSpec((1, 1, d, ff2), wmap), pl.BlockSpec((1, 1, 1, ff2), wmap),
                  pl.BlockSpec((1, 1, ff2 // 2, d), wmap), pl.BlockSpec((1, 1, 1, d), wmap)],
        out_specs=pl.BlockSpec(memory_space=pl.ANY),
        scratch_shapes=[pltpu.VMEM((2, MOE_BLOCK, d), F32), pltpu.VMEM((2, MOE_BLOCK, d), F32),
                        pltpu.SemaphoreType.DMA((2,)), pltpu.SemaphoreType.DMA((2,))],
    )
    return pl.pallas_call(
        _ffn_kernel,
        grid_spec=grid_spec,
        out_shape=jax.ShapeDtypeStruct(xs.shape, F32),
        compiler_params=_cparams("arbitrary"),
        name="moe_ffn",
    )(block_exp, n_active, xs, w_gu, b_gu, w_down, b_down)


COMBINE_TILE = 128


def _combine_kernel(d0_ref, d1_ref, g_ref, rows_hbm, x_hbm, mod_ref, o_hbm, gbuf, xbuf, obuf, gsem, xsem, osem,
                    *, n_tiles):
    s = pl.program_id(0)
    slot = s % 2
    tm = COMBINE_TILE

    def gather(idx_ref, sl):
        def issue(it, c):
            for u in range(DMA_UNROLL):
                r = it * DMA_UNROLL + u
                pltpu.make_async_copy(rows_hbm.at[pl.ds(idx_ref[0, 0, r], 1)], gbuf.at[sl, pl.ds(r, 1)],
                                      gsem.at[sl]).start(priority=u % 2)
            return c

        lax.fori_loop(0, TOP_K * tm // DMA_UNROLL, issue, 0)

    def x_copies(tile, sl):
        rows = pl.ds(tile * tm, tm)
        return [pltpu.make_async_copy(x_hbm.at[rows, c * LANES:(c + 1) * LANES], xbuf.at[sl, :, c, :], xsem.at[sl])
                for c in range(ROW_SLABS)]

    def out_copies(tile, sl):
        rows = pl.ds(tile * tm, tm)
        return [pltpu.make_async_copy(obuf.at[sl, :, c, :], o_hbm.at[rows, c * LANES:(c + 1) * LANES], osem.at[sl])
                for c in range(ROW_SLABS)]

    @pl.when(s == 0)
    def _():
        gather(d0_ref, 0)
        for cp in x_copies(0, 0):
            cp.start()

    @pl.when(s + 1 < n_tiles)
    def _():
        gather(d1_ref, 1 - slot)
        for cp in x_copies(s + 1, 1 - slot):
            cp.start()

    _wait_rows(TOP_K * tm, rows_hbm, gsem.at[slot])
    for cp in x_copies(s, slot):
        cp.wait()
    mod = mod_ref[0]

    def token(t, c):
        acc = g_ref[0, 0, TOP_K * t] * gbuf[slot, t]
        for k in range(1, TOP_K):
            acc = acc + g_ref[0, 0, TOP_K * t + k] * gbuf[slot, k * tm + t]
        obuf[slot, t] = xbuf[slot, t] + mod * acc
        return c

    lax.fori_loop(0, tm, token, 0, unroll=4)

    @pl.when(s >= 1)
    def _():
        for cp in out_copies(s - 1, 1 - slot):
            cp.wait()

    for cp in out_copies(s, slot):
        cp.start()

    @pl.when(s == n_tiles - 1)
    def _():
        for cp in out_copies(s, slot):
            cp.wait()


def moe_combine(dest_tiles, out_rows, x, gates, gate_mod, *, n_rows, seq, n_batch):
    d = x.shape[1]
    tm = COMBINE_TILE
    n_tiles = n_rows // tm
    tiles_per_seq = seq // tm
    gate_tiles = gates[:n_rows, :TOP_K].reshape(n_tiles, 1, TOP_K * tm)
    mod_slabs = gate_mod.reshape(gate_mod.shape[0], ROW_SLABS, LANES)

    def smem_spec(ahead):
        return pl.BlockSpec((1, 1, TOP_K * tm), lambda s: (jnp.minimum(s + ahead, n_tiles - 1), 0, 0),
                            memory_space=pltpu.SMEM)

    slab_buf = pltpu.VMEM((2, tm, ROW_SLABS, LANES), F32)
    return pl.pallas_call(
        functools.partial(_combine_kernel, n_tiles=n_tiles),
        grid=(n_tiles,),
        in_specs=[smem_spec(0), smem_spec(1), smem_spec(0),
                  pl.BlockSpec(memory_space=pl.ANY), pl.BlockSpec(memory_space=pl.ANY),
                  pl.BlockSpec((1, ROW_SLABS, LANES), lambda s: (_group_of_tile(s, tiles_per_seq, n_batch), 0, 0))],
        out_specs=pl.BlockSpec(memory_space=pl.ANY),
        out_shape=jax.ShapeDtypeStruct((n_rows, d), F32),
        scratch_shapes=[pltpu.VMEM((2, TOP_K * tm, ROW_SLABS, LANES), F32), slab_buf, slab_buf,
                        pltpu.SemaphoreType.DMA((2,)), pltpu.SemaphoreType.DMA((2,)), pltpu.SemaphoreType.DMA((2,))],
        compiler_params=_cparams("arbitrary"),
        name="moe_combine",
    )(dest_tiles, dest_tiles, gate_tiles, out_rows, x, mod_slabs)


def moe_layer(x, g, shift, scale, gate_mod, w_router, b_router, w_gu, b_gu, w_down, b_down,
              *, layer, n_rows, seq, n_batch):
    n_exp = w_router.shape[1]
    w_pad = jnp.pad(w_router, ((0, 0), (0, LANES - n_exp)))
    w_hi = w_pad.astype(BF16)
    w_lo = (w_pad - w_hi.astype(F32)).astype(BF16)
    b_pad = jnp.concatenate([b_router, jnp.full((LANES - n_exp,), NEG_BIG, F32)])[None, :]
    h3, meta, gates, cnt = moe_router(x, g, shift, scale, w_hi, w_lo, b_pad, n_rows=n_rows, seq=seq, n_batch=n_batch)
    top_idx, rank = meta[:, :TOP_K], meta[:, TOP_K:2 * TOP_K]
    counts = cnt[0, :n_exp].astype(I32)
    padded = (counts + MOE_BLOCK - 1) // MOE_BLOCK * MOE_BLOCK
    pad_end = jnp.cumsum(padded)
    pad_start = pad_end - padded
    onehot = top_idx[:, :, None] == jnp.arange(n_exp, dtype=I32)[None, None, :]
    dest = jnp.sum(jnp.where(onehot, pad_start[None, None, :], 0), axis=-1) + rank
    n_blocks = n_rows * TOP_K // MOE_BLOCK + n_exp
    block_row0 = jnp.arange(n_blocks, dtype=I32) * MOE_BLOCK
    block_exp = jnp.minimum(jnp.sum((pad_end[None, :] <= block_row0[:, None]).astype(I32), axis=1), n_exp - 1)
    n_active = (pad_end[-1:] // MOE_BLOCK).astype(I32)
    xs = moe_dispatch(pad_start + counts, padded - counts, n_active, dest, h3, n_blocks=n_blocks)
    out_rows = moe_ffn_blocks(block_exp, n_active, xs, w_gu, b_gu, w_down, b_down, layer=layer)
    tm = COMBINE_TILE
    dest_tiles = dest.reshape(n_rows // tm, tm, TOP_K).transpose(0, 2, 1).reshape(n_rows // tm, 1, TOP_K * tm)
    return moe_combine(dest_tiles, out_rows, x, gates, gate_mod, n_rows=n_rows, seq=seq, n_batch=n_batch)


def attention_layer(x, g, shift, scale, gate_mod, w_in, q_gain, k_gain, w_out, sink, *, n_heads, n_kv, head_dim,
                    seq, n_ctx, n_batch, need_ctx):
    n_qk = (n_heads + n_kv) * head_dim
    q_scale = head_dim ** -0.5 * LOG2_E
    gain_row = jnp.concatenate([jnp.tile(q_gain, n_heads) * q_scale, jnp.tile(k_gain, n_kv),
                                jnp.ones((n_kv * head_dim,), F32)])[None, :]
    qkv = qkv_project(x, g, shift, scale, w_in.astype(BF16), gain_row, head_group_matrix(256, head_dim),
                      rope_tables(seq, head_dim, _proj_row_tile(seq, x.shape[0] - n_batch * seq)),
                      n_qk_cols=n_qk, head_dim=head_dim, seq=seq, n_batch=n_batch)
    if sink is not None:
        att = swa_attention(qkv, sink, n_batch=n_batch, seq=seq, n_ctx=n_ctx, with_ctx_queries=need_ctx)
    else:
        att = global_attention(qkv, n_batch=n_batch, seq=seq, n_ctx=n_ctx, with_ctx_queries=need_ctx)
    n_rows = n_batch * (seq + (n_ctx if need_ctx else 0))
    return out_project(att, w_out.astype(BF16), x, gate_mod, n_rows=n_rows, seq=seq, n_batch=n_batch)


def ssd_layer(x, g, shift, scale, gate_mod, w_in, conv_w, conv_b, dt_bias, a_log, d_skip, norm_g, w_out,
              *, seq, n_ctx, n_batch):
    t, d = x.shape
    n_g = SSD_GROUPS
    n_heads = d_skip.shape[0]
    inner = n_heads * SSD_HEAD_DIM
    n_main = inner + conv_w.shape[1]
    w_pad = jnp.pad(w_in, ((0, 0), (0, n_main + SSD_PROJ_COLS - w_in.shape[1]))).astype(BF16)
    main, dt_raw = ssd_in_project(x, g, shift, scale, w_pad, n_main=n_main, seq=seq, n_batch=n_batch)

    def to_groups(a):
        lead = a.shape[:-1]
        a = a.reshape(lead + (2, n_g, HEADS_PER_GROUP))
        a = jnp.moveaxis(a, -2, 0).reshape((n_g,) + lead + (DIR_LANES,))
        return jnp.pad(a, [(0, 0)] * (a.ndim - 1) + [(0, LANES - DIR_LANES)])

    prep = ssd_prep(to_groups(dt_raw[:, :2 * n_heads]), to_groups(dt_bias.reshape(1, -1)),
                    to_groups(a_log.reshape(1, -1)))
    dskip_g = jnp.repeat(d_skip, SSD_HEAD_DIM).reshape(n_g, 1, GROUP_COLS)
    normg_g = norm_g.reshape(n_g, 1, GROUP_COLS)
    n_lat = n_batch * seq
    xbc_c = ssd_conv(main, conv_w, conv_b, seq_len=n_ctx, n_seqs=n_batch, row_block0=n_lat // n_ctx, col0=inner)
    xbc_l = ssd_conv(main, conv_w, conv_b, seq_len=seq, n_seqs=n_batch, row_block0=0, col0=inner)
    zero_state = jnp.zeros((n_batch, n_g, 2 * PAIRS_PER_GROUP, SSD_STATE, LANES), F32)
    y = jnp.zeros((t, inner), BF16)
    y, ctx_state = ssd_scan(xbc_c, main, prep, dskip_g, normg_g, zero_state, y, seq_len=n_ctx, n_batch=n_batch,
                            row_block0=n_lat // n_ctx, n_rows_total=t, emit_state=True)
    (y,) = ssd_scan(xbc_l, main, prep, dskip_g, normg_g, ctx_state, y, seq_len=seq, n_batch=n_batch,
                    row_block0=0, n_rows_total=t, emit_state=False)
    return out_project(y, w_out.astype(BF16), x, gate_mod, n_rows=t, seq=seq, n_batch=n_batch)


def kernel(x, c, ctx, c_ctx, ada_w, ada_b, norm_mix, norm_ffn, swa_w_in, swa_q_norm, swa_k_norm, swa_sink,
           swa_w_out, ssd_w_in, ssd_conv_w, ssd_conv_b, ssd_dt_bias, ssd_a_log, ssd_d, ssd_norm, ssd_w_out,
           ga_w_in, ga_q_norm, ga_k_norm, ga_w_out, moe_w_router, moe_b_router, moe_w_gate_up, moe_b_gate_up,
           moe_w_down, moe_b_down):
    n_batch, seq, d = x.shape
    n_ctx = ctx.shape[1]
    n_lat = n_batch * seq
    xs = jnp.concatenate([x.reshape(n_lat, d), ctx.reshape(n_batch * n_ctx, d)], axis=0)
    cvec = jnp.concatenate([c, c_ctx[None, :], jnp.zeros((8 - n_batch - 1, d), F32)], axis=0)
    mods = adaln_all(cvec, ada_w, ada_b)[:, :n_batch + 1].reshape(DEPTH, n_batch + 1, 6, 1, d)
    dims = dict(seq=seq, n_ctx=n_ctx, n_batch=n_batch)
    w_gu_bf, w_down_bf = moe_w_gate_up.astype(BF16), moe_w_down.astype(BF16)
    for i in range(DEPTH):
        kind, j = i % N_MIXERS, i // N_MIXERS
        need_ctx = i < DEPTH - 1
        m = [mods[i, :, k] for k in range(6)]
        g_mix = norm_mix[i][None, :]
        if kind == 0:
            xs = attention_layer(xs, g_mix, m[0], m[1], m[2], swa_w_in[j], swa_q_norm[j], swa_k_norm[j],
                                 swa_w_out[j], swa_sink[j], n_heads=SWA_HEADS, n_kv=SWA_KV_HEADS,
                                 head_dim=SWA_HEAD_DIM, need_ctx=need_ctx, **dims)
        elif kind == 1:
            xs = ssd_layer(xs, g_mix, m[0], m[1], m[2], ssd_w_in[j], ssd_conv_w[j], ssd_conv_b[j], ssd_dt_bias[j],
                           ssd_a_log[j], ssd_d[j], ssd_norm[j], ssd_w_out[j], **dims)
        else:
            xs = attention_layer(xs, g_mix, m[0], m[1], m[2], ga_w_in[j], ga_q_norm[j], ga_k_norm[j], ga_w_out[j],
                                 None, n_heads=GA_HEADS, n_kv=GA_KV_HEADS, head_dim=GA_HEAD_DIM,
                                 need_ctx=need_ctx, **dims)
        n_rows = n_batch * (seq + (n_ctx if need_ctx else 0))
        xs = moe_layer(xs, norm_ffn[i][None, :], m[3], m[4], m[5], moe_w_router[i], moe_b_router[i],
                       w_gu_bf, moe_b_gate_up[:, :, None, :], w_down_bf, moe_b_down[:, :, None, :],
                       layer=i, n_rows=n_rows, seq=seq, n_batch=n_batch)
    return xs[:n_lat].reshape(n_batch, seq, d)
```
